```python
import math
import jax, jax.numpy as jnp
from jax import lax
import numpy as np

D_MODEL = 1024
BATCH = 4
SEQ = 4096
DEPTH = 2
DEC_BATCH = 32
DEC_SEQ = 32
PAST_LEN = 1024

CHUNK = 64
Q_BLOCK = 128
HEAD_DIM = 64
N_GROUPS = 4
GROUP_WIDTH = D_MODEL // N_GROUPS
N_HEADS_G = GROUP_WIDTH // HEAD_DIM
DIFF_HALF = HEAD_DIM // 2
BAND_CHUNKS = 8
BAND_ROWS = BAND_CHUNKS * CHUNK
REL_CLIP = 128
CONV_W = 4
D_FF = ((8 * D_MODEL + 3 * 256 - 1) // (3 * 256)) * 256
FOX_FORGET_BIAS = 4.0
EPS = 1e-6
IN_SPLIT_SIZES = (GROUP_WIDTH, GROUP_WIDTH, GROUP_WIDTH, N_HEADS_G,
                  GROUP_WIDTH, GROUP_WIDTH, GROUP_WIDTH,
                  2 * GROUP_WIDTH, GROUP_WIDTH, N_HEADS_G, N_HEADS_G, GROUP_WIDTH,
                  GROUP_WIDTH, GROUP_WIDTH, GROUP_WIDTH)
N_IN = 13 * GROUP_WIDTH + 3 * N_HEADS_G

kernel_name = 'hybrid_streaming_encoder_step'


def _rms(x, g):
    x32 = x.astype(jnp.float32)
    y = x32 * lax.rsqrt(jnp.mean(x32 * x32, axis=-1, keepdims=True) + EPS)
    return (y * g.astype(jnp.float32)).astype(x.dtype)


def _heads(t):
    return t.reshape(t.shape[:-1] + (N_HEADS_G, HEAD_DIM))


def _split_cols(z):
    cuts = np.cumsum(IN_SPLIT_SIZES)[:-1].tolist()
    return jnp.split(z, cuts, axis=-1)


def _alibi_slopes(n):
    return 2.0 ** (-8.0 * (jnp.arange(n, dtype=jnp.float32) + 1.0) / n)


def _to_blocks(a, size):
    b, t = a.shape[:2]
    return jnp.moveaxis(a.reshape((b, t // size, size) + a.shape[2:]), 1, 0)


def _from_blocks(o):
    return jnp.moveaxis(o, 0, 1).reshape((o.shape[1], o.shape[0] * o.shape[2]) + o.shape[3:])


def _fox_attend(q, k, v, cum_q, cum_k, q_pos, k_pos):
    s = jnp.einsum('bqhd,bkhd->bhqk', q, k).astype(jnp.float32) * HEAD_DIM ** -0.5
    s = s + jnp.swapaxes(cum_q, 1, 2)[..., :, None] - jnp.swapaxes(cum_k, 1, 2)[..., None, :]
    s = jnp.where(k_pos[None, :] <= q_pos[:, None], s, -jnp.inf)
    p = jax.nn.softmax(s, axis=-1).astype(v.dtype)
    return jnp.einsum('bhqk,bkhd->bqhd', p, v)


def _diff_attend(q, k, v, lam, q_pos, k_pos):
    s = jnp.einsum('bqhjd,bkhjd->bhjqk', q, k).astype(jnp.float32) * DIFF_HALF ** -0.5
    dist = jnp.abs(q_pos[:, None] - k_pos[None, :]).astype(jnp.float32)
    s = s - _alibi_slopes(N_HEADS_G)[:, None, None, None] * dist
    seen = (k_pos[None, :] // CHUNK) <= (q_pos[:, None] // CHUNK)
    s = jnp.where(seen, s, -jnp.inf)
    p = jax.nn.softmax(s, axis=-1)
    a = (p[:, :, 0] - lam * p[:, :, 1]).astype(v.dtype)
    return jnp.einsum('bhqk,bkhd->bqhd', a, v)


def _band_attend(q, k, v, rel, valid, table):
    s = jnp.einsum('bcqhd,bckhd->bchqk', q, k).astype(jnp.float32) * HEAD_DIM ** -0.5
    s = s + table.astype(jnp.float32)[:, jnp.clip(rel, -REL_CLIP, REL_CLIP) + REL_CLIP]
    s = jnp.where(valid[None, :, None, None, :], s, -jnp.inf)
    p = jax.nn.softmax(s, axis=-1).astype(v.dtype)
    return jnp.einsum('bchqk,bckhd->bcqhd', p, v)


def _mlstm_chunk(carry, inp):
    C, n, m = carry
    q, k, v, ig, lf = inp
    L = q.shape[1]
    F = jnp.swapaxes(jnp.cumsum(lf, axis=1), 1, 2)
    igh = jnp.swapaxes(ig, 1, 2)
    causal = jnp.tril(jnp.ones((L, L), bool))
    logw = jnp.where(causal, F[..., :, None] - F[..., None, :] + igh[..., None, :], -jnp.inf)
    logb = F + m[..., None]
    m_t = jnp.maximum(logb, logw.max(-1))
    q32, k32, v32 = q.astype(jnp.float32), k.astype(jnp.float32), v.astype(jnp.float32)
    w = jnp.exp(logw - m_t[..., None]) * jnp.einsum('bthd,bshd->bhts', q32, k32)
    a = jnp.exp(logb - m_t)
    num = jnp.einsum('bhts,bshd->bhtd', w, v32) + a[..., None] * jnp.einsum('bthk,bhkv->bhtv', q32, C)
    den = w.sum(-1) + a * jnp.einsum('bthk,bhk->bht', q32, n)
    h = num / jnp.maximum(jnp.abs(den), jnp.exp(-m_t))[..., None]
    F_end = F[..., -1]
    log_end = F_end[..., None] - F + igh
    m_new = jnp.maximum(F_end + m, log_end.max(-1))
    w_end = jnp.exp(log_end - m_new[..., None])
    decay = jnp.exp(F_end + m - m_new)
    C_new = decay[..., None, None] * C + jnp.einsum('bhs,bshk,bshv->bhkv', w_end, k32, v32)
    n_new = decay[..., None] * n + jnp.einsum('bhs,bshk->bhk', w_end, k32)
    return (C_new, n_new, m_new), jnp.swapaxes(h, 1, 2)


def _causal_conv(u, buf, w, b):
    t = u.shape[1]
    up = jnp.concatenate([buf.astype(u.dtype), u], axis=1)
    y = b + up[:, 0:t] * w[0]
    for j in range(1, CONV_W):
        y = y + up[:, j:j + t] * w[j]
    return jax.nn.silu(y), up[:, up.shape[1] - (CONV_W - 1):]


def _layer(x, c, lp, li, st):
    (g1, g2, w_mod, b_mod, w_in, b_in, gq_fox, gq_diff, gq_band, conv_w, conv_b,
     lam_p, g_subln, g_mh, rel_table, w_out, w_gate, w_up, w_down) = lp
    B, T, _ = x.shape
    sh1, sc1, gt1, sh2, sc2, gt2 = jnp.split((jax.nn.silu(c) @ w_mod + b_mod)[:, None, :], 6, axis=-1)
    h = _rms(x, g1) * (1.0 + sc1) + sh1
    (fq, fk, fv, ff, dq, dk, dv, mqk, mv, mi, mf, mo, bq, bk, bv) = _split_cols(h @ w_in + b_in)
    fq = _rms(_heads(fq), gq_fox[0])
    fk = _rms(_heads(fk), gq_fox[1])
    fv = _heads(fv)
    f_logf = jax.nn.log_sigmoid(ff.astype(jnp.float32))
    dq = _rms(dq.reshape(B, T, N_HEADS_G, 2, DIFF_HALF), gq_diff[0])
    dk = _rms(dk.reshape(B, T, N_HEADS_G, 2, DIFF_HALF), gq_diff[1])
    dv = _heads(dv)
    lam_init = 0.8 - 0.6 * math.exp(-0.3 * li)
    l32 = lam_p.astype(jnp.float32)
    lam = jnp.exp(jnp.sum(l32[0] * l32[1])) - jnp.exp(jnp.sum(l32[2] * l32[3])) + lam_init
    conv_buf = jnp.zeros((B, CONV_W - 1, 2 * GROUP_WIDTH), x.dtype) if st is None else st[10]
    mqk, new_conv = _causal_conv(mqk, conv_buf, conv_w, conv_b)
    mq, mk = jnp.split(mqk, 2, axis=-1)
    mq = _heads(mq)
    mk = _heads(mk) * HEAD_DIM ** -0.5
    mv = _heads(mv)
    m_ig = mi.astype(jnp.float32)
    m_lf = jax.nn.log_sigmoid(mf.astype(jnp.float32))
    bq = _rms(_heads(bq), gq_band[0])
    bk = _rms(_heads(bk), gq_band[1])
    bv = _heads(bv)

    if st is None:
        pos = jnp.arange(T)
        cum = jnp.cumsum(f_logf, axis=1)
        o_fox = _from_blocks(lax.map(
            lambda a: _fox_attend(a[0], fk, fv, a[1], cum, a[2], pos),
            (_to_blocks(fq, Q_BLOCK), _to_blocks(cum, Q_BLOCK), pos.reshape(-1, Q_BLOCK))))
        o_diff = _from_blocks(lax.map(
            lambda a: _diff_attend(a[0], dk, dv, lam, a[1], pos),
            (_to_blocks(dq, Q_BLOCK), pos.reshape(-1, Q_BLOCK))))
        n_c = T // CHUNK
        pad = ((0, 0), (BAND_ROWS, 0), (0, 0), (0, 0))
        idx = (jnp.arange(n_c) * CHUNK)[:, None] + jnp.arange(BAND_ROWS + CHUNK)[None, :]
        rel = jnp.arange(CHUNK)[:, None] + BAND_ROWS - jnp.arange(BAND_ROWS + CHUNK)[None, :]
        o_band = _band_attend(bq.reshape(B, n_c, CHUNK, N_HEADS_G, HEAD_DIM),
                              jnp.pad(bk, pad)[:, idx], jnp.pad(bv, pad)[:, idx],
                              rel, idx >= BAND_ROWS, rel_table).reshape(B, T, N_HEADS_G, HEAD_DIM)
        carry0 = (jnp.zeros((B, N_HEADS_G, HEAD_DIM, HEAD_DIM), jnp.float32),
                  jnp.zeros((B, N_HEADS_G, HEAD_DIM), jnp.float32),
                  jnp.zeros((B, N_HEADS_G), jnp.float32))
        (mC, mn, mm), hs = lax.scan(_mlstm_chunk, carry0,
                                    tuple(_to_blocks(a, CHUNK) for a in (mq, mk, mv, m_ig, m_lf)))
        h_ml = _from_blocks(hs)
        keep = min(BAND_ROWS, T)
        band_k_new, band_v_new = bk[:, T - keep:], bv[:, T - keep:]
    else:
        (c_fk, c_fv, c_flf, c_dk, c_dv, c_bk, c_bv, s_c, s_n, s_m, _) = st
        P = c_fk.shape[1]
        cum = jnp.cumsum(jnp.concatenate([c_flf.astype(jnp.float32), f_logf], axis=1), axis=1)
        o_fox = _fox_attend(fq, jnp.concatenate([c_fk, fk], axis=1), jnp.concatenate([c_fv, fv], axis=1),
                            cum[:, P:], cum, P + jnp.arange(T), jnp.arange(P + T))
        Pd = c_dk.shape[1]
        o_diff = _diff_attend(dq, jnp.concatenate([c_dk, dk], axis=1), jnp.concatenate([c_dv, dv], axis=1),
                              lam, Pd + jnp.arange(T), jnp.arange(Pd + T))
        Lb = c_bk.shape[1]
        rel = (Lb + jnp.arange(T))[:, None] - jnp.arange(Lb + T)[None, :]
        o_band = _band_attend(bq[:, None], jnp.concatenate([c_bk, bk], axis=1)[:, None],
                              jnp.concatenate([c_bv, bv], axis=1)[:, None],
                              rel, jnp.ones((1, Lb + T), bool), rel_table)[:, 0]
        (mC, mn, mm), h_ml = _mlstm_chunk(
            (s_c.astype(jnp.float32), s_n.astype(jnp.float32), s_m.astype(jnp.float32)),
            (mq, mk, mv, m_ig, m_lf))
        band_k_new, band_v_new = bk, bv

    o_diff = _rms(o_diff, g_subln) * (1.0 - lam_init)
    o_ml = _rms(jax.nn.sigmoid(mo).reshape(B, T, N_HEADS_G, HEAD_DIM) * h_ml.astype(x.dtype), g_mh)
    mix = jnp.concatenate([o_fox, o_diff, o_ml, o_band], axis=2).reshape(B, T, D_MODEL)
    x = x + gt1 * (mix @ w_out)
    h2 = _rms(x, g2) * (1.0 + sc2) + sh2
    x = x + gt2 * ((jax.nn.silu(h2 @ w_gate) * (h2 @ w_up)) @ w_down)
    return x, (fk, fv, f_logf, dk, dv, band_k_new, band_v_new, mC, mn, mm, new_conv)


def setup_inputs(seed: int = 0) -> dict:
    key = jax.random.key(seed)
    ks = iter(jax.random.split(key, 48))

    def nrm(shape, scale):
        return scale * jax.random.normal(next(ks), shape, jnp.float32)

    D, H, d = D_MODEL, N_HEADS_G, HEAD_DIM
    band_len = min(BAND_ROWS, PAST_LEN)
    starts = np.concatenate([[0], np.cumsum(IN_SPLIT_SIZES)]).tolist()
    b_in = nrm((DEPTH, N_IN), 0.02)
    b_in = b_in.at[:, int(starts[3]):int(starts[4])].add(FOX_FORGET_BIAS)
    b_in = b_in.at[:, int(starts[10]):int(starts[11])].add(jnp.linspace(3.0, 6.0, H))
    return {
        'x_prompt': nrm((BATCH, SEQ, D), 1.0),
        'x_sample': nrm((DEC_BATCH, DEC_SEQ, D), 1.0),
        'c_prompt': nrm((BATCH, D), 1.0),
        'c_sample': nrm((DEC_BATCH, D), 1.0),
        'cache_fox_k': nrm((DEPTH, DEC_BATCH, PAST_LEN, H, d), 1.0),
        'cache_fox_v': nrm((DEPTH, DEC_BATCH, PAST_LEN, H, d), 1.0),
        'cache_fox_logf': jax.nn.log_sigmoid(FOX_FORGET_BIAS + nrm((DEPTH, DEC_BATCH, PAST_LEN, H), 1.0)),
        'cache_diff_k': nrm((DEPTH, DEC_BATCH, PAST_LEN, H, 2, DIFF_HALF), 1.0),
        'cache_diff_v': nrm((DEPTH, DEC_BATCH, PAST_LEN, H, d), 1.0),
        'cache_band_k': nrm((DEPTH, DEC_BATCH, band_len, H, d), 1.0),
        'cache_band_v': nrm((DEPTH, DEC_BATCH, band_len, H, d), 1.0),
        'state_mlstm_c': nrm((DEPTH, DEC_BATCH, H, d, d), 0.1),
        'state_mlstm_n': nrm((DEPTH, DEC_BATCH, H, d), 0.1),
        'state_mlstm_m': nrm((DEPTH, DEC_BATCH, H), 1.0),
        'state_conv': nrm((DEPTH, DEC_BATCH, CONV_W - 1, 2 * GROUP_WIDTH), 1.0),
        'norm1_g': 1.0 + nrm((DEPTH, D), 0.02),
        'norm2_g': 1.0 + nrm((DEPTH, D), 0.02),
        'w_mod': nrm((DEPTH, D, 6 * D), 0.5 * D ** -0.5),
        'b_mod': nrm((DEPTH, 6 * D), 0.02),
        'w_in': nrm((DEPTH, D, N_IN), D ** -0.5),
        'b_in': b_in,
        'qk_g_fox': 1.0 + nrm((DEPTH, 2, d), 0.02),
        'qk_g_diff': 1.0 + nrm((DEPTH, 2, DIFF_HALF), 0.02),
        'qk_g_band': 1.0 + nrm((DEPTH, 2, d), 0.02),
        'conv_w': nrm((DEPTH, CONV_W, 2 * GROUP_WIDTH), CONV_W ** -0.5),
        'conv_b': nrm((DEPTH, 2 * GROUP_WIDTH), 0.02),
        'diff_lambda': nrm((DEPTH, 4, DIFF_HALF), 0.1),
        'diff_subln_g': 1.0 + nrm((DEPTH, d), 0.02),
        'mlstm_norm_g': 1.0 + nrm((DEPTH, d), 0.02),
        'band_rel_bias': nrm((DEPTH, H, 2 * REL_CLIP + 1), 0.1),
        'w_out': nrm((DEPTH, D, D), D ** -0.5),
        'w_ffn_gate': nrm((DEPTH, D, D_FF), D ** -0.5),
        'w_ffn_up': nrm((DEPTH, D, D_FF), D ** -0.5),
        'w_ffn_down': nrm((DEPTH, D_FF, D), D_FF ** -0.5),
    }


def reference(x_prompt, x_sample, c_prompt, c_sample, cache_fox_k, cache_fox_v, cache_fox_logf,
              cache_diff_k, cache_diff_v, cache_band_k, cache_band_v, state_mlstm_c, state_mlstm_n,
              state_mlstm_m, state_conv, norm1_g, norm2_g, w_mod, b_mod, w_in, b_in, qk_g_fox,
              qk_g_diff, qk_g_band, conv_w, conv_b, diff_lambda, diff_subln_g, mlstm_norm_g,
              band_rel_bias, w_out, w_ffn_gate, w_ffn_up, w_ffn_down):
    weights = (norm1_g, norm2_g, w_mod, b_mod, w_in, b_in, qk_g_fox, qk_g_diff, qk_g_band,
               conv_w, conv_b, diff_lambda, diff_subln_g, mlstm_norm_g, band_rel_bias,
               w_out, w_ffn_gate, w_ffn_up, w_ffn_down)
    caches = (cache_fox_k, cache_fox_v, cache_fox_logf, cache_diff_k, cache_diff_v,
              cache_band_k, cache_band_v, state_mlstm_c, state_mlstm_n, state_mlstm_m, state_conv)
    x = x_prompt
    p_out = []
    for l in range(DEPTH):
        x, s = _layer(x, c_prompt, tuple(w[l] for w in weights), l, None)
        p_out.append(s)
    y_prompt = x
    x = x_sample
    s_out = []
    for l in range(DEPTH):
        x, s = _layer(x, c_sample, tuple(w[l] for w in weights), l, tuple(cc[l] for cc in caches))
        s_out.append(s)
    y_sample = x
    (p_fox_k, p_fox_v, p_fox_logf, p_diff_k, p_diff_v, p_band_k, p_band_v,
     p_mlstm_c, p_mlstm_n, p_mlstm_m, p_conv) = [jnp.stack(z) for z in zip(*p_out)]
    (s_fox_k, s_fox_v, s_fox_logf, s_diff_k, s_diff_v, s_band_k, s_band_v,
     s_mlstm_c, s_mlstm_n, s_mlstm_m, s_conv) = [jnp.stack(z) for z in zip(*s_out)]
    return (y_prompt, y_sample,
            p_fox_k, p_fox_v, p_fox_logf, p_diff_k, p_diff_v, p_band_k, p_band_v,
            p_mlstm_c, p_mlstm_n, p_mlstm_m, p_conv,
            s_fox_k, s_fox_v, s_fox_logf, s_diff_k, s_diff_v, s_band_k, s_band_v,
            s_mlstm_c, s_mlstm_n, s_mlstm_m, s_conv)
```

```python
import functools
import math

import numpy as np
import jax
import jax.numpy as jnp
from jax import lax
from jax.experimental import pallas as pl
from jax.experimental.pallas import tpu as pltpu

F32 = jnp.float32
BF16 = jnp.bfloat16

D_MODEL = 1024
HEAD_DIM = 64
N_HEADS = 4
GROUP = N_HEADS * HEAD_DIM
DIFF_HALF = HEAD_DIM // 2
CHUNK = 64
BAND_CHUNKS = 8
BAND_ROWS = BAND_CHUNKS * CHUNK
REL_CLIP = 128
CONV_W = 4
D_FF = 2816
EPS = 1e-6
NEG = -1e30

LANES = 128
PAIR = 2 * HEAD_DIM
N_GATES = 3 * N_HEADS
BAND_WIN = (BAND_CHUNKS + 2) * CHUNK
ML_CHUNK = 256
ML_PAD = 128
VMEM_LIMIT = 56 * 1024 * 1024

IN_SPLIT_SIZES = (GROUP, GROUP, GROUP, N_HEADS, GROUP, GROUP, GROUP, 2 * GROUP, GROUP,
                  N_HEADS, N_HEADS, GROUP, GROUP, GROUP, GROUP)
_FULL_GROUPS = (0, 1, 2, 4, 5, 6, 7, 8, 11, 12, 13, 14)
_GATE_GROUPS = (3, 9, 10)
N_FULL = 13 * GROUP
N_IN_PAD = N_FULL + LANES


def _cparams(sem):
    return pltpu.CompilerParams(dimension_semantics=sem, vmem_limit_bytes=VMEM_LIMIT)


def _nt_dot(a, b):
    return lax.dot_general(a, b, (((1,), (1,)), ((), ())), preferred_element_type=F32)


def _dot(a, b):
    return jnp.dot(a, b, preferred_element_type=F32)


def _split3(x):
    hi = x.astype(BF16)
    r = x - hi.astype(F32)
    mid = r.astype(BF16)
    lo = (r - mid.astype(F32)).astype(BF16)
    return hi, mid, lo


def _log_sigmoid(x):
    return jnp.minimum(x, 0.0) - jnp.log1p(jnp.exp(-jnp.abs(x)))


def _mod_kernel(c_ref, w_ref, b_ref, o_ref):
    c = c_ref[...]
    a = (c * jax.nn.sigmoid(c)).astype(BF16)
    o_ref[0] = _dot(a, w_ref[0].astype(BF16)) + b_ref[0]


def _modulation(c_all, w_mod, b_mod):
    depth, d, n = w_mod.shape
    rows = c_all.shape[0]
    tn = 768
    return pl.pallas_call(
        _mod_kernel,
        grid=(depth, n // tn),
        in_specs=[pl.BlockSpec((rows, d), lambda l, j: (0, 0)),
                  pl.BlockSpec((1, d, tn), lambda l, j: (l, 0, j)),
                  pl.BlockSpec((1, 1, tn), lambda l, j: (l, 0, j))],
        out_specs=pl.BlockSpec((1, rows, tn), lambda l, j: (l, 0, j)),
        out_shape=jax.ShapeDtypeStruct((depth, rows, n), F32),
        compiler_params=_cparams(("parallel", "parallel")),
        name="modulation",
    )(c_all, w_mod, b_mod.reshape(depth, 1, n))


_IN_OUTS = (
    ("fqa", 4 * LANES, BF16),
    ("fka", 4 * LANES, BF16),
    ("fk", GROUP, F32), ("fv", GROUP, F32), ("fvb", GROUP, BF16),
    ("dqv", 8 * LANES, BF16),
    ("dk", GROUP, F32), ("dkb", GROUP, BF16), ("dv", GROUP, F32), ("dvb", GROUP, BF16),
    ("mqk", 2 * GROUP, F32), ("mvb", GROUP, BF16), ("mos", GROUP, F32),
    ("bqm", 4 * LANES, BF16),
    ("bk", GROUP, F32), ("bkb", GROUP, BF16), ("bv", GROUP, F32), ("bvb", GROUP, BF16),
    ("gates", LANES, F32),
    ("cum", LANES, F32),
)


def _in_proj_kernel(x_ref, sh_ref, sc_ref, g1_ref, w_ref, b_ref, gains_ref, s64_ref, s32_ref, tril_ref,
                    fqa_ref, fka_ref, fk_ref, fv_ref, fvb_ref, dqv_ref, dk_ref, dkb_ref, dv_ref, dvb_ref,
                    mqk_ref, mvb_ref, mos_ref, bqm_ref, bk_ref, bkb_ref, bv_ref, bvb_ref, gates_ref,
                    cum_ref, carry_ref, *, bb, tt, running, ml_blocks):
    tm = bb * tt
    x = x_ref[...]
    ms = jnp.mean(x * x, axis=-1, keepdims=True)
    h = x * lax.rsqrt(ms + EPS) * g1_ref[...]
    h = h * (1.0 + sc_ref[...]) + sh_ref[...]
    hb = h.reshape(tm, D_MODEL).astype(BF16)

    def proj(g, width=GROUP):
        c0 = GROUP * g
        return _dot(hb, w_ref[:, c0:c0 + width]) + b_ref[:, c0:c0 + width]

    def rms_seg(z, s_ref, row, n):
        ss = _dot((z * z).astype(BF16), s_ref[...])
        return z * lax.rsqrt(ss * (1.0 / n) + EPS) * gains_ref[row:row + 1, :]

    lane = lax.broadcasted_iota(jnp.int32, (tm, LANES), 1)

    zg = proj(13, LANES)
    is_ls = (lane < N_HEADS) | ((lane >= 2 * N_HEADS) & (lane < N_GATES))
    gates = jnp.where(is_ls, _log_sigmoid(zg), zg)
    gates_ref[...] = gates
    ghi, gmid, glo = _split3(gates)
    tril = tril_ref[...]
    lane_b = lax.broadcasted_iota(jnp.int32, (LANES, LANES), 1)
    if running:
        @pl.when(pl.program_id(1) == 0)
        def _():
            carry_ref[...] = jnp.zeros_like(carry_ref)
        carry_a = carry_ref[0:1, :]
    carry_b = None
    cums = []
    for blk in range(tm // LANES):
        sl = slice(LANES * blk, LANES * blk + LANES)
        p = _dot(tril, ghi[sl]) + _dot(tril, gmid[sl]) + _dot(tril, glo[sl])
        if running:
            ca = p + carry_a
            carry_a = ca[LANES - 1:LANES, :]
            cb = p if blk % ml_blocks == 0 else p + carry_b
            carry_b = cb[LANES - 1:LANES, :]
            p = jnp.where(lane_b < N_HEADS, ca, cb)
        cums.append(p)
    if running:
        carry_ref[0:1, :] = carry_a
    cum = jnp.concatenate(cums, axis=0)
    cum_ref[...] = cum

    fqn = rms_seg(proj(0), s64_ref, 0, HEAD_DIM) * (HEAD_DIM ** -0.5)
    fkn = rms_seg(proj(1), s64_ref, 1, HEAD_DIM)
    fk_ref[...] = fkn
    fv = proj(2)
    fv_ref[...] = fv
    fvb_ref[...] = fv.astype(BF16)
    ones_q = jnp.where((lane >= HEAD_DIM + 3) & (lane < HEAD_DIM + 6), 1.0, 0.0)
    ones_k = jnp.where((lane >= HEAD_DIM) & (lane < HEAD_DIM + 3), 1.0, 0.0)
    for hd in range(N_HEADS):
        pr, e = divmod(hd, 2)
        bq = fqn[:, LANES * pr:LANES * pr + LANES]
        bk = fkn[:, LANES * pr:LANES * pr + LANES]
        if e:
            bq = pltpu.roll(bq, HEAD_DIM, axis=1)
            bk = pltpu.roll(bk, HEAD_DIM, axis=1)
        cbc = jnp.broadcast_to(cum[:, hd:hd + 1], (tm, LANES))
        chi = cbc.astype(BF16).astype(F32)
        r = cbc - chi
        cmid = r.astype(BF16).astype(F32)
        clo = (r - cmid).astype(BF16).astype(F32)
        aq = jnp.where(lane == HEAD_DIM, chi,
                       jnp.where(lane == HEAD_DIM + 1, cmid, jnp.where(lane == HEAD_DIM + 2, clo, ones_q)))
        ak = jnp.where(lane == HEAD_DIM + 3, -chi,
                       jnp.where(lane == HEAD_DIM + 4, -cmid, jnp.where(lane == HEAD_DIM + 5, -clo, ones_k)))
        fqa_ref[:, LANES * hd:LANES * hd + LANES] = jnp.where(lane < HEAD_DIM, bq, aq).astype(BF16)
        fka_ref[:, LANES * hd:LANES * hd + LANES] = jnp.where(lane < HEAD_DIM, bk, ak).astype(BF16)

    dqn = rms_seg(proj(3), s32_ref, 2, DIFF_HALF) * (DIFF_HALF ** -0.5)
    dkn = rms_seg(proj(4), s32_ref, 3, DIFF_HALF)
    dk_ref[...] = dkn
    dkb_ref[...] = dkn.astype(BF16)
    dv = proj(5)
    dv_ref[...] = dv
    dvb_ref[...] = dv.astype(BF16)
    for pr in range(2):
        blk = dqn[:, LANES * pr:LANES * pr + LANES]
        for e in range(2):
            for jm in range(2):
                l0 = HEAD_DIM * e + DIFF_HALF * jm
                idx = 4 * pr + 2 * e + jm
                dqv_ref[:, LANES * idx:LANES * idx + LANES] = jnp.where(
                    (lane >= l0) & (lane < l0 + DIFF_HALF), blk, 0.0).astype(BF16)

    mqk_ref[:, 0:GROUP] = proj(6)
    mqk_ref[:, GROUP:2 * GROUP] = proj(7)
    mvb_ref[...] = proj(8).astype(BF16)
    mos_ref[...] = jax.nn.sigmoid(proj(9))

    bqn = rms_seg(proj(10), s64_ref, 4, HEAD_DIM) * (HEAD_DIM ** -0.5)
    bkn = rms_seg(proj(11), s64_ref, 5, HEAD_DIM)
    bk_ref[...] = bkn
    bkb_ref[...] = bkn.astype(BF16)
    bv = proj(12)
    bv_ref[...] = bv
    bvb_ref[...] = bv.astype(BF16)
    for hd in range(N_HEADS):
        pr, e = divmod(hd, 2)
        blk = bqn[:, LANES * pr:LANES * pr + LANES]
        bqm_ref[:, LANES * hd:LANES * hd + LANES] = jnp.where(
            (lane >= HEAD_DIM * e) & (lane < HEAD_DIM * e + HEAD_DIM), blk, 0.0).astype(BF16)


def _in_proj(x, sh, sc, g1, w, b, gains, s64, s32, tril, *, bb, tt, running, ml_blocks):
    bx, tx, d = x.shape
    n = bx * tx
    nb, nt = bx // bb, tx // tt
    tm = bb * tt
    const = lambda i, t: (0, 0)
    tok = lambda i, t: (i * nt + t, 0)
    in_specs = [
        pl.BlockSpec((bb, tt, d), lambda i, t: (i, t, 0)),
        pl.BlockSpec((bb, 1, d), lambda i, t: (i, 0, 0)),
        pl.BlockSpec((bb, 1, d), lambda i, t: (i, 0, 0)),
        pl.BlockSpec((1, d), const),
        pl.BlockSpec((d, N_IN_PAD), const),
        pl.BlockSpec((1, N_IN_PAD), const),
        pl.BlockSpec((8, GROUP), const),
        pl.BlockSpec((GROUP, GROUP), const),
        pl.BlockSpec((GROUP, GROUP), const),
        pl.BlockSpec((LANES, LANES), const),
    ]
    out_specs = [pl.BlockSpec((tm, wd), tok) for _, wd, _ in _IN_OUTS]
    out_shape = [jax.ShapeDtypeStruct((n, wd), dt) for _, wd, dt in _IN_OUTS]
    outs = pl.pallas_call(
        functools.partial(_in_proj_kernel, bb=bb, tt=tt, running=running, ml_blocks=ml_blocks),
        grid=(nb, nt),
        in_specs=in_specs,
        out_specs=out_specs,
        out_shape=out_shape,
        scratch_shapes=[pltpu.VMEM((8, LANES), F32)],
        compiler_params=_cparams(("arbitrary", "arbitrary")),
        name="in_proj",
    )(x, sh, sc, g1, w, b, gains, s64, s32, tril)
    return {name: o for (name, _, _), o in zip(_IN_OUTS, outs)}


def _softmax_step(s, v, m_ref, l_ref, acc_ref, c):
    m_prev = m_ref[c]
    m_new = jnp.maximum(m_prev, jnp.max(s, axis=-1, keepdims=True))
    alpha = jnp.exp(m_prev - m_new)
    p = jnp.exp(s - m_new)
    l_ref[c] = alpha * l_ref[c] + jnp.sum(p, axis=-1, keepdims=True)
    acc_ref[c] = alpha * acc_ref[c] + _dot(p.astype(BF16), v)
    m_ref[c] = m_new


def _fox_kernel(q_ref, k_ref, v_ref, o_ref, m_ref, l_ref, acc_ref, *, tq):
    i = pl.program_id(1)
    j = pl.program_id(2)

    @pl.when(j == 0)
    def _():
        m_ref[...] = jnp.full_like(m_ref, NEG)
        l_ref[...] = jnp.zeros_like(l_ref)
        acc_ref[...] = jnp.zeros_like(acc_ref)

    def step(diag):
        q = q_ref[0]
        k = k_ref[0]
        v = v_ref[0]
        if diag:
            row = lax.broadcasted_iota(jnp.int32, (tq, tq), 0)
            col = lax.broadcasted_iota(jnp.int32, (tq, tq), 1)
            keep = col <= row
        for hd in range(N_HEADS):
            s = _nt_dot(q[:, LANES * hd:LANES * hd + LANES], k[:, LANES * hd:LANES * hd + LANES])
            if diag:
                s = jnp.where(keep, s, NEG)
            pr = hd // 2
            _softmax_step(s, v[:, LANES * pr:LANES * pr + LANES], m_ref, l_ref, acc_ref, hd)

    @pl.when(j < i)
    def _():
        step(False)

    @pl.when(j == i)
    def _():
        step(True)
        lane = lax.broadcasted_iota(jnp.int32, (tq, LANES), 1)
        for pr in range(2):
            oe = acc_ref[2 * pr] / l_ref[2 * pr]
            oo = acc_ref[2 * pr + 1] / l_ref[2 * pr + 1]
            o_ref[0, :, LANES * pr:LANES * pr + LANES] = jnp.where(lane < HEAD_DIM, oe, oo).astype(BF16)


def _fox_prompt(qa, ka, vb, *, tq):
    b, t, _ = qa.shape
    nq = t // tq
    return pl.pallas_call(
        functools.partial(_fox_kernel, tq=tq),
        grid=(b, nq, nq),
        in_specs=[pl.BlockSpec((1, tq, 4 * LANES), lambda bi, i, j: (bi, i, 0)),
                  pl.BlockSpec((1, tq, 4 * LANES), lambda bi, i, j: (bi, jnp.minimum(i, j), 0)),
                  pl.BlockSpec((1, tq, GROUP), lambda bi, i, j: (bi, jnp.minimum(i, j), 0))],
        out_specs=pl.BlockSpec((1, tq, GROUP), lambda bi, i, j: (bi, i, 0)),
        out_shape=jax.ShapeDtypeStruct((b, t, GROUP), BF16),
        scratch_shapes=[pltpu.VMEM((N_HEADS, tq, 1), F32), pltpu.VMEM((N_HEADS, tq, 1), F32),
                        pltpu.VMEM((N_HEADS, tq, LANES), F32)],
        compiler_params=_cparams(("parallel", "parallel", "arbitrary")),
        name="fox_prompt",
    )(qa, ka, vb)


def _diff_lambda(lp, lam_init):
    a = jnp.sum(lp[0:1, :] * lp[1:2, :], axis=-1, keepdims=True)
    b = jnp.sum(lp[2:3, :] * lp[3:4, :], axis=-1, keepdims=True)
    return jnp.exp(a) - jnp.exp(b) + lam_init


def _diff_finish(acc_ref, l_ref, lam, gsub, lam_init, rows, o_ref):
    lane = lax.broadcasted_iota(jnp.int32, (rows, LANES), 1)
    for pr in range(2):
        outs = []
        for e in range(2):
            hd = 2 * pr + e
            o = acc_ref[2 * hd] / l_ref[2 * hd] - lam * (acc_ref[2 * hd + 1] / l_ref[2 * hd + 1])
            valid = (lane >= HEAD_DIM * e) & (lane < HEAD_DIM * e + HEAD_DIM)
            ms = jnp.sum(jnp.where(valid, o * o, 0.0), axis=-1, keepdims=True) * (1.0 / HEAD_DIM)
            outs.append(o * lax.rsqrt(ms + EPS) * gsub * (1.0 - lam_init))
        o_ref[0, :, LANES * pr:LANES * pr + LANES] = jnp.where(lane < HEAD_DIM, outs[0], outs[1]).astype(BF16)


def _alibi_slope(hd):
    return 2.0 ** (-8.0 * (hd + 1) / N_HEADS)


def _diff_kernel(q_ref, k_ref, v_ref, lam_ref, gsub_ref, o_ref, m_ref, l_ref, acc_ref, *, tq, lam_init):
    i = pl.program_id(1)
    j = pl.program_id(2)

    @pl.when(j == 0)
    def _():
        m_ref[...] = jnp.full_like(m_ref, NEG)
        l_ref[...] = jnp.zeros_like(l_ref)
        acc_ref[...] = jnp.zeros_like(acc_ref)

    def step(diag):
        q = q_ref[0]
        k = k_ref[0]
        v = v_ref[0]
        row = lax.broadcasted_iota(jnp.int32, (tq, tq), 0)
        col = lax.broadcasted_iota(jnp.int32, (tq, tq), 1)
        dist = jnp.abs((i - j) * tq + row - col).astype(F32)
        if diag:
            seen = (col // CHUNK) <= (row // CHUNK)
        for hd in range(N_HEADS):
            pr, e = divmod(hd, 2)
            bias = _alibi_slope(hd) * dist
            kp = k[:, LANES * pr:LANES * pr + LANES]
            vp = v[:, LANES * pr:LANES * pr + LANES]
            for jm in range(2):
                idx = 4 * pr + 2 * e + jm
                s = _nt_dot(q[:, LANES * idx:LANES * idx + LANES], kp) - bias
                if diag:
                    s = jnp.where(seen, s, NEG)
                _softmax_step(s, vp, m_ref, l_ref, acc_ref, 2 * hd + jm)

    @pl.when(j < i)
    def _():
        step(False)

    @pl.when(j == i)
    def _():
        step(True)
        lam = _diff_lambda(lam_ref[...], lam_init)
        _diff_finish(acc_ref, l_ref, lam, gsub_ref[...], lam_init, tq, o_ref)


def _diff_prompt(qv, kb, vb, lam_p, gsub, *, tq, lam_init):
    b, t, _ = qv.shape
    nq = t // tq
    return pl.pallas_call(
        functools.partial(_diff_kernel, tq=tq, lam_init=lam_init),
        grid=(b, nq, nq),
        in_specs=[pl.BlockSpec((1, tq, 8 * LANES), lambda bi, i, j: (bi, i, 0)),
                  pl.BlockSpec((1, tq, GROUP), lambda bi, i, j: (bi, jnp.minimum(i, j), 0)),
                  pl.BlockSpec((1, tq, GROUP), lambda bi, i, j: (bi, jnp.minimum(i, j), 0)),
                  pl.BlockSpec((4, DIFF_HALF), lambda bi, i, j: (0, 0)),
                  pl.BlockSpec((1, LANES), lambda bi, i, j: (0, 0))],
        out_specs=pl.BlockSpec((1, tq, GROUP), lambda bi, i, j: (bi, i, 0)),
        out_shape=jax.ShapeDtypeStruct((b, t, GROUP), BF16),
        scratch_shapes=[pltpu.VMEM((2 * N_HEADS, tq, 1), F32), pltpu.VMEM((2 * N_HEADS, tq, 1), F32),
                        pltpu.VMEM((2 * N_HEADS, tq, LANES), F32)],
        compiler_params=_cparams(("parallel", "parallel", "arbitrary")),
        name="diff_prompt",
    )(qv, kb, vb, lam_p, gsub)


def _band_bias_kernel(g_ref, bp_ref, bs_ref):
    lane = lax.broadcasted_iota(jnp.int32, (CHUNK, BAND_WIN), 1)
    width = g_ref.shape[1]
    for hd in range(N_HEADS):
        x = jnp.broadcast_to(g_ref[hd:hd + 1, :], (CHUNK, width))
        rp = pltpu.roll(x, width - CHUNK + 1, axis=1, stride=1, stride_axis=0)
        bp_ref[hd] = jnp.where(lane < CHUNK, NEG, rp[:, :BAND_WIN])
        rs = pltpu.roll(x, width - 2 * CHUNK + 1, axis=1, stride=1, stride_axis=0)
        bs_ref[hd] = rs[:, :BAND_WIN]


def _band_bias(table):
    gv = jnp.concatenate([jnp.broadcast_to(table[:, 2 * REL_CLIP:], (N_HEADS, 4 * LANES)),
                          table[:, 2 * REL_CLIP - 1::-1]], axis=1)
    shp = jax.ShapeDtypeStruct((N_HEADS, CHUNK, BAND_WIN), F32)
    return pl.pallas_call(_band_bias_kernel, out_shape=(shp, shp), name="band_bias")(gv)


def _band_kernel(q_ref, k_ref, v_ref, bias_ref, o_ref, *, tq):
    t = pl.program_id(1)
    n_sub = tq // CHUNK
    lane = lax.broadcasted_iota(jnp.int32, (CHUNK, LANES), 1)
    col = lax.broadcasted_iota(jnp.int32, (CHUNK, BAND_WIN), 1)
    for c in range(n_sub):
        chunk = t * n_sub + c
        start = pl.multiple_of((chunk + 1) * CHUNK, CHUNK)
        kw = k_ref[0, pl.ds(start, BAND_WIN), :]
        vw = v_ref[0, pl.ds(start, BAND_WIN), :]
        q = q_ref[0, CHUNK * c:CHUNK * c + CHUNK, :]
        valid = col + (chunk - BAND_CHUNKS - 1) * CHUNK >= 0
        for pr in range(2):
            outs = []
            for e in range(2):
                hd = 2 * pr + e
                s = _nt_dot(q[:, LANES * hd:LANES * hd + LANES], kw[:, LANES * pr:LANES * pr + LANES])
                s = jnp.where(valid, s + bias_ref[hd], NEG)
                m = jnp.max(s, axis=-1, keepdims=True)
                p = jnp.exp(s - m)
                l = jnp.sum(p, axis=-1, keepdims=True)
                outs.append(_dot(p.astype(BF16), vw[:, LANES * pr:LANES * pr + LANES]) / l)
            o_ref[0, CHUNK * c:CHUNK * c + CHUNK, LANES * pr:LANES * pr + LANES] = jnp.where(
                lane < HEAD_DIM, outs[0], outs[1]).astype(BF16)


def _band_prompt(qm, kpad, vpad, bias, *, tq):
    b, t, _ = qm.shape
    tp = kpad.shape[1]
    return pl.pallas_call(
        functools.partial(_band_kernel, tq=tq),
        grid=(b, t // tq),
        in_specs=[pl.BlockSpec((1, tq, 4 * LANES), lambda bi, i: (bi, i, 0)),
                  pl.BlockSpec((1, tp, GROUP), lambda bi, i: (bi, 0, 0)),
                  pl.BlockSpec((1, tp, GROUP), lambda bi, i: (bi, 0, 0)),
                  pl.BlockSpec((N_HEADS, CHUNK, BAND_WIN), lambda bi, i: (0, 0, 0))],
        out_specs=pl.BlockSpec((1, tq, GROUP), lambda bi, i: (bi, i, 0)),
        out_shape=jax.ShapeDtypeStruct((b, t, GROUP), BF16),
        compiler_params=_cparams(("parallel", "arbitrary")),
        name="band_prompt",
    )(qm, kpad, vpad, bias)


def _mlstm_kernel(mqk_ref, mv_ref, gates_ref, cum_ref, mos_ref, cw_ref, cb_ref, gmh_ref, shead_ref, bd_ref,
                  c0_ref, n0_ref, m0_ref, conv0_ref,
                  o_ref, c_out_ref, n_out_ref, m_out_ref,
                  cbuf_ref, c_ref, n_ref, m_ref, *, rows, valid):
    t = pl.program_id(1)

    @pl.when(t == 0)
    def _():
        c_ref[...] = c0_ref[0]
        n_ref[...] = n0_ref[0]
        m_ref[...] = m0_ref[0]
        cbuf_ref[0:8, :] = conv0_ref[0]

    def padded(a, fill=0.0):
        if valid == rows:
            return a
        return jnp.concatenate([a, jnp.full((rows - valid, a.shape[1]), fill, a.dtype)], axis=0)

    u = padded(mqk_ref[...])
    cbuf_ref[8:8 + rows, :] = u
    y = cb_ref[...] + cw_ref[3:4, :] * u
    for jw in range(CONV_W - 1):
        y = y + cw_ref[jw:jw + 1, :] * cbuf_ref[5 + jw:5 + jw + rows, :]
    cbuf_ref[0:8, :] = cbuf_ref[valid:valid + 8, :]
    qk = y * jax.nn.sigmoid(y)
    q = qk[:, 0:GROUP]
    k = qk[:, GROUP:2 * GROUP] * (HEAD_DIM ** -0.5)
    qb = q.astype(BF16)
    kb = k.astype(BF16)
    v = padded(mv_ref[...])
    mos = padded(mos_ref[...])

    g = gates_ref[...]
    cm = cum_ref[...]
    if valid != rows:
        g = padded(g, NEG)
        cm = jnp.concatenate([cm, jnp.broadcast_to(cm[valid - 1:valid, :], (rows - valid, LANES))], axis=0)
    g_t = g.T
    cm_t = cm.T

    row = lax.broadcasted_iota(jnp.int32, (rows, rows), 0)
    col = lax.broadcasted_iota(jnp.int32, (rows, rows), 1)
    causal = col <= row
    lane_g = lax.broadcasted_iota(jnp.int32, (rows, GROUP), 1)
    lane_r = lax.broadcasted_iota(jnp.int32, (1, GROUP), 1)
    m_prev_all = m_ref[...]

    zeros = jnp.zeros((rows, GROUP), F32)
    num, a_full, ws_full, mt_full, wend_full = zeros, zeros, zeros, zeros, zeros
    decay_lane = jnp.zeros((1, GROUP), F32)
    mnew_lane = jnp.zeros((1, GROUP), F32)
    for hd in range(N_HEADS):
        hm = (lane_g >= HEAD_DIM * hd) & (lane_g < HEAD_DIM * hd + HEAD_DIM)
        hm_r = (lane_r >= HEAD_DIM * hd) & (lane_r < HEAD_DIM * hd + HEAD_DIM)
        f_c = cm[:, 2 * N_HEADS + hd:2 * N_HEADS + hd + 1]
        ig_c = g[:, N_HEADS + hd:N_HEADS + hd + 1]
        f_r = cm_t[2 * N_HEADS + hd:2 * N_HEADS + hd + 1, :]
        ig_r = g_t[N_HEADS + hd:N_HEADS + hd + 1, :]
        m_prev = m_prev_all[:, HEAD_DIM * hd:HEAD_DIM * hd + 1]
        logw = jnp.where(causal, f_c - (f_r - ig_r), NEG)
        logb = f_c + m_prev
        m_t = jnp.maximum(logb, jnp.max(logw, axis=-1, keepdims=True))
        qm = jnp.where(hm, q, 0.0).astype(BF16)
        w = jnp.exp(logw - m_t) * _nt_dot(qm, kb)
        a = jnp.exp(logb - m_t)
        num = jnp.where(hm, _dot(w.astype(BF16), v), num)
        a_full = jnp.where(hm, a, a_full)
        ws_full = jnp.where(hm, jnp.sum(w, axis=-1, keepdims=True), ws_full)
        mt_full = jnp.where(hm, m_t, mt_full)
        f_end = f_c[rows - 1:rows, :]
        log_end = f_end - f_c + ig_c
        m_new = jnp.maximum(f_end + m_prev, jnp.max(log_end, axis=0, keepdims=True))
        wend_full = jnp.where(hm, jnp.exp(log_end - m_new), wend_full)
        decay_lane = jnp.where(hm_r, jnp.exp(f_end + m_prev - m_new), decay_lane)
        mnew_lane = jnp.where(hm_r, m_new, mnew_lane)

    c_old = c_ref[...]
    n_old = n_ref[...]
    shead = shead_ref[...]
    q_c = _dot(qb, c_old.astype(BF16))
    q_n = _dot((q * n_old).astype(BF16), shead)
    num = num + a_full * q_c
    den = ws_full + a_full * q_n
    h = num / jnp.maximum(jnp.abs(den), jnp.exp(-mt_full))
    o = mos * h
    ss = _dot((o * o).astype(BF16), shead)
    on = o * lax.rsqrt(ss * (1.0 / HEAD_DIM) + EPS) * gmh_ref[...]
    o_ref[...] = on[0:valid, :].astype(BF16)

    wk = k * wend_full
    kv = lax.dot_general(wk.astype(BF16), v, (((0,), (0,)), ((), ())), preferred_element_type=F32)
    c_ref[...] = c_old * decay_lane + jnp.where(bd_ref[...] > 0.5, kv, 0.0)
    n_ref[...] = n_old * decay_lane + jnp.sum(wk, axis=0, keepdims=True)
    m_ref[...] = mnew_lane

    @pl.when(t == pl.num_programs(1) - 1)
    def _():
        c_out_ref[0] = c_ref[...]
        n_out_ref[0] = n_ref[...]
        m_out_ref[0] = m_ref[...]


def _mlstm(mqk, mvb, gates, cum, mos, cw, cb, gmh, shead, bd, c0, n0, m0, conv0, *, nb, t, valid, rows):
    nc = t // valid
    tok = lambda bi, i: (bi * nc + i, 0)
    const = lambda bi, i: (0, 0)
    per_b = lambda bi, i: (bi, 0, 0)
    return pl.pallas_call(
        functools.partial(_mlstm_kernel, rows=rows, valid=valid),
        grid=(nb, nc),
        in_specs=[pl.BlockSpec((valid, 2 * GROUP), tok), pl.BlockSpec((valid, GROUP), tok),
                  pl.BlockSpec((valid, LANES), tok), pl.BlockSpec((valid, LANES), tok),
                  pl.BlockSpec((valid, GROUP), tok),
                  pl.BlockSpec((CONV_W, 2 * GROUP), const), pl.BlockSpec((1, 2 * GROUP), const),
                  pl.BlockSpec((1, GROUP), const), pl.BlockSpec((GROUP, GROUP), const),
                  pl.BlockSpec((GROUP, GROUP), const),
                  pl.BlockSpec((1, GROUP, GROUP), per_b), pl.BlockSpec((1, 1, GROUP), per_b),
                  pl.BlockSpec((1, 1, GROUP), per_b), pl.BlockSpec((1, 8, 2 * GROUP), per_b)],
        out_specs=[pl.BlockSpec((valid, GROUP), tok), pl.BlockSpec((1, GROUP, GROUP), per_b),
                   pl.BlockSpec((1, 1, GROUP), per_b), pl.BlockSpec((1, 1, GROUP), per_b)],
        out_shape=[jax.ShapeDtypeStruct((nb * t, GROUP), BF16), jax.ShapeDtypeStruct((nb, GROUP, GROUP), F32),
                   jax.ShapeDtypeStruct((nb, 1, GROUP), F32), jax.ShapeDtypeStruct((nb, 1, GROUP), F32)],
        scratch_shapes=[pltpu.VMEM((8 + rows, 2 * GROUP), F32), pltpu.VMEM((GROUP, GROUP), F32),
                        pltpu.VMEM((1, GROUP), F32), pltpu.VMEM((1, GROUP), F32)],
        compiler_params=_cparams(("parallel", "arbitrary")),
        name="mlstm",
    )(mqk, mvb, gates, cum, mos, cw, cb, gmh, shead, bd, c0, n0, m0, conv0)


def _pad_rows(a, rows):
    return jnp.concatenate([a, jnp.zeros((rows - a.shape[0], a.shape[1]), a.dtype)], axis=0)


def _two_part_softmax(s_c, s_n, v_c, v_n):
    m = jnp.maximum(jnp.max(s_c, axis=-1, keepdims=True), jnp.max(s_n, axis=-1, keepdims=True))
    p_c = jnp.exp(s_c - m)
    p_n = jnp.exp(s_n - m)
    l = jnp.sum(p_c, axis=-1, keepdims=True) + jnp.sum(p_n, axis=-1, keepdims=True)
    return _dot(p_c.astype(BF16), v_c) + _dot(p_n.astype(BF16), v_n), l


def _fox_sample_kernel(q_ref, kn_ref, vn_ref, ck_ref, cv_ref, clf_ref, ustr_ref, o_ref, *, tn, past):
    q = q_ref[0]
    kn = _pad_rows(kn_ref[0], LANES)
    vn = _pad_rows(vn_ref[0], LANES)
    ck = ck_ref[0]
    cv = cv_ref[0].astype(BF16)
    x = clf_ref[0]
    xhi, xmid, xlo = _split3(x)
    ustr = ustr_ref[...]
    nblk = past // LANES
    carry = jnp.zeros((1, LANES), F32)
    sufs = [None] * nblk
    for blk in range(nblk - 1, -1, -1):
        sl = slice(LANES * blk, LANES * blk + LANES)
        s = _dot(ustr, xhi[sl]) + _dot(ustr, xmid[sl]) + _dot(ustr, xlo[sl]) + carry
        carry = s[0:1, :] + x[LANES * blk:LANES * blk + 1, :]
        sufs[blk] = s
    suf = jnp.concatenate(sufs, axis=0)

    lane = lax.broadcasted_iota(jnp.int32, (past, LANES), 1)
    lane_o = lax.broadcasted_iota(jnp.int32, (tn, LANES), 1)
    ones_k = jnp.where((lane >= HEAD_DIM) & (lane < HEAD_DIM + 3), 1.0, 0.0)
    row = lax.broadcasted_iota(jnp.int32, (tn, LANES), 0)
    causal = lane_o <= row
    for pr in range(2):
        outs = []
        for e in range(2):
            hd = 2 * pr + e
            base = ck[:, LANES * pr:LANES * pr + LANES]
            if e:
                base = pltpu.roll(base, HEAD_DIM, axis=1)
            sb = jnp.broadcast_to(suf[:, hd:hd + 1], (past, LANES))
            shi = sb.astype(BF16).astype(F32)
            r = sb - shi
            smid = r.astype(BF16).astype(F32)
            slo = (r - smid).astype(BF16).astype(F32)
            ak = jnp.where(lane == HEAD_DIM + 3, shi,
                           jnp.where(lane == HEAD_DIM + 4, smid, jnp.where(lane == HEAD_DIM + 5, slo, ones_k)))
            kc = jnp.where(lane < HEAD_DIM, base, ak).astype(BF16)
            qh = q[:, LANES * hd:LANES * hd + LANES]
            s_c = _nt_dot(qh, kc)
            s_n = jnp.where(causal, _nt_dot(qh, kn[:, LANES * hd:LANES * hd + LANES]), NEG)
            acc, l = _two_part_softmax(s_c, s_n, cv[:, LANES * pr:LANES * pr + LANES],
                                       vn[:, LANES * pr:LANES * pr + LANES])
            outs.append(acc / l)
        o_ref[0, :, LANES * pr:LANES * pr + LANES] = jnp.where(lane_o < HEAD_DIM, outs[0], outs[1]).astype(BF16)


def _fox_sample(qa, ka, vb, ck, cv, clf, ustr):
    b, tn, _ = qa.shape
    past = ck.shape[1]
    per_b = lambda i: (i, 0, 0)
    return pl.pallas_call(
        functools.partial(_fox_sample_kernel, tn=tn, past=past),
        grid=(b,),
        in_specs=[pl.BlockSpec((1, tn, 4 * LANES), per_b), pl.BlockSpec((1, tn, 4 * LANES), per_b),
                  pl.BlockSpec((1, tn, GROUP), per_b), pl.BlockSpec((1, past, GROUP), per_b),
                  pl.BlockSpec((1, past, GROUP), per_b), pl.BlockSpec((1, past, LANES), per_b),
                  pl.BlockSpec((LANES, LANES), lambda i: (0, 0))],
        out_specs=pl.BlockSpec((1, tn, GROUP), per_b),
        out_shape=jax.ShapeDtypeStruct((b, tn, GROUP), BF16),
        compiler_params=_cparams(("parallel",)),
        name="fox_sample",
    )(qa, ka, vb, ck, cv, clf, ustr)


def _diff_sample_kernel(q_ref, kn_ref, vn_ref, ck_ref, cv_ref, lam_ref, gsub_ref, o_ref, l_ref, acc_ref,
                        *, tn, past, lam_init):
    q = q_ref[0]
    kn = _pad_rows(kn_ref[0], LANES)
    vn = _pad_rows(vn_ref[0], LANES)
    ck = ck_ref[0].astype(BF16)
    cv = cv_ref[0].astype(BF16)
    row_c = lax.broadcasted_iota(jnp.int32, (tn, past), 0)
    col_c = lax.broadcasted_iota(jnp.int32, (tn, past), 1)
    dist_c = (past + row_c - col_c).astype(F32)
    row_n = lax.broadcasted_iota(jnp.int32, (tn, LANES), 0)
    col_n = lax.broadcasted_iota(jnp.int32, (tn, LANES), 1)
    dist_n = jnp.abs(row_n - col_n).astype(F32)
    real = col_n < tn
    for hd in range(N_HEADS):
        pr, e = divmod(hd, 2)
        slope = _alibi_slope(hd)
        for jm in range(2):
            idx = 4 * pr + 2 * e + jm
            qh = q[:, LANES * idx:LANES * idx + LANES]
            s_c = _nt_dot(qh, ck[:, LANES * pr:LANES * pr + LANES]) - slope * dist_c
            s_n = jnp.where(real, _nt_dot(qh, kn[:, LANES * pr:LANES * pr + LANES]) - slope * dist_n, NEG)
            acc, l = _two_part_softmax(s_c, s_n, cv[:, LANES * pr:LANES * pr + LANES],
                                       vn[:, LANES * pr:LANES * pr + LANES])
            acc_ref[2 * hd + jm] = acc
            l_ref[2 * hd + jm] = l
    lam = _diff_lambda(lam_ref[...], lam_init)
    _diff_finish(acc_ref, l_ref, lam, gsub_ref[...], lam_init, tn, o_ref)


def _diff_sample(qv, kb, vb, ck, cv, lam_p, gsub, *, lam_init):
    b, tn, _ = qv.shape
    past = ck.shape[1]
    per_b = lambda i: (i, 0, 0)
    return pl.pallas_call(
        functools.partial(_diff_sample_kernel, tn=tn, past=past, lam_init=lam_init),
        grid=(b,),
        in_specs=[pl.BlockSpec((1, tn, 8 * LANES), per_b), pl.BlockSpec((1, tn, GROUP), per_b),
                  pl.BlockSpec((1, tn, GROUP), per_b), pl.BlockSpec((1, past, GROUP), per_b),
                  pl.BlockSpec((1, past, GROUP), per_b), pl.BlockSpec((4, DIFF_HALF), lambda i: (0, 0)),
                  pl.BlockSpec((1, LANES), lambda i: (0, 0))],
        out_specs=pl.BlockSpec((1, tn, GROUP), per_b),
        out_shape=jax.ShapeDtypeStruct((b, tn, GROUP), BF16),
        scratch_shapes=[pltpu.VMEM((2 * N_HEADS, tn, 1), F32), pltpu.VMEM((2 * N_HEADS, tn, LANES), F32)],
        compiler_params=_cparams(("parallel",)),
        name="diff_sample",
    )(qv, kb, vb, ck, cv, lam_p, gsub)


def _band_sample_kernel(q_ref, kn_ref, vn_ref, ck_ref, cv_ref, bias_ref, o_ref, *, tn, past):
    q = q_ref[0]
    kn = _pad_rows(kn_ref[0], LANES)
    vn = _pad_rows(vn_ref[0], LANES)
    ck = ck_ref[0].astype(BF16)
    cv = cv_ref[0].astype(BF16)
    lane = lax.broadcasted_iota(jnp.int32, (tn, LANES), 1)
    real = lane < tn
    for pr in range(2):
        outs = []
        for e in range(2):
            hd = 2 * pr + e
            qh = q[:, LANES * hd:LANES * hd + LANES]
            s_c = _nt_dot(qh, ck[:, LANES * pr:LANES * pr + LANES]) + bias_ref[hd, 0:tn, 0:past]
            s_n = _nt_dot(qh, kn[:, LANES * pr:LANES * pr + LANES]) + bias_ref[hd, 0:tn, past:past + LANES]
            s_n = jnp.where(real, s_n, NEG)
            acc, l = _two_part_softmax(s_c, s_n, cv[:, LANES * pr:LANES * pr + LANES],
                                       vn[:, LANES * pr:LANES * pr + LANES])
            outs.append(acc / l)
        o_ref[0, :, LANES * pr:LANES * pr + LANES] = jnp.where(lane < HEAD_DIM, outs[0], outs[1]).astype(BF16)


def _band_sample(qm, kb, vb, ck, cv, bias):
    b, tn, _ = qm.shape
    past = ck.shape[1]
    per_b = lambda i: (i, 0, 0)
    return pl.pallas_call(
        functools.partial(_band_sample_kernel, tn=tn, past=past),
        grid=(b,),
        in_specs=[pl.BlockSpec((1, tn, 4 * LANES), per_b), pl.BlockSpec((1, tn, GROUP), per_b),
                  pl.BlockSpec((1, tn, GROUP), per_b), pl.BlockSpec((1, past, GROUP), per_b),
                  pl.BlockSpec((1, past, GROUP), per_b),
                  pl.BlockSpec((N_HEADS, CHUNK, BAND_WIN), lambda i: (0, 0, 0))],
        out_specs=pl.BlockSpec((1, tn, GROUP), per_b),
        out_shape=jax.ShapeDtypeStruct((b, tn, GROUP), BF16),
        compiler_params=_cparams(("parallel",)),
        name="band_sample",
    )(qm, kb, vb, ck, cv, bias)


def _out_proj_kernel(x_ref, gt_ref, a_ref, b_ref, c_ref, d_ref, w_ref, o_ref, *, bb, tt):
    acc = _dot(a_ref[...], w_ref[0:GROUP, :])
    acc += _dot(b_ref[...], w_ref[GROUP:2 * GROUP, :])
    acc += _dot(c_ref[...], w_ref[2 * GROUP:3 * GROUP, :])
    acc += _dot(d_ref[...], w_ref[3 * GROUP:4 * GROUP, :])
    o_ref[...] = x_ref[...] + gt_ref[...] * acc.reshape(bb, tt, D_MODEL)


def _out_proj(x, gt, o_fox, o_diff, o_ml, o_band, w_out, *, bb, tt):
    bx, tx, d = x.shape
    nb, nt = bx // bb, tx // tt
    tm = bb * tt
    tok = lambda i, t: (i * nt + t, 0)
    return pl.pallas_call(
        functools.partial(_out_proj_kernel, bb=bb, tt=tt),
        grid=(nb, nt),
        in_specs=[pl.BlockSpec((bb, tt, d), lambda i, t: (i, t, 0)),
                  pl.BlockSpec((bb, 1, d), lambda i, t: (i, 0, 0)),
                  pl.BlockSpec((tm, GROUP), tok), pl.BlockSpec((tm, GROUP), tok),
                  pl.BlockSpec((tm, GROUP), tok), pl.BlockSpec((tm, GROUP), tok),
                  pl.BlockSpec((d, d), lambda i, t: (0, 0))],
        out_specs=pl.BlockSpec((bb, tt, d), lambda i, t: (i, t, 0)),
        out_shape=jax.ShapeDtypeStruct(x.shape, F32),
        compiler_params=_cparams(("parallel", "parallel")),
        name="out_proj",
    )(x, gt, o_fox, o_diff, o_ml, o_band, w_out)


def _ffn_kernel(x_ref, sh_ref, sc_ref, gt_ref, g2_ref, wg_ref, wu_ref, wd_ref, o_ref, hb_ref, acc_ref,
                *, bb, tt):
    f = pl.program_id(2)
    tm = bb * tt

    @pl.when(f == 0)
    def _():
        x = x_ref[...]
        ms = jnp.mean(x * x, axis=-1, keepdims=True)
        h = x * lax.rsqrt(ms + EPS) * g2_ref[...]
        h = h * (1.0 + sc_ref[...]) + sh_ref[...]
        hb_ref[...] = h.reshape(tm, D_MODEL).astype(BF16)
        acc_ref[...] = jnp.zeros_like(acc_ref)

    hb = hb_ref[...]
    g = _dot(hb, wg_ref[...])
    u = _dot(hb, wu_ref[...])
    a = (g * jax.nn.sigmoid(g) * u).astype(BF16)
    acc_ref[...] += _dot(a, wd_ref[...])

    @pl.when(f == pl.num_programs(2) - 1)
    def _():
        o_ref[...] = x_ref[...] + gt_ref[...] * acc_ref[...].reshape(bb, tt, D_MODEL)


def _ffn(x, sh, sc, gt, g2, wg, wu, wd, *, bb, tt, tf):
    bx, tx, d = x.shape
    nb, nt = bx // bb, tx // tt
    nf = D_FF // tf
    tm = bb * tt
    xs = pl.BlockSpec((bb, tt, d), lambda i, t, f: (i, t, 0))
    ms = pl.BlockSpec((bb, 1, d), lambda i, t, f: (i, 0, 0))
    return pl.pallas_call(
        functools.partial(_ffn_kernel, bb=bb, tt=tt),
        grid=(nb, nt, nf),
        in_specs=[xs, ms, ms, ms, pl.BlockSpec((1, d), lambda i, t, f: (0, 0)),
                  pl.BlockSpec((d, tf), lambda i, t, f: (0, f)),
                  pl.BlockSpec((d, tf), lambda i, t, f: (0, f)),
                  pl.BlockSpec((tf, d), lambda i, t, f: (f, 0))],
        out_specs=xs,
        out_shape=jax.ShapeDtypeStruct(x.shape, F32),
        scratch_shapes=[pltpu.VMEM((tm, d), BF16), pltpu.VMEM((tm, d), F32)],
        compiler_params=_cparams(("parallel", "parallel", "arbitrary")),
        name="ffn",
    )(x, sh, sc, gt, g2, wg, wu, wd)


def _consts(seg):
    r = np.arange(GROUP)
    s64 = (r[:, None] // HEAD_DIM == r[None, :] // HEAD_DIM)
    s32 = (r[:, None] // DIFF_HALF == r[None, :] // DIFF_HALF)
    q = np.arange(LANES)
    tril = (q[None, :] <= q[:, None]) & (q[:, None] // seg == q[None, :] // seg)
    ustr = q[None, :] > q[:, None]
    as_bf16 = lambda m: jnp.asarray(m.astype(np.float32), dtype=BF16)
    return as_bf16(s64), as_bf16(s32), as_bf16(tril), as_bf16(ustr), jnp.asarray(s64.astype(np.float32))


def _prep_layer(l, norm1_g, norm2_g, w_in, b_in, qk_g_fox, qk_g_diff, qk_g_band, conv_w, conv_b,
                diff_lambda, diff_subln_g, mlstm_norm_g, band_rel_bias, w_out, w_ffn_gate, w_ffn_up,
                w_ffn_down):
    starts = np.concatenate([[0], np.cumsum(IN_SPLIT_SIZES)]).tolist()
    order = list(_FULL_GROUPS) + list(_GATE_GROUPS)
    wl, bl = w_in[l], b_in[l]
    w_cols = [wl[:, starts[g]:starts[g + 1]] for g in order]
    b_cols = [bl[starts[g]:starts[g + 1]] for g in order]
    pad = N_IN_PAD - N_FULL - N_GATES
    w_cols.append(jnp.zeros((D_MODEL, pad), F32))
    b_cols.append(jnp.zeros((pad,), F32))
    gains = jnp.stack([
        jnp.tile(qk_g_fox[l, 0], N_HEADS), jnp.tile(qk_g_fox[l, 1], N_HEADS),
        jnp.tile(qk_g_diff[l, 0], 2 * N_HEADS), jnp.tile(qk_g_diff[l, 1], 2 * N_HEADS),
        jnp.tile(qk_g_band[l, 0], N_HEADS), jnp.tile(qk_g_band[l, 1], N_HEADS),
        jnp.zeros((GROUP,), F32), jnp.zeros((GROUP,), F32)])
    return dict(
        g1=norm1_g[l].reshape(1, D_MODEL), g2=norm2_g[l].reshape(1, D_MODEL),
        w_in=jnp.concatenate(w_cols, axis=1).astype(BF16),
        b_in=jnp.concatenate(b_cols).reshape(1, N_IN_PAD),
        gains=gains, conv_w=conv_w[l], conv_b=conv_b[l].reshape(1, 2 * GROUP),
        lam_p=diff_lambda[l], gsub=jnp.tile(diff_subln_g[l], 2).reshape(1, LANES),
        gmh=jnp.tile(mlstm_norm_g[l], N_HEADS).reshape(1, GROUP),
        table=band_rel_bias[l],
        w_out=w_out[l].astype(BF16), wg=w_ffn_gate[l].astype(BF16), wu=w_ffn_up[l].astype(BF16),
        wd=w_ffn_down[l].astype(BF16),
        lam_init=0.8 - 0.6 * math.exp(-0.3 * l))


def _layer(x, mod, lp, caches, *, bb, tt):
    bx, tx, _ = x.shape
    n = bx * tx
    prompt = caches is None
    sh1, sc1, gt1, sh2, sc2, gt2 = [mod[:, i:i + 1, :] for i in range(6)]
    s64, s32, tril, ustr, bd = _consts(tx if prompt else min(tx, LANES))
    z = _in_proj(x, sh1, sc1, lp["g1"], lp["w_in"], lp["b_in"], lp["gains"], s64, s32, tril,
                 bb=bb, tt=tt, running=prompt, ml_blocks=ML_CHUNK // LANES)
    r3 = lambda a: a.reshape(bx, tx, a.shape[-1])
    bias_p, bias_s = _band_bias(lp["table"])

    if prompt:
        tq = 512
        o_fox = _fox_prompt(r3(z["fqa"]), r3(z["fka"]), r3(z["fvb"]), tq=tq)
        o_diff = _diff_prompt(r3(z["dqv"]), r3(z["dkb"]), r3(z["dvb"]), lp["lam_p"], lp["gsub"],
                              tq=tq, lam_init=lp["lam_init"])
        front = ((0, 0), (BAND_WIN, 0), (0, 0))
        o_band = _band_prompt(r3(z["bqm"]), jnp.pad(r3(z["bkb"]), front), jnp.pad(r3(z["bvb"]), front),
                              bias_p, tq=tq)
        c0 = jnp.zeros((bx, GROUP, GROUP), F32)
        n0 = jnp.zeros((bx, 1, GROUP), F32)
        m0 = jnp.zeros((bx, 1, GROUP), F32)
        conv0 = jnp.zeros((bx, 8, 2 * GROUP), F32)
        ml_valid, ml_rows = ML_CHUNK, ML_CHUNK
    else:
        (c_fk, c_fv, c_flf, c_dk, c_dv, c_bk, c_bv, s_c, s_n, s_m, s_conv) = caches
        past = c_fk.shape[1]
        flat = lambda a: a.reshape(a.shape[0], a.shape[1], GROUP)
        clf = jnp.pad(c_flf, ((0, 0), (0, 0), (0, LANES - N_HEADS)))
        o_fox = _fox_sample(r3(z["fqa"]), r3(z["fka"]), r3(z["fvb"]), flat(c_fk), flat(c_fv), clf, ustr)
        o_diff = _diff_sample(r3(z["dqv"]), r3(z["dkb"]), r3(z["dvb"]), flat(c_dk), flat(c_dv),
                              lp["lam_p"], lp["gsub"], lam_init=lp["lam_init"])
        o_band = _band_sample(r3(z["bqm"]), r3(z["bkb"]), r3(z["bvb"]), flat(c_bk), flat(c_bv), bias_s)
        eye = jnp.eye(N_HEADS, dtype=F32)
        c0 = (s_c[:, :, :, None, :] * eye[None, :, None, :, None]).reshape(bx, GROUP, GROUP)
        n0 = s_n.reshape(bx, 1, GROUP)
        m0 = jnp.repeat(s_m, HEAD_DIM, axis=-1).reshape(bx, 1, GROUP)
        conv0 = jnp.pad(s_conv, ((0, 0), (8 - (CONV_W - 1), 0), (0, 0)))
        ml_valid, ml_rows = tx, ML_PAD
    o_ml, c_new, n_new, m_new = _mlstm(
        z["mqk"], z["mvb"], z["gates"], z["cum"], z["mos"], lp["conv_w"], lp["conv_b"], lp["gmh"],
        s64, bd, c0, n0, m0, conv0, nb=bx, t=tx, valid=ml_valid, rows=ml_rows)

    flat2 = lambda a: a.reshape(n, GROUP)
    x1 = _out_proj(x, gt1, flat2(o_fox), flat2(o_diff), o_ml, flat2(o_band), lp["w_out"], bb=bb, tt=tt)
    x2 = _ffn(x1, sh2, sc2, gt2, lp["g2"], lp["wg"], lp["wu"], lp["wd"], bb=bb, tt=tt, tf=D_FF // 2)

    heads = lambda a: a.reshape(bx, tx, N_HEADS, HEAD_DIM)
    keep = min(BAND_ROWS, tx)
    mc = jnp.stack([c_new[:, HEAD_DIM * h:HEAD_DIM * h + HEAD_DIM, HEAD_DIM * h:HEAD_DIM * h + HEAD_DIM]
                    for h in range(N_HEADS)], axis=1)
    mqk3 = r3(z["mqk"])
    state = (heads(z["fk"]), heads(z["fv"]), r3(z["gates"])[:, :, 0:N_HEADS],
             z["dk"].reshape(bx, tx, N_HEADS, 2, DIFF_HALF), heads(z["dv"]),
             heads(z["bk"])[:, tx - keep:], heads(z["bv"])[:, tx - keep:],
             mc, n_new.reshape(bx, N_HEADS, HEAD_DIM), m_new.reshape(bx, N_HEADS, HEAD_DIM)[:, :, 0],
             mqk3[:, tx - (CONV_W - 1):, :])
    return x2, state


def kernel(x_prompt, x_sample, c_prompt, c_sample, cache_fox_k, cache_fox_v, cache_fox_logf, cache_diff_k, cache_diff_v, cache_band_k, cache_band_v, state_mlstm_c, state_mlstm_n, state_mlstm_m, state_conv, norm1_g, norm2_g, w_mod, b_mod, w_in, b_in, qk_g_fox, qk_g_diff, qk_g_band, conv_w, conv_b, diff_lambda, diff_subln_g, mlstm_norm_g, band_rel_bias, w_out, w_ffn_gate, w_ffn_up, w_ffn_down):
    depth = w_in.shape[0]
    bp, bs = x_prompt.shape[0], x_sample.shape[0]
    ts = x_sample.shape[1]
    caches = (cache_fox_k, cache_fox_v, cache_fox_logf, cache_diff_k, cache_diff_v,
              cache_band_k, cache_band_v, state_mlstm_c, state_mlstm_n, state_mlstm_m, state_conv)
    mod = _modulation(jnp.concatenate([c_prompt, c_sample], axis=0), w_mod, b_mod)
    mod = mod.reshape(depth, bp + bs, 6, D_MODEL)
    layers = [_prep_layer(l, norm1_g, norm2_g, w_in, b_in, qk_g_fox, qk_g_diff, qk_g_band, conv_w,
                          conv_b, diff_lambda, diff_subln_g, mlstm_norm_g, band_rel_bias, w_out,
                          w_ffn_gate, w_ffn_up, w_ffn_down) for l in range(depth)]
    sample_bb = 512 // ts
    xp, xs = x_prompt, x_sample
    p_out, s_out = [], []
    for l in range(depth):
        xp, st = _layer(xp, mod[l, :bp], layers[l], None, bb=1, tt=512)
        p_out.append(st)
    for l in range(depth):
        xs, st = _layer(xs, mod[l, bp:], layers[l], tuple(c[l] for c in caches), bb=sample_bb, tt=ts)
        s_out.append(st)
    p_st = [jnp.stack(zs) for zs in zip(*p_out)]
    s_st = [jnp.stack(zs) for zs in zip(*s_out)]
    return (xp, xs, *p_st, *s_st)
```

```python
import functools
import math

import numpy as np
import jax
import jax.numpy as jnp
from jax import lax
from jax.experimental import pallas as pl
from jax.experimental.pallas import tpu as pltpu

F32 = jnp.float32
BF16 = jnp.bfloat16

D_MODEL = 1024
HEAD_DIM = 64
N_HEADS = 4
GROUP = N_HEADS * HEAD_DIM
DIFF_HALF = HEAD_DIM // 2
CHUNK = 64
BAND_CHUNKS = 8
BAND_ROWS = BAND_CHUNKS * CHUNK
REL_CLIP = 128
CONV_W = 4
D_FF = 2816
EPS = 1e-6
NEG = -1e30

LANES = 128
PAIR = 2 * HEAD_DIM
N_GATES = 3 * N_HEADS
BAND_WIN = (BAND_CHUNKS + 2) * CHUNK
ML_CHUNK = 256
ML_PAD = 128
VMEM_LIMIT = 56 * 1024 * 1024

IN_SPLIT_SIZES = (GROUP, GROUP, GROUP, N_HEADS, GROUP, GROUP, GROUP, 2 * GROUP, GROUP,
                  N_HEADS, N_HEADS, GROUP, GROUP, GROUP, GROUP)
_FULL_GROUPS = (0, 1, 2, 4, 5, 6, 7, 8, 11, 12, 13, 14)
_GATE_GROUPS = (3, 9, 10)
N_FULL = 13 * GROUP
N_IN_PAD = N_FULL + LANES


def _cparams(sem):
    return pltpu.CompilerParams(dimension_semantics=sem, vmem_limit_bytes=VMEM_LIMIT)


def _nt_dot(a, b):
    return lax.dot_general(a, b, (((1,), (1,)), ((), ())), preferred_element_type=F32)


def _dot(a, b):
    return jnp.dot(a, b, preferred_element_type=F32)


def _split3(x):
    hi = x.astype(BF16)
    r = x - hi.astype(F32)
    mid = r.astype(BF16)
    lo = (r - mid.astype(F32)).astype(BF16)
    return hi, mid, lo


def _log_sigmoid(x):
    return jnp.minimum(x, 0.0) - jnp.log1p(jnp.exp(-jnp.abs(x)))


def _mod_kernel(c_ref, w_ref, b_ref, o_ref):
    c = c_ref[...]
    a = (c * jax.nn.sigmoid(c)).astype(BF16)
    o_ref[0] = _dot(a, w_ref[0].astype(BF16)) + b_ref[0]


def _modulation(c_all, w_mod, b_mod):
    depth, d, n = w_mod.shape
    rows = c_all.shape[0]
    tn = 768
    return pl.pallas_call(
        _mod_kernel,
        grid=(depth, n // tn),
        in_specs=[pl.BlockSpec((rows, d), lambda l, j: (0, 0)),
                  pl.BlockSpec((1, d, tn), lambda l, j: (l, 0, j)),
                  pl.BlockSpec((1, 1, tn), lambda l, j: (l, 0, j))],
        out_specs=pl.BlockSpec((1, rows, tn), lambda l, j: (l, 0, j)),
        out_shape=jax.ShapeDtypeStruct((depth, rows, n), F32),
        compiler_params=_cparams(("parallel", "parallel")),
        name="modulation",
    )(c_all, w_mod, b_mod.reshape(depth, 1, n))


_IN_OUTS = (
    ("fqa", 4 * LANES, BF16),
    ("fka", 4 * LANES, BF16),
    ("fk", GROUP, F32), ("fv", GROUP, F32), ("fvb", GROUP, BF16),
    ("dqv", 8 * LANES, BF16),
    ("dk", GROUP, F32), ("dkb", GROUP, BF16), ("dv", GROUP, F32), ("dvb", GROUP, BF16),
    ("mqk", 2 * GROUP, F32), ("mvb", GROUP, BF16), ("mos", GROUP, F32),
    ("bqm", 4 * LANES, BF16),
    ("bk", GROUP, F32), ("bkb", GROUP, BF16), ("bv", GROUP, F32), ("bvb", GROUP, BF16),
    ("gates", LANES, F32),
    ("cum", LANES, F32),
)


def _in_proj_kernel(x_ref, sh_ref, sc_ref, g1_ref, w_ref, b_ref, gains_ref, s64_ref, s32_ref, tril_ref,
                    fqa_ref, fka_ref, fk_ref, fv_ref, fvb_ref, dqv_ref, dk_ref, dkb_ref, dv_ref, dvb_ref,
                    mqk_ref, mvb_ref, mos_ref, bqm_ref, bk_ref, bkb_ref, bv_ref, bvb_ref, gates_ref,
                    cum_ref, carry_ref, *, bb, tt, running, ml_blocks):
    tm = bb * tt
    x = x_ref[...]
    ms = jnp.mean(x * x, axis=-1, keepdims=True)
    h = x * lax.rsqrt(ms + EPS) * g1_ref[...]
    h = h * (1.0 + sc_ref[...]) + sh_ref[...]
    hb = h.reshape(tm, D_MODEL).astype(BF16)

    def proj(g, width=GROUP):
        c0 = GROUP * g
        return _dot(hb, w_ref[:, c0:c0 + width]) + b_ref[:, c0:c0 + width]

    def rms_seg(z, s_ref, row, n):
        ss = _dot((z * z).astype(BF16), s_ref[...])
        return z * lax.rsqrt(ss * (1.0 / n) + EPS) * gains_ref[row:row + 1, :]

    lane = lax.broadcasted_iota(jnp.int32, (tm, LANES), 1)

    zg = proj(13, LANES)
    is_ls = (lane < N_HEADS) | ((lane >= 2 * N_HEADS) & (lane < N_GATES))
    gates = jnp.where(is_ls, _log_sigmoid(zg), zg)
    gates_ref[...] = gates
    ghi, gmid, glo = _split3(gates)
    tril = tril_ref[...]
    lane_b = lax.broadcasted_iota(jnp.int32, (LANES, LANES), 1)
    if running:
        @pl.when(pl.program_id(1) == 0)
        def _():
            carry_ref[...] = jnp.zeros_like(carry_ref)
        carry_a = carry_ref[0:1, :]
    carry_b = None
    cums = []
    for blk in range(tm // LANES):
        sl = slice(LANES * blk, LANES * blk + LANES)
        p = _dot(tril, ghi[sl]) + _dot(tril, gmid[sl]) + _dot(tril, glo[sl])
        if running:
            ca = p + carry_a
            carry_a = ca[LANES - 1:LANES, :]
            cb = p if blk % ml_blocks == 0 else p + carry_b
            carry_b = cb[LANES - 1:LANES, :]
            p = jnp.where(lane_b < N_HEADS, ca, cb)
        cums.append(p)
    if running:
        carry_ref[0:1, :] = carry_a
    cum = jnp.concatenate(cums, axis=0)
    cum_ref[...] = cum

    fqn = rms_seg(proj(0), s64_ref, 0, HEAD_DIM) * (HEAD_DIM ** -0.5)
    fkn = rms_seg(proj(1), s64_ref, 1, HEAD_DIM)
    fk_ref[...] = fkn
    fv = proj(2)
    fv_ref[...] = fv
    fvb_ref[...] = fv.astype(BF16)
    ones_q = jnp.where((lane >= HEAD_DIM + 3) & (lane < HEAD_DIM + 6), 1.0, 0.0)
    ones_k = jnp.where((lane >= HEAD_DIM) & (lane < HEAD_DIM + 3), 1.0, 0.0)
    for hd in range(N_HEADS):
        pr, e = divmod(hd, 2)
        bq = fqn[:, LANES * pr:LANES * pr + LANES]
        bk = fkn[:, LANES * pr:LANES * pr + LANES]
        if e:
            bq = pltpu.roll(bq, HEAD_DIM, axis=1)
            bk = pltpu.roll(bk, HEAD_DIM, axis=1)
        cbc = jnp.broadcast_to(cum[:, hd:hd + 1], (tm, LANES))
        chi = cbc.astype(BF16).astype(F32)
        r = cbc - chi
        cmid = r.astype(BF16).astype(F32)
        clo = (r - cmid).astype(BF16).astype(F32)
        aq = jnp.where(lane == HEAD_DIM, chi,
                       jnp.where(lane == HEAD_DIM + 1, cmid, jnp.where(lane == HEAD_DIM + 2, clo, ones_q)))
        ak = jnp.where(lane == HEAD_DIM + 3, -chi,
                       jnp.where(lane == HEAD_DIM + 4, -cmid, jnp.where(lane == HEAD_DIM + 5, -clo, ones_k)))
        fqa_ref[:, LANES * hd:LANES * hd + LANES] = jnp.where(lane < HEAD_DIM, bq, aq).astype(BF16)
        fka_ref[:, LANES * hd:LANES * hd + LANES] = jnp.where(lane < HEAD_DIM, bk, ak).astype(BF16)

    dqn = rms_seg(proj(3), s32_ref, 2, DIFF_HALF) * (DIFF_HALF ** -0.5)
    dkn = rms_seg(proj(4), s32_ref, 3, DIFF_HALF)
    dk_ref[...] = dkn
    dkb_ref[...] = dkn.astype(BF16)
    dv = proj(5)
    dv_ref[...] = dv
    dvb_ref[...] = dv.astype(BF16)
    for pr in range(2):
        blk = dqn[:, LANES * pr:LANES * pr + LANES]
        for e in range(2):
            for jm in range(2):
                l0 = HEAD_DIM * e + DIFF_HALF * jm
                idx = 4 * pr + 2 * e + jm
                dqv_ref[:, LANES * idx:LANES * idx + LANES] = jnp.where(
                    (lane >= l0) & (lane < l0 + DIFF_HALF), blk, 0.0).astype(BF16)

    mqk_ref[:, 0:GROUP] = proj(6)
    mqk_ref[:, GROUP:2 * GROUP] = proj(7)
    mvb_ref[...] = proj(8).astype(BF16)
    mos_ref[...] = jax.nn.sigmoid(proj(9))

    bqn = rms_seg(proj(10), s64_ref, 4, HEAD_DIM) * (HEAD_DIM ** -0.5)
    bkn = rms_seg(proj(11), s64_ref, 5, HEAD_DIM)
    bk_ref[...] = bkn
    bkb_ref[...] = bkn.astype(BF16)
    bv = proj(12)
    bv_ref[...] = bv
    bvb_ref[...] = bv.astype(BF16)
    for hd in range(N_HEADS):
        pr, e = divmod(hd, 2)
        blk = bqn[:, LANES * pr:LANES * pr + LANES]
        bqm_ref[:, LANES * hd:LANES * hd + LANES] = jnp.where(
            (lane >= HEAD_DIM * e) & (lane < HEAD_DIM * e + HEAD_DIM), blk, 0.0).astype(BF16)


def _in_proj(x, sh, sc, g1, w, b, gains, s64, s32, tril, *, bb, tt, running, ml_blocks):
    bx, tx, d = x.shape
    n = bx * tx
    nb, nt = bx // bb, tx // tt
    tm = bb * tt
    const = lambda i, t: (0, 0)
    tok = lambda i, t: (i * nt + t, 0)
    in_specs = [
        pl.BlockSpec((bb, tt, d), lambda i, t: (i, t, 0)),
        pl.BlockSpec((bb, 1, d), lambda i, t: (i, 0, 0)),
        pl.BlockSpec((bb, 1, d), lambda i, t: (i, 0, 0)),
        pl.BlockSpec((1, d), const),
        pl.BlockSpec((d, N_IN_PAD), const),
        pl.BlockSpec((1, N_IN_PAD), const),
        pl.BlockSpec((8, GROUP), const),
        pl.BlockSpec((GROUP, GROUP), const),
        pl.BlockSpec((GROUP, GROUP), const),
        pl.BlockSpec((LANES, LANES), const),
    ]
    out_specs = [pl.BlockSpec((tm, wd), tok) for _, wd, _ in _IN_OUTS]
    out_shape = [jax.ShapeDtypeStruct((n, wd), dt) for _, wd, dt in _IN_OUTS]
    outs = pl.pallas_call(
        functools.partial(_in_proj_kernel, bb=bb, tt=tt, running=running, ml_blocks=ml_blocks),
        grid=(nb, nt),
        in_specs=in_specs,
        out_specs=out_specs,
        out_shape=out_shape,
        scratch_shapes=[pltpu.VMEM((8, LANES), F32)],
        compiler_params=_cparams(("arbitrary", "arbitrary")),
        name="in_proj",
    )(x, sh, sc, g1, w, b, gains, s64, s32, tril)
    return {name: o for (name, _, _), o in zip(_IN_OUTS, outs)}


def _softmax_step(s, v, m_ref, l_ref, acc_ref, c):
    nt = s.shape[1] // LANES
    tiles = [s[:, LANES * t:LANES * t + LANES] for t in range(nt)]
    m_prev = m_ref[c]
    m_new = jnp.maximum(m_prev, jnp.max(s, axis=-1, keepdims=True))
    alpha = jnp.exp(m_prev - m_new)
    ps = [jnp.exp(t - m_new) for t in tiles]
    psum = ps[0]
    for p in ps[1:]:
        psum = psum + p
    l_ref[c] = alpha * l_ref[c] + psum
    p = jnp.concatenate([p.astype(BF16) for p in ps], axis=1)
    acc_ref[c] = alpha * acc_ref[c] + _dot(p, v)
    m_ref[c] = m_new


def _pipelined(n, make_s, consume):
    s_next = make_s(0)
    for c in range(n):
        s_cur = s_next
        if c + 1 < n:
            s_next = make_s(c + 1)
        consume(c, s_cur)


def _row_sum(l):
    return jnp.sum(l, axis=-1, keepdims=True)


def _fox_kernel(q_ref, k_ref, v_ref, o_ref, m_ref, l_ref, acc_ref, *, tq):
    i = pl.program_id(1)
    j = pl.program_id(2)

    @pl.when(j == 0)
    def _():
        m_ref[...] = jnp.full_like(m_ref, NEG)
        l_ref[...] = jnp.zeros_like(l_ref)
        acc_ref[...] = jnp.zeros_like(acc_ref)

    def step(diag):
        q = q_ref[0]
        k = k_ref[0]
        v = v_ref[0]
        if diag:
            row = lax.broadcasted_iota(jnp.int32, (tq, tq), 0)
            col = lax.broadcasted_iota(jnp.int32, (tq, tq), 1)
            keep = col <= row

        def make_s(hd):
            s = _nt_dot(q[:, LANES * hd:LANES * hd + LANES], k[:, LANES * hd:LANES * hd + LANES])
            return jnp.where(keep, s, NEG) if diag else s

        def consume(hd, s):
            pr = hd // 2
            _softmax_step(s, v[:, LANES * pr:LANES * pr + LANES], m_ref, l_ref, acc_ref, hd)

        _pipelined(N_HEADS, make_s, consume)

    @pl.when(j < i)
    def _():
        step(False)

    @pl.when(j == i)
    def _():
        step(True)
        lane = lax.broadcasted_iota(jnp.int32, (tq, LANES), 1)
        for pr in range(2):
            oe = acc_ref[2 * pr] / _row_sum(l_ref[2 * pr])
            oo = acc_ref[2 * pr + 1] / _row_sum(l_ref[2 * pr + 1])
            o_ref[0, :, LANES * pr:LANES * pr + LANES] = jnp.where(lane < HEAD_DIM, oe, oo).astype(BF16)


def _fox_prompt(qa, ka, vb, *, tq):
    b, t, _ = qa.shape
    nq = t // tq
    return pl.pallas_call(
        functools.partial(_fox_kernel, tq=tq),
        grid=(b, nq, nq),
        in_specs=[pl.BlockSpec((1, tq, 4 * LANES), lambda bi, i, j: (bi, i, 0)),
                  pl.BlockSpec((1, tq, 4 * LANES), lambda bi, i, j: (bi, jnp.minimum(i, j), 0)),
                  pl.BlockSpec((1, tq, GROUP), lambda bi, i, j: (bi, jnp.minimum(i, j), 0))],
        out_specs=pl.BlockSpec((1, tq, GROUP), lambda bi, i, j: (bi, i, 0)),
        out_shape=jax.ShapeDtypeStruct((b, t, GROUP), BF16),
        scratch_shapes=[pltpu.VMEM((N_HEADS, tq, LANES), F32), pltpu.VMEM((N_HEADS, tq, LANES), F32),
                        pltpu.VMEM((N_HEADS, tq, LANES), F32)],
        compiler_params=_cparams(("parallel", "parallel", "arbitrary")),
        name="fox_prompt",
    )(qa, ka, vb)


def _diff_lambda(lp, lam_init):
    a = jnp.sum(lp[0:1, :] * lp[1:2, :], axis=-1, keepdims=True)
    b = jnp.sum(lp[2:3, :] * lp[3:4, :], axis=-1, keepdims=True)
    return jnp.exp(a) - jnp.exp(b) + lam_init


def _diff_finish(acc_ref, l_ref, lam, gsub, lam_init, rows, o_ref):
    lane = lax.broadcasted_iota(jnp.int32, (rows, LANES), 1)
    for pr in range(2):
        outs = []
        for e in range(2):
            hd = 2 * pr + e
            o = (acc_ref[2 * hd] / _row_sum(l_ref[2 * hd])
                 - lam * (acc_ref[2 * hd + 1] / _row_sum(l_ref[2 * hd + 1])))
            valid = (lane >= HEAD_DIM * e) & (lane < HEAD_DIM * e + HEAD_DIM)
            ms = jnp.sum(jnp.where(valid, o * o, 0.0), axis=-1, keepdims=True) * (1.0 / HEAD_DIM)
            outs.append(o * lax.rsqrt(ms + EPS) * gsub * (1.0 - lam_init))
        o_ref[0, :, LANES * pr:LANES * pr + LANES] = jnp.where(lane < HEAD_DIM, outs[0], outs[1]).astype(BF16)


def _alibi_slope(hd):
    return 2.0 ** (-8.0 * (hd + 1) / N_HEADS)


def _diff_kernel(q_ref, k_ref, v_ref, lam_ref, gsub_ref, o_ref, m_ref, l_ref, acc_ref, *, tq, lam_init):
    i = pl.program_id(1)
    j = pl.program_id(2)

    @pl.when(j == 0)
    def _():
        m_ref[...] = jnp.full_like(m_ref, NEG)
        l_ref[...] = jnp.zeros_like(l_ref)
        acc_ref[...] = jnp.zeros_like(acc_ref)

    def step(diag):
        q = q_ref[0]
        k = k_ref[0]
        v = v_ref[0]
        row = lax.broadcasted_iota(jnp.int32, (tq, tq), 0)
        col = lax.broadcasted_iota(jnp.int32, (tq, tq), 1)
        dist = jnp.abs((i - j) * tq + row - col).astype(F32)
        if diag:
            seen = (col // CHUNK) <= (row // CHUNK)

        def make_s(c):
            hd, jm = divmod(c, 2)
            pr, e = divmod(hd, 2)
            idx = 4 * pr + 2 * e + jm
            s = _nt_dot(q[:, LANES * idx:LANES * idx + LANES], k[:, LANES * pr:LANES * pr + LANES])
            s = s - _alibi_slope(hd) * dist
            return jnp.where(seen, s, NEG) if diag else s

        def consume(c, s):
            pr = c // 4
            _softmax_step(s, v[:, LANES * pr:LANES * pr + LANES], m_ref, l_ref, acc_ref, c)

        _pipelined(2 * N_HEADS, make_s, consume)

    @pl.when(j < i)
    def _():
        step(False)

    @pl.when(j == i)
    def _():
        step(True)
        lam = _diff_lambda(lam_ref[...], lam_init)
        _diff_finish(acc_ref, l_ref, lam, gsub_ref[...], lam_init, tq, o_ref)


def _diff_prompt(qv, kb, vb, lam_p, gsub, *, tq, lam_init):
    b, t, _ = qv.shape
    nq = t // tq
    return pl.pallas_call(
        functools.partial(_diff_kernel, tq=tq, lam_init=lam_init),
        grid=(b, nq, nq),
        in_specs=[pl.BlockSpec((1, tq, 8 * LANES), lambda bi, i, j: (bi, i, 0)),
                  pl.BlockSpec((1, tq, GROUP), lambda bi, i, j: (bi, jnp.minimum(i, j), 0)),
                  pl.BlockSpec((1, tq, GROUP), lambda bi, i, j: (bi, jnp.minimum(i, j), 0)),
                  pl.BlockSpec((4, DIFF_HALF), lambda bi, i, j: (0, 0)),
                  pl.BlockSpec((1, LANES), lambda bi, i, j: (0, 0))],
        out_specs=pl.BlockSpec((1, tq, GROUP), lambda bi, i, j: (bi, i, 0)),
        out_shape=jax.ShapeDtypeStruct((b, t, GROUP), BF16),
        scratch_shapes=[pltpu.VMEM((2 * N_HEADS, tq, LANES), F32), pltpu.VMEM((2 * N_HEADS, tq, LANES), F32),
                        pltpu.VMEM((2 * N_HEADS, tq, LANES), F32)],
        compiler_params=_cparams(("parallel", "parallel", "arbitrary")),
        name="diff_prompt",
    )(qv, kb, vb, lam_p, gsub)


def _band_bias_kernel(g_ref, bp_ref, bs_ref):
    lane = lax.broadcasted_iota(jnp.int32, (CHUNK, BAND_WIN), 1)
    width = g_ref.shape[1]
    for hd in range(N_HEADS):
        x = jnp.broadcast_to(g_ref[hd:hd + 1, :], (CHUNK, width))
        rp = pltpu.roll(x, width - CHUNK + 1, axis=1, stride=1, stride_axis=0)
        bp_ref[hd] = jnp.where(lane < CHUNK, NEG, rp[:, :BAND_WIN])
        rs = pltpu.roll(x, width - 2 * CHUNK + 1, axis=1, stride=1, stride_axis=0)
        bs_ref[hd] = rs[:, :BAND_WIN]


def _band_bias(table):
    gv = jnp.concatenate([jnp.broadcast_to(table[:, 2 * REL_CLIP:], (N_HEADS, 4 * LANES)),
                          table[:, 2 * REL_CLIP - 1::-1]], axis=1)
    shp = jax.ShapeDtypeStruct((N_HEADS, CHUNK, BAND_WIN), F32)
    return pl.pallas_call(_band_bias_kernel, out_shape=(shp, shp), name="band_bias")(gv)


def _band_kernel(q_ref, k_ref, v_ref, bias_ref, o_ref, *, tq):
    t = pl.program_id(1)
    n_sub = tq // CHUNK
    lane = lax.broadcasted_iota(jnp.int32, (CHUNK, LANES), 1)
    col = lax.broadcasted_iota(jnp.int32, (CHUNK, BAND_WIN), 1)

    def window(ref, c, pr):
        start = pl.multiple_of((t * n_sub + c + 1) * CHUNK, CHUNK)
        return ref[0, pl.ds(start, BAND_WIN), LANES * pr:LANES * pr + LANES]

    def make_s(idx):
        c, hd = divmod(idx, N_HEADS)
        q = q_ref[0, CHUNK * c:CHUNK * c + CHUNK, LANES * hd:LANES * hd + LANES]
        valid = col + (t * n_sub + c - BAND_CHUNKS - 1) * CHUNK >= 0
        return jnp.where(valid, _nt_dot(q, window(k_ref, c, hd // 2)) + bias_ref[hd], NEG)

    even = {}

    def consume(idx, s):
        c, hd = divmod(idx, N_HEADS)
        pr, e = divmod(hd, 2)
        m = jnp.max(s, axis=-1, keepdims=True)
        p = jnp.exp(s - m)
        l = jnp.sum(p, axis=-1, keepdims=True)
        o = _dot(p.astype(BF16), window(v_ref, c, pr)) / l
        if e == 0:
            even[pr] = o
        else:
            o_ref[0, CHUNK * c:CHUNK * c + CHUNK, LANES * pr:LANES * pr + LANES] = jnp.where(
                lane < HEAD_DIM, even[pr], o).astype(BF16)

    _pipelined(n_sub * N_HEADS, make_s, consume)


def _band_prompt(qm, kpad, vpad, bias, *, tq):
    b, t, _ = qm.shape
    tp = kpad.shape[1]
    return pl.pallas_call(
        functools.partial(_band_kernel, tq=tq),
        grid=(b, t // tq),
        in_specs=[pl.BlockSpec((1, tq, 4 * LANES), lambda bi, i: (bi, i, 0)),
                  pl.BlockSpec((1, tp, GROUP), lambda bi, i: (bi, 0, 0)),
                  pl.BlockSpec((1, tp, GROUP), lambda bi, i: (bi, 0, 0)),
                  pl.BlockSpec((N_HEADS, CHUNK, BAND_WIN), lambda bi, i: (0, 0, 0))],
        out_specs=pl.BlockSpec((1, tq, GROUP), lambda bi, i: (bi, i, 0)),
        out_shape=jax.ShapeDtypeStruct((b, t, GROUP), BF16),
        compiler_params=_cparams(("parallel", "arbitrary")),
        name="band_prompt",
    )(qm, kpad, vpad, bias)


def _mlstm_kernel(mqk_ref, mv_ref, gates_ref, cum_ref, mos_ref, cw_ref, cb_ref, gmh_ref, shead_ref, bd_ref,
                  c0_ref, n0_ref, m0_ref, conv0_ref,
                  o_ref, c_out_ref, n_out_ref, m_out_ref,
                  cbuf_ref, c_ref, n_ref, m_ref, *, rows, valid):
    t = pl.program_id(1)

    @pl.when(t == 0)
    def _():
        c_ref[...] = c0_ref[0]
        n_ref[...] = n0_ref[0]
        m_ref[...] = m0_ref[0]
        cbuf_ref[0:8, :] = conv0_ref[0]

    def padded(a, fill=0.0):
        if valid == rows:
            return a
        return jnp.concatenate([a, jnp.full((rows - valid, a.shape[1]), fill, a.dtype)], axis=0)

    u = padded(mqk_ref[...])
    cbuf_ref[8:8 + rows, :] = u
    y = cb_ref[...] + cw_ref[3:4, :] * u
    for jw in range(CONV_W - 1):
        y = y + cw_ref[jw:jw + 1, :] * cbuf_ref[5 + jw:5 + jw + rows, :]
    cbuf_ref[0:8, :] = cbuf_ref[valid:valid + 8, :]
    qk = y * jax.nn.sigmoid(y)
    q = qk[:, 0:GROUP]
    k = qk[:, GROUP:2 * GROUP] * (HEAD_DIM ** -0.5)
    qb = q.astype(BF16)
    kb = k.astype(BF16)
    v = padded(mv_ref[...])
    mos = padded(mos_ref[...])

    g = gates_ref[...]
    cm = cum_ref[...]
    if valid != rows:
        g = padded(g, NEG)
        cm = jnp.concatenate([cm, jnp.broadcast_to(cm[valid - 1:valid, :], (rows - valid, LANES))], axis=0)
    g_t = g.T
    cm_t = cm.T

    row = lax.broadcasted_iota(jnp.int32, (rows, rows), 0)
    col = lax.broadcasted_iota(jnp.int32, (rows, rows), 1)
    causal = col <= row
    lane_g = lax.broadcasted_iota(jnp.int32, (rows, GROUP), 1)
    lane_r = lax.broadcasted_iota(jnp.int32, (1, GROUP), 1)
    m_prev_all = m_ref[...]

    zeros = jnp.zeros((rows, GROUP), F32)
    num, a_full, ws_full, mt_full, wend_full = zeros, zeros, zeros, zeros, zeros
    decay_lane = jnp.zeros((1, GROUP), F32)
    mnew_lane = jnp.zeros((1, GROUP), F32)
    head_masks = [(lane_g >= HEAD_DIM * hd) & (lane_g < HEAD_DIM * hd + HEAD_DIM) for hd in range(N_HEADS)]
    sqks = [_nt_dot(jnp.where(hm, q, 0.0).astype(BF16), kb) for hm in head_masks]
    c_old = c_ref[...]
    n_old = n_ref[...]
    shead = shead_ref[...]
    q_c = _dot(qb, c_old.astype(BF16))
    q_n = _dot((q * n_old).astype(BF16), shead)
    for hd in range(N_HEADS):
        hm = head_masks[hd]
        hm_r = (lane_r >= HEAD_DIM * hd) & (lane_r < HEAD_DIM * hd + HEAD_DIM)
        f_c = cm[:, 2 * N_HEADS + hd:2 * N_HEADS + hd + 1]
        ig_c = g[:, N_HEADS + hd:N_HEADS + hd + 1]
        f_r = cm_t[2 * N_HEADS + hd:2 * N_HEADS + hd + 1, :]
        ig_r = g_t[N_HEADS + hd:N_HEADS + hd + 1, :]
        m_prev = m_prev_all[:, HEAD_DIM * hd:HEAD_DIM * hd + 1]
        logw = jnp.where(causal, f_c - (f_r - ig_r), NEG)
        logb = f_c + m_prev
        m_t = jnp.maximum(logb, jnp.max(logw, axis=-1, keepdims=True))
        w = jnp.exp(logw - m_t) * sqks[hd]
        a = jnp.exp(logb - m_t)
        num = jnp.where(hm, _dot(w.astype(BF16), v), num)
        a_full = jnp.where(hm, a, a_full)
        ws_full = jnp.where(hm, jnp.sum(w, axis=-1, keepdims=True), ws_full)
        mt_full = jnp.where(hm, m_t, mt_full)
        f_end = f_c[rows - 1:rows, :]
        log_end = f_end - f_c + ig_c
        m_new = jnp.maximum(f_end + m_prev, jnp.max(log_end, axis=0, keepdims=True))
        wend_full = jnp.where(hm, jnp.exp(log_end - m_new), wend_full)
        decay_lane = jnp.where(hm_r, jnp.exp(f_end + m_prev - m_new), decay_lane)
        mnew_lane = jnp.where(hm_r, m_new, mnew_lane)

    num = num + a_full * q_c
    den = ws_full + a_full * q_n
    h = num / jnp.maximum(jnp.abs(den), jnp.exp(-mt_full))
    o = mos * h
    ss = _dot((o * o).astype(BF16), shead)
    on = o * lax.rsqrt(ss * (1.0 / HEAD_DIM) + EPS) * gmh_ref[...]
    o_ref[...] = on[0:valid, :].astype(BF16)

    wk = k * wend_full
    kv = lax.dot_general(wk.astype(BF16), v, (((0,), (0,)), ((), ())), preferred_element_type=F32)
    c_ref[...] = c_old * decay_lane + jnp.where(bd_ref[...] > 0.5, kv, 0.0)
    n_ref[...] = n_old * decay_lane + jnp.sum(wk, axis=0, keepdims=True)
    m_ref[...] = mnew_lane

    @pl.when(t == pl.num_programs(1) - 1)
    def _():
        c_out_ref[0] = c_ref[...]
        n_out_ref[0] = n_ref[...]
        m_out_ref[0] = m_ref[...]


def _mlstm(mqk, mvb, gates, cum, mos, cw, cb, gmh, shead, bd, c0, n0, m0, conv0, *, nb, t, valid, rows):
    nc = t // valid
    tok = lambda bi, i: (bi * nc + i, 0)
    const = lambda bi, i: (0, 0)
    per_b = lambda bi, i: (bi, 0, 0)
    return pl.pallas_call(
        functools.partial(_mlstm_kernel, rows=rows, valid=valid),
        grid=(nb, nc),
        in_specs=[pl.BlockSpec((valid, 2 * GROUP), tok), pl.BlockSpec((valid, GROUP), tok),
                  pl.BlockSpec((valid, LANES), tok), pl.BlockSpec((valid, LANES), tok),
                  pl.BlockSpec((valid, GROUP), tok),
                  pl.BlockSpec((CONV_W, 2 * GROUP), const), pl.BlockSpec((1, 2 * GROUP), const),
                  pl.BlockSpec((1, GROUP), const), pl.BlockSpec((GROUP, GROUP), const),
                  pl.BlockSpec((GROUP, GROUP), const),
                  pl.BlockSpec((1, GROUP, GROUP), per_b), pl.BlockSpec((1, 1, GROUP), per_b),
                  pl.BlockSpec((1, 1, GROUP), per_b), pl.BlockSpec((1, 8, 2 * GROUP), per_b)],
        out_specs=[pl.BlockSpec((valid, GROUP), tok), pl.BlockSpec((1, GROUP, GROUP), per_b),
                   pl.BlockSpec((1, 1, GROUP), per_b), pl.BlockSpec((1, 1, GROUP), per_b)],
        out_shape=[jax.ShapeDtypeStruct((nb * t, GROUP), BF16), jax.ShapeDtypeStruct((nb, GROUP, GROUP), F32),
                   jax.ShapeDtypeStruct((nb, 1, GROUP), F32), jax.ShapeDtypeStruct((nb, 1, GROUP), F32)],
        scratch_shapes=[pltpu.VMEM((8 + rows, 2 * GROUP), F32), pltpu.VMEM((GROUP, GROUP), F32),
                        pltpu.VMEM((1, GROUP), F32), pltpu.VMEM((1, GROUP), F32)],
        compiler_params=_cparams(("parallel", "arbitrary")),
        name="mlstm",
    )(mqk, mvb, gates, cum, mos, cw, cb, gmh, shead, bd, c0, n0, m0, conv0)


def _pad_rows(a, rows):
    return jnp.concatenate([a, jnp.zeros((rows - a.shape[0], a.shape[1]), a.dtype)], axis=0)


def _two_part_softmax(s_c, s_n, v_c, v_n):
    m = jnp.maximum(jnp.max(s_c, axis=-1, keepdims=True), jnp.max(s_n, axis=-1, keepdims=True))
    p_c = jnp.exp(s_c - m)
    p_n = jnp.exp(s_n - m)
    l = jnp.sum(p_c, axis=-1, keepdims=True) + jnp.sum(p_n, axis=-1, keepdims=True)
    return _dot(p_c.astype(BF16), v_c) + _dot(p_n.astype(BF16), v_n), l


def _fox_sample_kernel(q_ref, kn_ref, vn_ref, ck_ref, cv_ref, clf_ref, ustr_ref, o_ref, *, tn, past):
    q = q_ref[0]
    kn = _pad_rows(kn_ref[0], LANES)
    vn = _pad_rows(vn_ref[0], LANES)
    ck = ck_ref[0]
    cv = cv_ref[0].astype(BF16)
    x = clf_ref[0]
    xhi, xmid, xlo = _split3(x)
    ustr = ustr_ref[...]
    nblk = past // LANES
    carry = jnp.zeros((1, LANES), F32)
    sufs = [None] * nblk
    for blk in range(nblk - 1, -1, -1):
        sl = slice(LANES * blk, LANES * blk + LANES)
        s = _dot(ustr, xhi[sl]) + _dot(ustr, xmid[sl]) + _dot(ustr, xlo[sl]) + carry
        carry = s[0:1, :] + x[LANES * blk:LANES * blk + 1, :]
        sufs[blk] = s
    suf = jnp.concatenate(sufs, axis=0)

    lane = lax.broadcasted_iota(jnp.int32, (past, LANES), 1)
    lane_o = lax.broadcasted_iota(jnp.int32, (tn, LANES), 1)
    ones_k = jnp.where((lane >= HEAD_DIM) & (lane < HEAD_DIM + 3), 1.0, 0.0)
    row = lax.broadcasted_iota(jnp.int32, (tn, LANES), 0)
    causal = lane_o <= row
    for pr in range(2):
        outs = []
        for e in range(2):
            hd = 2 * pr + e
            base = ck[:, LANES * pr:LANES * pr + LANES]
            if e:
                base = pltpu.roll(base, HEAD_DIM, axis=1)
            sb = jnp.broadcast_to(suf[:, hd:hd + 1], (past, LANES))
            shi = sb.astype(BF16).astype(F32)
            r = sb - shi
            smid = r.astype(BF16).astype(F32)
            slo = (r - smid).astype(BF16).astype(F32)
            ak = jnp.where(lane == HEAD_DIM + 3, shi,
                           jnp.where(lane == HEAD_DIM + 4, smid, jnp.where(lane == HEAD_DIM + 5, slo, ones_k)))
            kc = jnp.where(lane < HEAD_DIM, base, ak).astype(BF16)
            qh = q[:, LANES * hd:LANES * hd + LANES]
            s_c = _nt_dot(qh, kc)
            s_n = jnp.where(causal, _nt_dot(qh, kn[:, LANES * hd:LANES * hd + LANES]), NEG)
            acc, l = _two_part_softmax(s_c, s_n, cv[:, LANES * pr:LANES * pr + LANES],
                                       vn[:, LANES * pr:LANES * pr + LANES])
            outs.append(acc / l)
        o_ref[0, :, LANES * pr:LANES * pr + LANES] = jnp.where(lane_o < HEAD_DIM, outs[0], outs[1]).astype(BF16)


def _fox_sample(qa, ka, vb, ck, cv, clf, ustr):
    b, tn, _ = qa.shape
    past = ck.shape[1]
    per_b = lambda i: (i, 0, 0)
    return pl.pallas_call(
        functools.partial(_fox_sample_kernel, tn=tn, past=past),
        grid=(b,),
        in_specs=[pl.BlockSpec((1, tn, 4 * LANES), per_b), pl.BlockSpec((1, tn, 4 * LANES), per_b),
                  pl.BlockSpec((1, tn, GROUP), per_b), pl.BlockSpec((1, past, GROUP), per_b),
                  pl.BlockSpec((1, past, GROUP), per_b), pl.BlockSpec((1, past, LANES), per_b),
                  pl.BlockSpec((LANES, LANES), lambda i: (0, 0))],
        out_specs=pl.BlockSpec((1, tn, GROUP), per_b),
        out_shape=jax.ShapeDtypeStruct((b, tn, GROUP), BF16),
        compiler_params=_cparams(("parallel",)),
        name="fox_sample",
    )(qa, ka, vb, ck, cv, clf, ustr)


def _diff_sample_kernel(q_ref, kn_ref, vn_ref, ck_ref, cv_ref, lam_ref, gsub_ref, o_ref, l_ref, acc_ref,
                        *, tn, past, lam_init):
    q = q_ref[0]
    kn = _pad_rows(kn_ref[0], LANES)
    vn = _pad_rows(vn_ref[0], LANES)
    ck = ck_ref[0].astype(BF16)
    cv = cv_ref[0].astype(BF16)
    row_c = lax.broadcasted_iota(jnp.int32, (tn, past), 0)
    col_c = lax.broadcasted_iota(jnp.int32, (tn, past), 1)
    dist_c = (past + row_c - col_c).astype(F32)
    row_n = lax.broadcasted_iota(jnp.int32, (tn, LANES), 0)
    col_n = lax.broadcasted_iota(jnp.int32, (tn, LANES), 1)
    dist_n = jnp.abs(row_n - col_n).astype(F32)
    real = col_n < tn
    for hd in range(N_HEADS):
        pr, e = divmod(hd, 2)
        slope = _alibi_slope(hd)
        for jm in range(2):
            idx = 4 * pr + 2 * e + jm
            qh = q[:, LANES * idx:LANES * idx + LANES]
            s_c = _nt_dot(qh, ck[:, LANES * pr:LANES * pr + LANES]) - slope * dist_c
            s_n = jnp.where(real, _nt_dot(qh, kn[:, LANES * pr:LANES * pr + LANES]) - slope * dist_n, NEG)
            acc, l = _two_part_softmax(s_c, s_n, cv[:, LANES * pr:LANES * pr + LANES],
                                       vn[:, LANES * pr:LANES * pr + LANES])
            acc_ref[2 * hd + jm] = acc
            l_ref[2 * hd + jm] = l
    lam = _diff_lambda(lam_ref[...], lam_init)
    _diff_finish(acc_ref, l_ref, lam, gsub_ref[...], lam_init, tn, o_ref)


def _diff_sample(qv, kb, vb, ck, cv, lam_p, gsub, *, lam_init):
    b, tn, _ = qv.shape
    past = ck.shape[1]
    per_b = lambda i: (i, 0, 0)
    return pl.pallas_call(
        functools.partial(_diff_sample_kernel, tn=tn, past=past, lam_init=lam_init),
        grid=(b,),
        in_specs=[pl.BlockSpec((1, tn, 8 * LANES), per_b), pl.BlockSpec((1, tn, GROUP), per_b),
                  pl.BlockSpec((1, tn, GROUP), per_b), pl.BlockSpec((1, past, GROUP), per_b),
                  pl.BlockSpec((1, past, GROUP), per_b), pl.BlockSpec((4, DIFF_HALF), lambda i: (0, 0)),
                  pl.BlockSpec((1, LANES), lambda i: (0, 0))],
        out_specs=pl.BlockSpec((1, tn, GROUP), per_b),
        out_shape=jax.ShapeDtypeStruct((b, tn, GROUP), BF16),
        scratch_shapes=[pltpu.VMEM((2 * N_HEADS, tn, 1), F32), pltpu.VMEM((2 * N_HEADS, tn, LANES), F32)],
        compiler_params=_cparams(("parallel",)),
        name="diff_sample",
    )(qv, kb, vb, ck, cv, lam_p, gsub)


def _band_sample_kernel(q_ref, kn_ref, vn_ref, ck_ref, cv_ref, bias_ref, o_ref, *, tn, past):
    q = q_ref[0]
    kn = _pad_rows(kn_ref[0], LANES)
    vn = _pad_rows(vn_ref[0], LANES)
    ck = ck_ref[0].astype(BF16)
    cv = cv_ref[0].astype(BF16)
    lane = lax.broadcasted_iota(jnp.int32, (tn, LANES), 1)
    real = lane < tn
    for pr in range(2):
        outs = []
        for e in range(2):
            hd = 2 * pr + e
            qh = q[:, LANES * hd:LANES * hd + LANES]
            s_c = _nt_dot(qh, ck[:, LANES * pr:LANES * pr + LANES]) + bias_ref[hd, 0:tn, 0:past]
            s_n = _nt_dot(qh, kn[:, LANES * pr:LANES * pr + LANES]) + bias_ref[hd, 0:tn, past:past + LANES]
            s_n = jnp.where(real, s_n, NEG)
            acc, l = _two_part_softmax(s_c, s_n, cv[:, LANES * pr:LANES * pr + LANES],
                                       vn[:, LANES * pr:LANES * pr + LANES])
            outs.append(acc / l)
        o_ref[0, :, LANES * pr:LANES * pr + LANES] = jnp.where(lane < HEAD_DIM, outs[0], outs[1]).astype(BF16)


def _band_sample(qm, kb, vb, ck, cv, bias):
    b, tn, _ = qm.shape
    past = ck.shape[1]
    per_b = lambda i: (i, 0, 0)
    return pl.pallas_call(
        functools.partial(_band_sample_kernel, tn=tn, past=past),
        grid=(b,),
        in_specs=[pl.BlockSpec((1, tn, 4 * LANES), per_b), pl.BlockSpec((1, tn, GROUP), per_b),
                  pl.BlockSpec((1, tn, GROUP), per_b), pl.BlockSpec((1, past, GROUP), per_b),
                  pl.BlockSpec((1, past, GROUP), per_b),
                  pl.BlockSpec((N_HEADS, CHUNK, BAND_WIN), lambda i: (0, 0, 0))],
        out_specs=pl.BlockSpec((1, tn, GROUP), per_b),
        out_shape=jax.ShapeDtypeStruct((b, tn, GROUP), BF16),
        compiler_params=_cparams(("parallel",)),
        name="band_sample",
    )(qm, kb, vb, ck, cv, bias)


def _out_proj_kernel(x_ref, gt_ref, a_ref, b_ref, c_ref, d_ref, w_ref, o_ref, *, bb, tt):
    acc = _dot(a_ref[...], w_ref[0:GROUP, :])
    acc += _dot(b_ref[...], w_ref[GROUP:2 * GROUP, :])
    acc += _dot(c_ref[...], w_ref[2 * GROUP:3 * GROUP, :])
    acc += _dot(d_ref[...], w_ref[3 * GROUP:4 * GROUP, :])
    o_ref[...] = x_ref[...] + gt_ref[...] * acc.reshape(bb, tt, D_MODEL)


def _out_proj(x, gt, o_fox, o_diff, o_ml, o_band, w_out, *, bb, tt):
    bx, tx, d = x.shape
    nb, nt = bx // bb, tx // tt
    tm = bb * tt
    tok = lambda i, t: (i * nt + t, 0)
    return pl.pallas_call(
        functools.partial(_out_proj_kernel, bb=bb, tt=tt),
        grid=(nb, nt),
        in_specs=[pl.BlockSpec((bb, tt, d), lambda i, t: (i, t, 0)),
                  pl.BlockSpec((bb, 1, d), lambda i, t: (i, 0, 0)),
                  pl.BlockSpec((tm, GROUP), tok), pl.BlockSpec((tm, GROUP), tok),
                  pl.BlockSpec((tm, GROUP), tok), pl.BlockSpec((tm, GROUP), tok),
                  pl.BlockSpec((d, d), lambda i, t: (0, 0))],
        out_specs=pl.BlockSpec((bb, tt, d), lambda i, t: (i, t, 0)),
        out_shape=jax.ShapeDtypeStruct(x.shape, F32),
        compiler_params=_cparams(("parallel", "parallel")),
        name="out_proj",
    )(x, gt, o_fox, o_diff, o_ml, o_band, w_out)


def _ffn_kernel(x_ref, sh_ref, sc_ref, gt_ref, g2_ref, wg_ref, wu_ref, wd_ref, o_ref, hb_ref, acc_ref,
                *, bb, tt):
    f = pl.program_id(2)
    tm = bb * tt

    @pl.when(f == 0)
    def _():
        x = x_ref[...]
        ms = jnp.mean(x * x, axis=-1, keepdims=True)
        h = x * lax.rsqrt(ms + EPS) * g2_ref[...]
        h = h * (1.0 + sc_ref[...]) + sh_ref[...]
        hb_ref[...] = h.reshape(tm, D_MODEL).astype(BF16)
        acc_ref[...] = jnp.zeros_like(acc_ref)

    hb = hb_ref[...]
    g = _dot(hb, wg_ref[...])
    u = _dot(hb, wu_ref[...])
    a = (g * jax.nn.sigmoid(g) * u).astype(BF16)
    acc_ref[...] += _dot(a, wd_ref[...])

    @pl.when(f == pl.num_programs(2) - 1)
    def _():
        o_ref[...] = x_ref[...] + gt_ref[...] * acc_ref[...].reshape(bb, tt, D_MODEL)


def _ffn(x, sh, sc, gt, g2, wg, wu, wd, *, bb, tt, tf):
    bx, tx, d = x.shape
    nb, nt = bx // bb, tx // tt
    nf = D_FF // tf
    tm = bb * tt
    xs = pl.BlockSpec((bb, tt, d), lambda i, t, f: (i, t, 0))
    ms = pl.BlockSpec((bb, 1, d), lambda i, t, f: (i, 0, 0))
    return pl.pallas_call(
        functools.partial(_ffn_kernel, bb=bb, tt=tt),
        grid=(nb, nt, nf),
        in_specs=[xs, ms, ms, ms, pl.BlockSpec((1, d), lambda i, t, f: (0, 0)),
                  pl.BlockSpec((d, tf), lambda i, t, f: (0, f)),
                  pl.BlockSpec((d, tf), lambda i, t, f: (0, f)),
                  pl.BlockSpec((tf, d), lambda i, t, f: (f, 0))],
        out_specs=xs,
        out_shape=jax.ShapeDtypeStruct(x.shape, F32),
        scratch_shapes=[pltpu.VMEM((tm, d), BF16), pltpu.VMEM((tm, d), F32)],
        compiler_params=_cparams(("parallel", "parallel", "arbitrary")),
        name="ffn",
    )(x, sh, sc, gt, g2, wg, wu, wd)


def _consts(seg):
    r = np.arange(GROUP)
    s64 = (r[:, None] // HEAD_DIM == r[None, :] // HEAD_DIM)
    s32 = (r[:, None] // DIFF_HALF == r[None, :] // DIFF_HALF)
    q = np.arange(LANES)
    tril = (q[None, :] <= q[:, None]) & (q[:, None] // seg == q[None, :] // seg)
    ustr = q[None, :] > q[:, None]
    as_bf16 = lambda m: jnp.asarray(m.astype(np.float32), dtype=BF16)
    return as_bf16(s64), as_bf16(s32), as_bf16(tril), as_bf16(ustr), jnp.asarray(s64.astype(np.float32))


def _prep_layer(l, norm1_g, norm2_g, w_in, b_in, qk_g_fox, qk_g_diff, qk_g_band, conv_w, conv_b,
                diff_lambda, diff_subln_g, mlstm_norm_g, band_rel_bias, w_out, w_ffn_gate, w_ffn_up,
                w_ffn_down):
    starts = np.concatenate([[0], np.cumsum(IN_SPLIT_SIZES)]).tolist()
    order = list(_FULL_GROUPS) + list(_GATE_GROUPS)
    wl, bl = w_in[l], b_in[l]
    w_cols = [wl[:, starts[g]:starts[g + 1]] for g in order]
    b_cols = [bl[starts[g]:starts[g + 1]] for g in order]
    pad = N_IN_PAD - N_FULL - N_GATES
    w_cols.append(jnp.zeros((D_MODEL, pad), F32))
    b_cols.append(jnp.zeros((pad,), F32))
    gains = jnp.stack([
        jnp.tile(qk_g_fox[l, 0], N_HEADS), jnp.tile(qk_g_fox[l, 1], N_HEADS),
        jnp.tile(qk_g_diff[l, 0], 2 * N_HEADS), jnp.tile(qk_g_diff[l, 1], 2 * N_HEADS),
        jnp.tile(qk_g_band[l, 0], N_HEADS), jnp.tile(qk_g_band[l, 1], N_HEADS),
        jnp.zeros((GROUP,), F32), jnp.zeros((GROUP,), F32)])
    return dict(
        g1=norm1_g[l].reshape(1, D_MODEL), g2=norm2_g[l].reshape(1, D_MODEL),
        w_in=jnp.concatenate(w_cols, axis=1).astype(BF16),
        b_in=jnp.concatenate(b_cols).reshape(1, N_IN_PAD),
        gains=gains, conv_w=conv_w[l], conv_b=conv_b[l].reshape(1, 2 * GROUP),
        lam_p=diff_lambda[l], gsub=jnp.tile(diff_subln_g[l], 2).reshape(1, LANES),
        gmh=jnp.tile(mlstm_norm_g[l], N_HEADS).reshape(1, GROUP),
        table=band_rel_bias[l],
        w_out=w_out[l].astype(BF16), wg=w_ffn_gate[l].astype(BF16), wu=w_ffn_up[l].astype(BF16),
        wd=w_ffn_down[l].astype(BF16),
        lam_init=0.8 - 0.6 * math.exp(-0.3 * l))


def _layer(x, mod, lp, caches, *, bb, tt):
    bx, tx, _ = x.shape
    n = bx * tx
    prompt = caches is None
    sh1, sc1, gt1, sh2, sc2, gt2 = [mod[:, i:i + 1, :] for i in range(6)]
    s64, s32, tril, ustr, bd = _consts(tx if prompt else min(tx, LANES))
    z = _in_proj(x, sh1, sc1, lp["g1"], lp["w_in"], lp["b_in"], lp["gains"], s64, s32, tril,
                 bb=bb, tt=tt, running=prompt, ml_blocks=ML_CHUNK // LANES)
    r3 = lambda a: a.reshape(bx, tx, a.shape[-1])
    bias_p, bias_s = _band_bias(lp["table"])

    if prompt:
        tq = 512
        o_fox = _fox_prompt(r3(z["fqa"]), r3(z["fka"]), r3(z["fvb"]), tq=tq)
        o_diff = _diff_prompt(r3(z["dqv"]), r3(z["dkb"]), r3(z["dvb"]), lp["lam_p"], lp["gsub"],
                              tq=tq, lam_init=lp["lam_init"])
        front = ((0, 0), (BAND_WIN, 0), (0, 0))
        o_band = _band_prompt(r3(z["bqm"]), jnp.pad(r3(z["bkb"]), front), jnp.pad(r3(z["bvb"]), front),
                              bias_p, tq=tq)
        c0 = jnp.zeros((bx, GROUP, GROUP), F32)
        n0 = jnp.zeros((bx, 1, GROUP), F32)
        m0 = jnp.zeros((bx, 1, GROUP), F32)
        conv0 = jnp.zeros((bx, 8, 2 * GROUP), F32)
        ml_valid, ml_rows = ML_CHUNK, ML_CHUNK
    else:
        (c_fk, c_fv, c_flf, c_dk, c_dv, c_bk, c_bv, s_c, s_n, s_m, s_conv) = caches
        past = c_fk.shape[1]
        flat = lambda a: a.reshape(a.shape[0], a.shape[1], GROUP)
        clf = jnp.pad(c_flf, ((0, 0), (0, 0), (0, LANES - N_HEADS)))
        o_fox = _fox_sample(r3(z["fqa"]), r3(z["fka"]), r3(z["fvb"]), flat(c_fk), flat(c_fv), clf, ustr)
        o_diff = _diff_sample(r3(z["dqv"]), r3(z["dkb"]), r3(z["dvb"]), flat(c_dk), flat(c_dv),
                              lp["lam_p"], lp["gsub"], lam_init=lp["lam_init"])
        o_band = _band_sample(r3(z["bqm"]), r3(z["bkb"]), r3(z["bvb"]), flat(c_bk), flat(c_bv), bias_s)
        eye = jnp.eye(N_HEADS, dtype=F32)
        c0 = (s_c[:, :, :, None, :] * eye[None, :, None, :, None]).reshape(bx, GROUP, GROUP)
        n0 = s_n.reshape(bx, 1, GROUP)
        m0 = jnp.repeat(s_m, HEAD_DIM, axis=-1).reshape(bx, 1, GROUP)
        conv0 = jnp.pad(s_conv, ((0, 0), (8 - (CONV_W - 1), 0), (0, 0)))
        ml_valid, ml_rows = tx, ML_PAD
    o_ml, c_new, n_new, m_new = _mlstm(
        z["mqk"], z["mvb"], z["gates"], z["cum"], z["mos"], lp["conv_w"], lp["conv_b"], lp["gmh"],
        s64, bd, c0, n0, m0, conv0, nb=bx, t=tx, valid=ml_valid, rows=ml_rows)

    flat2 = lambda a: a.reshape(n, GROUP)
    x1 = _out_proj(x, gt1, flat2(o_fox), flat2(o_diff), o_ml, flat2(o_band), lp["w_out"], bb=bb, tt=tt)
    x2 = _ffn(x1, sh2, sc2, gt2, lp["g2"], lp["wg"], lp["wu"], lp["wd"], bb=bb, tt=tt, tf=D_FF // 2)

    heads = lambda a: a.reshape(bx, tx, N_HEADS, HEAD_DIM)
    keep = min(BAND_ROWS, tx)
    mc = jnp.stack([c_new[:, HEAD_DIM * h:HEAD_DIM * h + HEAD_DIM, HEAD_DIM * h:HEAD_DIM * h + HEAD_DIM]
                    for h in range(N_HEADS)], axis=1)
    mqk3 = r3(z["mqk"])
    state = (heads(z["fk"]), heads(z["fv"]), r3(z["gates"])[:, :, 0:N_HEADS],
             z["dk"].reshape(bx, tx, N_HEADS, 2, DIFF_HALF), heads(z["dv"]),
             heads(z["bk"])[:, tx - keep:], heads(z["bv"])[:, tx - keep:],
             mc, n_new.reshape(bx, N_HEADS, HEAD_DIM), m_new.reshape(bx, N_HEADS, HEAD_DIM)[:, :, 0],
             mqk3[:, tx - (CONV_W - 1):, :])
    return x2, state


def kernel(x_prompt, x_sample, c_prompt, c_sample, cache_fox_k, cache_fox_v, cache_fox_logf, cache_diff_k, cache_diff_v, cache_band_k, cache_band_v, state_mlstm_c, state_mlstm_n, state_mlstm_m, state_conv, norm1_g, norm2_g, w_mod, b_mod, w_in, b_in, qk_g_fox, qk_g_diff, qk_g_band, conv_w, conv_b, diff_lambda, diff_subln_g, mlstm_norm_g, band_rel_bias, w_out, w_ffn_gate, w_ffn_up, w_ffn_down):
    depth = w_in.shape[0]
    bp, bs = x_prompt.shape[0], x_sample.shape[0]
    ts = x_sample.shape[1]
    caches = (cache_fox_k, cache_fox_v, cache_fox_logf, cache_diff_k, cache_diff_v,
              cache_band_k, cache_band_v, state_mlstm_c, state_mlstm_n, state_mlstm_m, state_conv)
    mod = _modulation(jnp.concatenate([c_prompt, c_sample], axis=0), w_mod, b_mod)
    mod = mod.reshape(depth, bp + bs, 6, D_MODEL)
    layers = [_prep_layer(l, norm1_g, norm2_g, w_in, b_in, qk_g_fox, qk_g_diff, qk_g_band, conv_w,
                          conv_b, diff_lambda, diff_subln_g, mlstm_norm_g, band_rel_bias, w_out,
                          w_ffn_gate, w_ffn_up, w_ffn_down) for l in range(depth)]
    sample_bb = 512 // ts
    xp, xs = x_prompt, x_sample
    p_out, s_out = [], []
    for l in range(depth):
        xp, st = _layer(xp, mod[l, :bp], layers[l], None, bb=1, tt=512)
        p_out.append(st)
    for l in range(depth):
        xs, st = _layer(xs, mod[l, bp:], layers[l], tuple(c[l] for c in caches), bb=sample_bb, tt=ts)
        s_out.append(st)
    p_st = [jnp.stack(zs) for zs in zip(*p_out)]
    s_st = [jnp.stack(zs) for zs in zip(*s_out)]
    return (xp, xs, *p_st, *s_st)
```

```python
import functools
import math

import numpy as np
import jax
import jax.numpy as jnp
from jax import lax
from jax.experimental import pallas as pl
from jax.experimental.pallas import tpu as pltpu

F32 = jnp.float32
BF16 = jnp.bfloat16

D_MODEL = 1024
HEAD_DIM = 64
N_HEADS = 4
GROUP = N_HEADS * HEAD_DIM
DIFF_HALF = HEAD_DIM // 2
CHUNK = 64
BAND_CHUNKS = 8
BAND_ROWS = BAND_CHUNKS * CHUNK
REL_CLIP = 128
CONV_W = 4
D_FF = 2816
EPS = 1e-6
NEG = -1e30

LANES = 128
PAIR = 2 * HEAD_DIM
N_GATES = 3 * N_HEADS
BAND_WIN = (BAND_CHUNKS + 2) * CHUNK
ML_CHUNK = 256
ML_PAD = 128
VMEM_LIMIT = 56 * 1024 * 1024

IN_SPLIT_SIZES = (GROUP, GROUP, GROUP, N_HEADS, GROUP, GROUP, GROUP, 2 * GROUP, GROUP,
                  N_HEADS, N_HEADS, GROUP, GROUP, GROUP, GROUP)
_FULL_GROUPS = (0, 1, 2, 4, 5, 6, 7, 8, 11, 12, 13, 14)
_GATE_GROUPS = (3, 9, 10)
N_FULL = 13 * GROUP
N_IN_PAD = N_FULL + LANES


def _cparams(sem):
    return pltpu.CompilerParams(dimension_semantics=sem, vmem_limit_bytes=VMEM_LIMIT)


def _nt_dot(a, b):
    return lax.dot_general(a, b, (((1,), (1,)), ((), ())), preferred_element_type=F32)


def _dot(a, b):
    return jnp.dot(a, b, preferred_element_type=F32)


def _split3(x):
    hi = x.astype(BF16)
    r = x - hi.astype(F32)
    mid = r.astype(BF16)
    lo = (r - mid.astype(F32)).astype(BF16)
    return hi, mid, lo


def _log_sigmoid(x):
    return jnp.minimum(x, 0.0) - jnp.log1p(jnp.exp(-jnp.abs(x)))


def _mod_kernel(c_ref, w_ref, b_ref, o_ref):
    c = c_ref[...]
    a = (c * jax.nn.sigmoid(c)).astype(BF16)
    o_ref[0] = _dot(a, w_ref[0].astype(BF16)) + b_ref[0]


def _modulation(c_all, w_mod, b_mod):
    depth, d, n = w_mod.shape
    rows = c_all.shape[0]
    tn = 768
    return pl.pallas_call(
        _mod_kernel,
        grid=(depth, n // tn),
        in_specs=[pl.BlockSpec((rows, d), lambda l, j: (0, 0)),
                  pl.BlockSpec((1, d, tn), lambda l, j: (l, 0, j)),
                  pl.BlockSpec((1, 1, tn), lambda l, j: (l, 0, j))],
        out_specs=pl.BlockSpec((1, rows, tn), lambda l, j: (l, 0, j)),
        out_shape=jax.ShapeDtypeStruct((depth, rows, n), F32),
        compiler_params=_cparams(("parallel", "parallel")),
        name="modulation",
    )(c_all, w_mod, b_mod.reshape(depth, 1, n))


def _regroup_kernel(w_ref, o_ref):
    starts = np.concatenate([[0], np.cumsum(IN_SPLIT_SIZES)]).tolist()
    col = 0
    for g in _FULL_GROUPS + _GATE_GROUPS:
        width = IN_SPLIT_SIZES[g]
        o_ref[0, :, col:col + width] = w_ref[0, :, starts[g]:starts[g] + width].astype(BF16)
        col += width
    o_ref[0, :, col:N_IN_PAD] = jnp.zeros((o_ref.shape[1], N_IN_PAD - col), BF16)


def _regroup_w_in(w_in):
    depth, d, n_in = w_in.shape
    tr = 256
    return pl.pallas_call(
        _regroup_kernel,
        grid=(depth, d // tr),
        in_specs=[pl.BlockSpec((1, tr, n_in), lambda l, r: (l, r, 0))],
        out_specs=pl.BlockSpec((1, tr, N_IN_PAD), lambda l, r: (l, r, 0)),
        out_shape=jax.ShapeDtypeStruct((depth, d, N_IN_PAD), BF16),
        compiler_params=_cparams(("parallel", "parallel")),
        name="regroup_w_in",
    )(w_in)


_IN_OUTS = (
    ("fqa", 4 * LANES, BF16),
    ("fka", 4 * LANES, BF16),
    ("fvb", GROUP, BF16),
    ("dqv", 8 * LANES, BF16),
    ("dkb", GROUP, BF16), ("dvb", GROUP, BF16),
    ("mqk", 2 * GROUP, F32), ("mvb", GROUP, BF16), ("mos", GROUP, F32),
    ("bqm", 4 * LANES, BF16),
    ("bkb", GROUP, BF16), ("bvb", GROUP, BF16),
    ("gates", LANES, F32),
    ("cum", LANES, F32),
)
_IN_STATE = ("fk", "fv", "dk", "dv", "bk", "bv")
N_IN_ARGS = 10


def _in_proj_kernel(*refs, bb, tt, running, ml_blocks, n_prev):
    (x_ref, sh_ref, sc_ref, g1_ref, w_ref, b_ref, gains_ref, s64_ref, s32_ref, tril_ref) = refs[:N_IN_ARGS]
    outs = refs[N_IN_ARGS + n_prev:]
    (fqa_ref, fka_ref, fvb_ref, dqv_ref, dkb_ref, dvb_ref, mqk_ref, mvb_ref, mos_ref, bqm_ref, bkb_ref,
     bvb_ref, gates_ref, cum_ref) = outs[:len(_IN_OUTS)]
    state = dict(zip(_IN_STATE, outs[len(_IN_OUTS):]))
    carry_ref = outs[len(_IN_OUTS) + len(_IN_STATE)]
    tm = bb * tt

    def put_state(name, val):
        ref = state[name]
        if not running:
            ref[...] = val
        elif name in ("bk", "bv"):
            @pl.when(pl.program_id(1) == pl.num_programs(1) - 1)
            def _():
                ref[0, 0] = val.T
        else:
            ref[0, 0] = val.T

    x = x_ref[...]
    ms = jnp.mean(x * x, axis=-1, keepdims=True)
    h = x * lax.rsqrt(ms + EPS) * g1_ref[...]
    h = h * (1.0 + sc_ref[...]) + sh_ref[...]
    hb = h.reshape(tm, D_MODEL).astype(BF16)

    def proj(g, width=GROUP):
        c0 = GROUP * g
        return _dot(hb, w_ref[:, c0:c0 + width]) + b_ref[:, c0:c0 + width]

    def rms_seg(z, s_ref, row, n):
        ss = _dot((z * z).astype(BF16), s_ref[...])
        return z * lax.rsqrt(ss * (1.0 / n) + EPS) * gains_ref[row:row + 1, :]

    lane = lax.broadcasted_iota(jnp.int32, (tm, LANES), 1)

    zg = proj(13, LANES)
    is_ls = (lane < N_HEADS) | ((lane >= 2 * N_HEADS) & (lane < N_GATES))
    gates = jnp.where(is_ls, _log_sigmoid(zg), zg)
    gates_ref[...] = gates
    ghi, gmid, glo = _split3(gates)
    tril = tril_ref[...]
    lane_b = lax.broadcasted_iota(jnp.int32, (LANES, LANES), 1)
    if running:
        @pl.when(pl.program_id(1) == 0)
        def _():
            carry_ref[...] = jnp.zeros_like(carry_ref)
        carry_a = carry_ref[0:1, :]
    carry_b = None
    cums = []
    for blk in range(tm // LANES):
        sl = slice(LANES * blk, LANES * blk + LANES)
        p = _dot(tril, ghi[sl]) + _dot(tril, gmid[sl]) + _dot(tril, glo[sl])
        if running:
            ca = p + carry_a
            carry_a = ca[LANES - 1:LANES, :]
            cb = p if blk % ml_blocks == 0 else p + carry_b
            carry_b = cb[LANES - 1:LANES, :]
            p = jnp.where(lane_b < N_HEADS, ca, cb)
        cums.append(p)
    if running:
        carry_ref[0:1, :] = carry_a
    cum = jnp.concatenate(cums, axis=0)
    cum_ref[...] = cum

    fqn = rms_seg(proj(0), s64_ref, 0, HEAD_DIM) * (HEAD_DIM ** -0.5)
    fkn = rms_seg(proj(1), s64_ref, 1, HEAD_DIM)
    put_state("fk", fkn)
    fv = proj(2)
    put_state("fv", fv)
    fvb_ref[...] = fv.astype(BF16)
    ones_q = jnp.where((lane >= HEAD_DIM + 3) & (lane < HEAD_DIM + 6), 1.0, 0.0)
    ones_k = jnp.where((lane >= HEAD_DIM) & (lane < HEAD_DIM + 3), 1.0, 0.0)
    for hd in range(N_HEADS):
        pr, e = divmod(hd, 2)
        bq = fqn[:, LANES * pr:LANES * pr + LANES]
        bk = fkn[:, LANES * pr:LANES * pr + LANES]
        if e:
            bq = pltpu.roll(bq, HEAD_DIM, axis=1)
            bk = pltpu.roll(bk, HEAD_DIM, axis=1)
        cbc = jnp.broadcast_to(cum[:, hd:hd + 1], (tm, LANES))
        chi = cbc.astype(BF16).astype(F32)
        r = cbc - chi
        cmid = r.astype(BF16).astype(F32)
        clo = (r - cmid).astype(BF16).astype(F32)
        aq = jnp.where(lane == HEAD_DIM, chi,
                       jnp.where(lane == HEAD_DIM + 1, cmid, jnp.where(lane == HEAD_DIM + 2, clo, ones_q)))
        ak = jnp.where(lane == HEAD_DIM + 3, -chi,
                       jnp.where(lane == HEAD_DIM + 4, -cmid, jnp.where(lane == HEAD_DIM + 5, -clo, ones_k)))
        fqa_ref[:, LANES * hd:LANES * hd + LANES] = jnp.where(lane < HEAD_DIM, bq, aq).astype(BF16)
        fka_ref[:, LANES * hd:LANES * hd + LANES] = jnp.where(lane < HEAD_DIM, bk, ak).astype(BF16)

    dqn = rms_seg(proj(3), s32_ref, 2, DIFF_HALF) * (DIFF_HALF ** -0.5)
    dkn = rms_seg(proj(4), s32_ref, 3, DIFF_HALF)
    put_state("dk", dkn)
    dkb_ref[...] = dkn.astype(BF16)
    dv = proj(5)
    put_state("dv", dv)
    dvb_ref[...] = dv.astype(BF16)
    for pr in range(2):
        blk = dqn[:, LANES * pr:LANES * pr + LANES]
        for e in range(2):
            for jm in range(2):
                l0 = HEAD_DIM * e + DIFF_HALF * jm
                idx = 4 * pr + 2 * e + jm
                dqv_ref[:, LANES * idx:LANES * idx + LANES] = jnp.where(
                    (lane >= l0) & (lane < l0 + DIFF_HALF), blk, 0.0).astype(BF16)

    mqk_ref[:, 0:GROUP] = proj(6)
    mqk_ref[:, GROUP:2 * GROUP] = proj(7)
    mvb_ref[...] = proj(8).astype(BF16)
    mos_ref[...] = jax.nn.sigmoid(proj(9))

    bqn = rms_seg(proj(10), s64_ref, 4, HEAD_DIM) * (HEAD_DIM ** -0.5)
    bkn = rms_seg(proj(11), s64_ref, 5, HEAD_DIM)
    put_state("bk", bkn)
    bkb_ref[...] = bkn.astype(BF16)
    bv = proj(12)
    put_state("bv", bv)
    bvb_ref[...] = bv.astype(BF16)
    for hd in range(N_HEADS):
        pr, e = divmod(hd, 2)
        blk = bqn[:, LANES * pr:LANES * pr + LANES]
        bqm_ref[:, LANES * hd:LANES * hd + LANES] = jnp.where(
            (lane >= HEAD_DIM * e) & (lane < HEAD_DIM * e + HEAD_DIM), blk, 0.0).astype(BF16)


def _in_proj(x, sh, sc, g1, w, b, gains, s64, s32, tril, *, bb, tt, running, ml_blocks,
             layer=0, depth=1, prev_state=None):
    bx, tx, d = x.shape
    n = bx * tx
    nb, nt = bx // bb, tx // tt
    tm = bb * tt
    const = lambda i, t: (0, 0)
    tok = lambda i, t: (i * nt + t, 0)
    keep = min(BAND_ROWS, tx)
    if running:
        assert bb == 1 and keep == tm
        st_shapes = [(depth, bx, GROUP, tx)] * 4 + [(depth, bx, GROUP, keep)] * 2
        st_specs = ([pl.BlockSpec((1, 1, GROUP, tm), lambda i, t: (layer, i, 0, t))] * 4
                    + [pl.BlockSpec((1, 1, GROUP, keep), lambda i, t: (layer, i, 0, 0))] * 2)
    else:
        st_shapes = [(n, GROUP)] * 6
        st_specs = [pl.BlockSpec((tm, GROUP), tok)] * 6
    prev = list(prev_state) if prev_state is not None else []
    aliases = {N_IN_ARGS + k: len(_IN_OUTS) + k for k in range(len(prev))}
    in_specs = [
        pl.BlockSpec((bb, tt, d), lambda i, t: (i, t, 0)),
        pl.BlockSpec((bb, 1, d), lambda i, t: (i, 0, 0)),
        pl.BlockSpec((bb, 1, d), lambda i, t: (i, 0, 0)),
        pl.BlockSpec((1, d), const),
        pl.BlockSpec((d, N_IN_PAD), const),
        pl.BlockSpec((1, N_IN_PAD), const),
        pl.BlockSpec((8, GROUP), const),
        pl.BlockSpec((GROUP, GROUP), const),
        pl.BlockSpec((GROUP, GROUP), const),
        pl.BlockSpec((LANES, LANES), const),
    ]
    in_specs += [pl.BlockSpec(memory_space=pl.ANY)] * len(prev)
    out_specs = [pl.BlockSpec((tm, wd), tok) for _, wd, _ in _IN_OUTS] + st_specs
    out_shape = ([jax.ShapeDtypeStruct((n, wd), dt) for _, wd, dt in _IN_OUTS]
                 + [jax.ShapeDtypeStruct(s, F32) for s in st_shapes])
    outs = pl.pallas_call(
        functools.partial(_in_proj_kernel, bb=bb, tt=tt, running=running, ml_blocks=ml_blocks,
                          n_prev=len(prev)),
        grid=(nb, nt),
        in_specs=in_specs,
        out_specs=out_specs,
        out_shape=out_shape,
        scratch_shapes=[pltpu.VMEM((8, LANES), F32)],
        input_output_aliases=aliases,
        compiler_params=_cparams(("arbitrary", "arbitrary")),
        name="in_proj",
    )(x, sh, sc, g1, w, b, gains, s64, s32, tril, *prev)
    ops = {name: o for (name, _, _), o in zip(_IN_OUTS, outs)}
    return ops, list(outs[len(_IN_OUTS):])


def _softmax_step(s, v, m_ref, l_ref, acc_ref, c):
    nt = s.shape[1] // LANES
    tiles = [s[:, LANES * t:LANES * t + LANES] for t in range(nt)]
    m_prev = m_ref[c]
    m_new = jnp.maximum(m_prev, jnp.max(s, axis=-1, keepdims=True))
    alpha = jnp.exp(m_prev - m_new)
    ps = [jnp.exp(t - m_new) for t in tiles]
    psum = ps[0]
    for p in ps[1:]:
        psum = psum + p
    l_ref[c] = alpha * l_ref[c] + psum
    p = jnp.concatenate([p.astype(BF16) for p in ps], axis=1)
    acc_ref[c] = alpha * acc_ref[c] + _dot(p, v)
    m_ref[c] = m_new


def _pipelined(n, make_s, consume):
    s_next = make_s(0)
    for c in range(n):
        s_cur = s_next
        if c + 1 < n:
            s_next = make_s(c + 1)
        consume(c, s_cur)


def _row_sum(l):
    return jnp.sum(l, axis=-1, keepdims=True)


def _fox_kernel(q_ref, k_ref, v_ref, o_ref, m_ref, l_ref, acc_ref, *, tq):
    i = pl.program_id(1)
    j = pl.program_id(2)

    @pl.when(j == 0)
    def _():
        m_ref[...] = jnp.full_like(m_ref, NEG)
        l_ref[...] = jnp.zeros_like(l_ref)
        acc_ref[...] = jnp.zeros_like(acc_ref)

    def step(diag):
        q = q_ref[0]
        k = k_ref[0]
        v = v_ref[0]
        if diag:
            row = lax.broadcasted_iota(jnp.int32, (tq, tq), 0)
            col = lax.broadcasted_iota(jnp.int32, (tq, tq), 1)
            keep = col <= row

        def make_s(hd):
            s = _nt_dot(q[:, LANES * hd:LANES * hd + LANES], k[:, LANES * hd:LANES * hd + LANES])
            return jnp.where(keep, s, NEG) if diag else s

        def consume(hd, s):
            pr = hd // 2
            _softmax_step(s, v[:, LANES * pr:LANES * pr + LANES], m_ref, l_ref, acc_ref, hd)

        _pipelined(N_HEADS, make_s, consume)

    @pl.when(j < i)
    def _():
        step(False)

    @pl.when(j == i)
    def _():
        step(True)
        lane = lax.broadcasted_iota(jnp.int32, (tq, LANES), 1)
        for pr in range(2):
            oe = acc_ref[2 * pr] / _row_sum(l_ref[2 * pr])
            oo = acc_ref[2 * pr + 1] / _row_sum(l_ref[2 * pr + 1])
            o_ref[0, :, LANES * pr:LANES * pr + LANES] = jnp.where(lane < HEAD_DIM, oe, oo).astype(BF16)


def _fox_prompt(qa, ka, vb, *, tq):
    b, t, _ = qa.shape
    nq = t // tq
    return pl.pallas_call(
        functools.partial(_fox_kernel, tq=tq),
        grid=(b, nq, nq),
        in_specs=[pl.BlockSpec((1, tq, 4 * LANES), lambda bi, i, j: (bi, i, 0)),
                  pl.BlockSpec((1, tq, 4 * LANES), lambda bi, i, j: (bi, jnp.minimum(i, j), 0)),
                  pl.BlockSpec((1, tq, GROUP), lambda bi, i, j: (bi, jnp.minimum(i, j), 0))],
        out_specs=pl.BlockSpec((1, tq, GROUP), lambda bi, i, j: (bi, i, 0)),
        out_shape=jax.ShapeDtypeStruct((b, t, GROUP), BF16),
        scratch_shapes=[pltpu.VMEM((N_HEADS, tq, LANES), F32), pltpu.VMEM((N_HEADS, tq, LANES), F32),
                        pltpu.VMEM((N_HEADS, tq, LANES), F32)],
        compiler_params=_cparams(("parallel", "parallel", "arbitrary")),
        name="fox_prompt",
    )(qa, ka, vb)


def _diff_lambda(lp, lam_init):
    a = jnp.sum(lp[0:1, :] * lp[1:2, :], axis=-1, keepdims=True)
    b = jnp.sum(lp[2:3, :] * lp[3:4, :], axis=-1, keepdims=True)
    return jnp.exp(a) - jnp.exp(b) + lam_init


def _diff_finish(acc_ref, l_ref, lam, gsub, lam_init, rows, o_ref):
    lane = lax.broadcasted_iota(jnp.int32, (rows, LANES), 1)
    for pr in range(2):
        outs = []
        for e in range(2):
            hd = 2 * pr + e
            o = (acc_ref[2 * hd] / _row_sum(l_ref[2 * hd])
                 - lam * (acc_ref[2 * hd + 1] / _row_sum(l_ref[2 * hd + 1])))
            valid = (lane >= HEAD_DIM * e) & (lane < HEAD_DIM * e + HEAD_DIM)
            ms = jnp.sum(jnp.where(valid, o * o, 0.0), axis=-1, keepdims=True) * (1.0 / HEAD_DIM)
            outs.append(o * lax.rsqrt(ms + EPS) * gsub * (1.0 - lam_init))
        o_ref[0, :, LANES * pr:LANES * pr + LANES] = jnp.where(lane < HEAD_DIM, outs[0], outs[1]).astype(BF16)


def _alibi_slope(hd):
    return 2.0 ** (-8.0 * (hd + 1) / N_HEADS)


def _diff_kernel(q_ref, k_ref, v_ref, lam_ref, gsub_ref, o_ref, m_ref, l_ref, acc_ref, *, tq, lam_init):
    i = pl.program_id(1)
    j = pl.program_id(2)

    @pl.when(j == 0)
    def _():
        m_ref[...] = jnp.full_like(m_ref, NEG)
        l_ref[...] = jnp.zeros_like(l_ref)
        acc_ref[...] = jnp.zeros_like(acc_ref)

    def step(diag):
        q = q_ref[0]
        k = k_ref[0]
        v = v_ref[0]
        row = lax.broadcasted_iota(jnp.int32, (tq, tq), 0)
        col = lax.broadcasted_iota(jnp.int32, (tq, tq), 1)
        dist = jnp.abs((i - j) * tq + row - col).astype(F32)
        if diag:
            seen = (col // CHUNK) <= (row // CHUNK)

        def make_s(c):
            hd, jm = divmod(c, 2)
            pr, e = divmod(hd, 2)
            idx = 4 * pr + 2 * e + jm
            s = _nt_dot(q[:, LANES * idx:LANES * idx + LANES], k[:, LANES * pr:LANES * pr + LANES])
            s = s - _alibi_slope(hd) * dist
            return jnp.where(seen, s, NEG) if diag else s

        def consume(c, s):
            pr = c // 4
            _softmax_step(s, v[:, LANES * pr:LANES * pr + LANES], m_ref, l_ref, acc_ref, c)

        _pipelined(2 * N_HEADS, make_s, consume)

    @pl.when(j < i)
    def _():
        step(False)

    @pl.when(j == i)
    def _():
        step(True)
        lam = _diff_lambda(lam_ref[...], lam_init)
        _diff_finish(acc_ref, l_ref, lam, gsub_ref[...], lam_init, tq, o_ref)


def _diff_prompt(qv, kb, vb, lam_p, gsub, *, tq, lam_init):
    b, t, _ = qv.shape
    nq = t // tq
    return pl.pallas_call(
        functools.partial(_diff_kernel, tq=tq, lam_init=lam_init),
        grid=(b, nq, nq),
        in_specs=[pl.BlockSpec((1, tq, 8 * LANES), lambda bi, i, j: (bi, i, 0)),
                  pl.BlockSpec((1, tq, GROUP), lambda bi, i, j: (bi, jnp.minimum(i, j), 0)),
                  pl.BlockSpec((1, tq, GROUP), lambda bi, i, j: (bi, jnp.minimum(i, j), 0)),
                  pl.BlockSpec((4, DIFF_HALF), lambda bi, i, j: (0, 0)),
                  pl.BlockSpec((1, LANES), lambda bi, i, j: (0, 0))],
        out_specs=pl.BlockSpec((1, tq, GROUP), lambda bi, i, j: (bi, i, 0)),
        out_shape=jax.ShapeDtypeStruct((b, t, GROUP), BF16),
        scratch_shapes=[pltpu.VMEM((2 * N_HEADS, tq, LANES), F32), pltpu.VMEM((2 * N_HEADS, tq, LANES), F32),
                        pltpu.VMEM((2 * N_HEADS, tq, LANES), F32)],
        compiler_params=_cparams(("parallel", "parallel", "arbitrary")),
        name="diff_prompt",
    )(qv, kb, vb, lam_p, gsub)


def _band_bias_kernel(g_ref, bp_ref, bs_ref):
    lane = lax.broadcasted_iota(jnp.int32, (CHUNK, BAND_WIN), 1)
    width = g_ref.shape[1]
    for hd in range(N_HEADS):
        x = jnp.broadcast_to(g_ref[hd:hd + 1, :], (CHUNK, width))
        rp = pltpu.roll(x, width - CHUNK + 1, axis=1, stride=1, stride_axis=0)
        bp_ref[hd] = jnp.where(lane < CHUNK, NEG, rp[:, :BAND_WIN])
        rs = pltpu.roll(x, width - 2 * CHUNK + 1, axis=1, stride=1, stride_axis=0)
        bs_ref[hd] = rs[:, :BAND_WIN]


def _band_bias(table):
    gv = jnp.concatenate([jnp.broadcast_to(table[:, 2 * REL_CLIP:], (N_HEADS, 4 * LANES)),
                          table[:, 2 * REL_CLIP - 1::-1]], axis=1)
    shp = jax.ShapeDtypeStruct((N_HEADS, CHUNK, BAND_WIN), F32)
    return pl.pallas_call(_band_bias_kernel, out_shape=(shp, shp), name="band_bias")(gv)


def _band_kernel(q_ref, k_ref, v_ref, bias_ref, o_ref, kpad_ref, vpad_ref, *, tq):
    t = pl.program_id(1)
    n_sub = tq // CHUNK
    lane = lax.broadcasted_iota(jnp.int32, (CHUNK, LANES), 1)
    col = lax.broadcasted_iota(jnp.int32, (CHUNK, BAND_WIN), 1)

    @pl.when(t == 0)
    def _():
        for src, dst in ((k_ref, kpad_ref), (v_ref, vpad_ref)):
            dst[0:BAND_WIN, :] = jnp.zeros((BAND_WIN, GROUP), BF16)
            dst[BAND_WIN:, :] = src[0]

    def window(ref, c, pr):
        start = pl.multiple_of((t * n_sub + c + 1) * CHUNK, CHUNK)
        return ref[pl.ds(start, BAND_WIN), LANES * pr:LANES * pr + LANES]

    def make_s(idx):
        c, hd = divmod(idx, N_HEADS)
        q = q_ref[0, CHUNK * c:CHUNK * c + CHUNK, LANES * hd:LANES * hd + LANES]
        valid = col + (t * n_sub + c - BAND_CHUNKS - 1) * CHUNK >= 0
        return jnp.where(valid, _nt_dot(q, window(kpad_ref, c, hd // 2)) + bias_ref[hd], NEG)

    even = {}

    def consume(idx, s):
        c, hd = divmod(idx, N_HEADS)
        pr, e = divmod(hd, 2)
        m = jnp.max(s, axis=-1, keepdims=True)
        p = jnp.exp(s - m)
        l = jnp.sum(p, axis=-1, keepdims=True)
        o = _dot(p.astype(BF16), window(vpad_ref, c, pr)) / l
        if e == 0:
            even[pr] = o
        else:
            o_ref[0, CHUNK * c:CHUNK * c + CHUNK, LANES * pr:LANES * pr + LANES] = jnp.where(
                lane < HEAD_DIM, even[pr], o).astype(BF16)

    _pipelined(n_sub * N_HEADS, make_s, consume)


def _band_prompt(qm, kb, vb, bias, *, tq):
    b, t, _ = qm.shape
    return pl.pallas_call(
        functools.partial(_band_kernel, tq=tq),
        grid=(b, t // tq),
        in_specs=[pl.BlockSpec((1, tq, 4 * LANES), lambda bi, i: (bi, i, 0)),
                  pl.BlockSpec((1, t, GROUP), lambda bi, i: (bi, 0, 0)),
                  pl.BlockSpec((1, t, GROUP), lambda bi, i: (bi, 0, 0)),
                  pl.BlockSpec((N_HEADS, CHUNK, BAND_WIN), lambda bi, i: (0, 0, 0))],
        out_specs=pl.BlockSpec((1, tq, GROUP), lambda bi, i: (bi, i, 0)),
        out_shape=jax.ShapeDtypeStruct((b, t, GROUP), BF16),
        scratch_shapes=[pltpu.VMEM((BAND_WIN + t, GROUP), BF16), pltpu.VMEM((BAND_WIN + t, GROUP), BF16)],
        compiler_params=_cparams(("arbitrary", "arbitrary")),
        name="band_prompt",
    )(qm, kb, vb, bias)


def _mlstm_kernel(mqk_ref, mv_ref, gates_ref, cum_ref, mos_ref, cw_ref, cb_ref, gmh_ref, shead_ref, bd_ref,
                  c0_ref, n0_ref, m0_ref, conv0_ref,
                  o_ref, c_out_ref, n_out_ref, m_out_ref,
                  cbuf_ref, c_ref, n_ref, m_ref, *, rows, valid):
    t = pl.program_id(1)

    @pl.when(t == 0)
    def _():
        c_ref[...] = c0_ref[0]
        n_ref[...] = n0_ref[0]
        m_ref[...] = m0_ref[0]
        cbuf_ref[0:8, :] = conv0_ref[0]

    def padded(a, fill=0.0):
        if valid == rows:
            return a
        return jnp.concatenate([a, jnp.full((rows - valid, a.shape[1]), fill, a.dtype)], axis=0)

    u = padded(mqk_ref[...])
    cbuf_ref[8:8 + rows, :] = u
    y = cb_ref[...] + cw_ref[3:4, :] * u
    for jw in range(CONV_W - 1):
        y = y + cw_ref[jw:jw + 1, :] * cbuf_ref[5 + jw:5 + jw + rows, :]
    cbuf_ref[0:8, :] = cbuf_ref[valid:valid + 8, :]
    qk = y * jax.nn.sigmoid(y)
    q = qk[:, 0:GROUP]
    k = qk[:, GROUP:2 * GROUP] * (HEAD_DIM ** -0.5)
    qb = q.astype(BF16)
    kb = k.astype(BF16)
    v = padded(mv_ref[...])
    mos = padded(mos_ref[...])

    g = gates_ref[...]
    cm = cum_ref[...]
    if valid != rows:
        g = padded(g, NEG)
        cm = jnp.concatenate([cm, jnp.broadcast_to(cm[valid - 1:valid, :], (rows - valid, LANES))], axis=0)
    g_t = g.T
    cm_t = cm.T

    row = lax.broadcasted_iota(jnp.int32, (rows, rows), 0)
    col = lax.broadcasted_iota(jnp.int32, (rows, rows), 1)
    causal = col <= row
    lane_g = lax.broadcasted_iota(jnp.int32, (rows, GROUP), 1)
    lane_r = lax.broadcasted_iota(jnp.int32, (1, GROUP), 1)
    m_prev_all = m_ref[...]

    zeros = jnp.zeros((rows, GROUP), F32)
    num, a_full, ws_full, mt_full, wend_full = zeros, zeros, zeros, zeros, zeros
    decay_lane = jnp.zeros((1, GROUP), F32)
    mnew_lane = jnp.zeros((1, GROUP), F32)
    head_masks = [(lane_g >= HEAD_DIM * hd) & (lane_g < HEAD_DIM * hd + HEAD_DIM) for hd in range(N_HEADS)]
    sqks = [_nt_dot(jnp.where(hm, q, 0.0).astype(BF16), kb) for hm in head_masks]
    c_old = c_ref[...]
    n_old = n_ref[...]
    shead = shead_ref[...]
    q_c = _dot(qb, c_old.astype(BF16))
    q_n = _dot((q * n_old).astype(BF16), shead)
    for hd in range(N_HEADS):
        hm = head_masks[hd]
        hm_r = (lane_r >= HEAD_DIM * hd) & (lane_r < HEAD_DIM * hd + HEAD_DIM)
        f_c = cm[:, 2 * N_HEADS + hd:2 * N_HEADS + hd + 1]
        ig_c = g[:, N_HEADS + hd:N_HEADS + hd + 1]
        f_r = cm_t[2 * N_HEADS + hd:2 * N_HEADS + hd + 1, :]
        ig_r = g_t[N_HEADS + hd:N_HEADS + hd + 1, :]
        m_prev = m_prev_all[:, HEAD_DIM * hd:HEAD_DIM * hd + 1]
        logw = jnp.where(causal, f_c - (f_r - ig_r), NEG)
        logb = f_c + m_prev
        m_t = jnp.maximum(logb, jnp.max(logw, axis=-1, keepdims=True))
        w = jnp.exp(logw - m_t) * sqks[hd]
        a = jnp.exp(logb - m_t)
        num = jnp.where(hm, _dot(w.astype(BF16), v), num)
        a_full = jnp.where(hm, a, a_full)
        ws_full = jnp.where(hm, jnp.sum(w, axis=-1, keepdims=True), ws_full)
        mt_full = jnp.where(hm, m_t, mt_full)
        f_end = f_c[rows - 1:rows, :]
        log_end = f_end - f_c + ig_c
        m_new = jnp.maximum(f_end + m_prev, jnp.max(log_end, axis=0, keepdims=True))
        wend_full = jnp.where(hm, jnp.exp(log_end - m_new), wend_full)
        decay_lane = jnp.where(hm_r, jnp.exp(f_end + m_prev - m_new), decay_lane)
        mnew_lane = jnp.where(hm_r, m_new, mnew_lane)

    num = num + a_full * q_c
    den = ws_full + a_full * q_n
    h = num / jnp.maximum(jnp.abs(den), jnp.exp(-mt_full))
    o = mos * h
    ss = _dot((o * o).astype(BF16), shead)
    on = o * lax.rsqrt(ss * (1.0 / HEAD_DIM) + EPS) * gmh_ref[...]
    o_ref[...] = on[0:valid, :].astype(BF16)

    wk = k * wend_full
    kv = lax.dot_general(wk.astype(BF16), v, (((0,), (0,)), ((), ())), preferred_element_type=F32)
    c_ref[...] = c_old * decay_lane + jnp.where(bd_ref[...] > 0.5, kv, 0.0)
    n_ref[...] = n_old * decay_lane + jnp.sum(wk, axis=0, keepdims=True)
    m_ref[...] = mnew_lane

    @pl.when(t == pl.num_programs(1) - 1)
    def _():
        c_out_ref[0] = c_ref[...]
        n_out_ref[0] = n_ref[...]
        m_out_ref[0] = m_ref[...]


def _mlstm(mqk, mvb, gates, cum, mos, cw, cb, gmh, shead, bd, c0, n0, m0, conv0, *, nb, t, valid, rows):
    nc = t // valid
    tok = lambda bi, i: (bi * nc + i, 0)
    const = lambda bi, i: (0, 0)
    per_b = lambda bi, i: (bi, 0, 0)
    return pl.pallas_call(
        functools.partial(_mlstm_kernel, rows=rows, valid=valid),
        grid=(nb, nc),
        in_specs=[pl.BlockSpec((valid, 2 * GROUP), tok), pl.BlockSpec((valid, GROUP), tok),
                  pl.BlockSpec((valid, LANES), tok), pl.BlockSpec((valid, LANES), tok),
                  pl.BlockSpec((valid, GROUP), tok),
                  pl.BlockSpec((CONV_W, 2 * GROUP), const), pl.BlockSpec((1, 2 * GROUP), const),
                  pl.BlockSpec((1, GROUP), const), pl.BlockSpec((GROUP, GROUP), const),
                  pl.BlockSpec((GROUP, GROUP), const),
                  pl.BlockSpec((1, GROUP, GROUP), per_b), pl.BlockSpec((1, 1, GROUP), per_b),
                  pl.BlockSpec((1, 1, GROUP), per_b), pl.BlockSpec((1, 8, 2 * GROUP), per_b)],
        out_specs=[pl.BlockSpec((valid, GROUP), tok), pl.BlockSpec((1, GROUP, GROUP), per_b),
                   pl.BlockSpec((1, 1, GROUP), per_b), pl.BlockSpec((1, 1, GROUP), per_b)],
        out_shape=[jax.ShapeDtypeStruct((nb * t, GROUP), BF16), jax.ShapeDtypeStruct((nb, GROUP, GROUP), F32),
                   jax.ShapeDtypeStruct((nb, 1, GROUP), F32), jax.ShapeDtypeStruct((nb, 1, GROUP), F32)],
        scratch_shapes=[pltpu.VMEM((8 + rows, 2 * GROUP), F32), pltpu.VMEM((GROUP, GROUP), F32),
                        pltpu.VMEM((1, GROUP), F32), pltpu.VMEM((1, GROUP), F32)],
        compiler_params=_cparams(("parallel", "arbitrary")),
        name="mlstm",
    )(mqk, mvb, gates, cum, mos, cw, cb, gmh, shead, bd, c0, n0, m0, conv0)


def _pad_rows(a, rows):
    return jnp.concatenate([a, jnp.zeros((rows - a.shape[0], a.shape[1]), a.dtype)], axis=0)


def _two_part_softmax(s_c, s_n, vt_c, v_n):
    m = jnp.maximum(jnp.max(s_c, axis=-1, keepdims=True), jnp.max(s_n, axis=-1, keepdims=True))
    p_c = jnp.exp(s_c - m)
    p_n = jnp.exp(s_n - m)
    l = jnp.sum(p_c, axis=-1, keepdims=True) + jnp.sum(p_n, axis=-1, keepdims=True)
    return _nt_dot(p_c.astype(BF16), vt_c) + _dot(p_n.astype(BF16), v_n), l


def _fox_sample_kernel(q_ref, kn_ref, vn_ref, ck_ref, cv_ref, clf_ref, lstr_ref, o_ref, *, tn, past):
    q = q_ref[0]
    kn = _pad_rows(kn_ref[0], LANES)
    vn = _pad_rows(vn_ref[0], LANES)
    ck = ck_ref[0]
    cv = cv_ref[0].astype(BF16)
    x = _pad_rows(clf_ref[0], 8)
    xhi, xmid, xlo = _split3(x)
    lstr = lstr_ref[...]
    suf = _dot(xhi, lstr) + _dot(xmid, lstr) + _dot(xlo, lstr)

    sub = lax.broadcasted_iota(jnp.int32, (8, past), 0)
    lane_o = lax.broadcasted_iota(jnp.int32, (tn, LANES), 1)
    row = lax.broadcasted_iota(jnp.int32, (tn, LANES), 0)
    causal = lane_o <= row
    fill = jnp.zeros((LANES - HEAD_DIM - 8, past), F32)
    for pr in range(2):
        outs = []
        for e in range(2):
            hd = 2 * pr + e
            sb = jnp.broadcast_to(suf[hd:hd + 1, :], (8, past))
            shi = sb.astype(BF16).astype(F32)
            r = sb - shi
            smid = r.astype(BF16).astype(F32)
            slo = (r - smid).astype(BF16).astype(F32)
            aug = jnp.where(sub < 3, 1.0,
                            jnp.where(sub == 3, shi, jnp.where(sub == 4, smid, jnp.where(sub == 5, slo, 0.0))))
            kc = jnp.concatenate([ck[HEAD_DIM * hd:HEAD_DIM * hd + HEAD_DIM, :], aug, fill], axis=0).astype(BF16)
            qh = q[:, LANES * hd:LANES * hd + LANES]
            s_c = _dot(qh, kc)
            s_n = jnp.where(causal, _nt_dot(qh, kn[:, LANES * hd:LANES * hd + LANES]), NEG)
            acc, l = _two_part_softmax(s_c, s_n, cv[LANES * pr:LANES * pr + LANES, :],
                                       vn[:, LANES * pr:LANES * pr + LANES])
            outs.append(acc / l)
        o_ref[0, :, LANES * pr:LANES * pr + LANES] = jnp.where(lane_o < HEAD_DIM, outs[0], outs[1]).astype(BF16)


def _fox_sample(qa, ka, vb, ck, cv, clf, lstr):
    b, tn, _ = qa.shape
    past = ck.shape[2]
    per_b = lambda i: (i, 0, 0)
    return pl.pallas_call(
        functools.partial(_fox_sample_kernel, tn=tn, past=past),
        grid=(b,),
        in_specs=[pl.BlockSpec((1, tn, 4 * LANES), per_b), pl.BlockSpec((1, tn, 4 * LANES), per_b),
                  pl.BlockSpec((1, tn, GROUP), per_b), pl.BlockSpec((1, GROUP, past), per_b),
                  pl.BlockSpec((1, GROUP, past), per_b), pl.BlockSpec((1, N_HEADS, past), per_b),
                  pl.BlockSpec((past, past), lambda i: (0, 0))],
        out_specs=pl.BlockSpec((1, tn, GROUP), per_b),
        out_shape=jax.ShapeDtypeStruct((b, tn, GROUP), BF16),
        compiler_params=_cparams(("parallel",)),
        name="fox_sample",
    )(qa, ka, vb, ck, cv, clf, lstr)


def _diff_sample_kernel(q_ref, kn_ref, vn_ref, ck_ref, cv_ref, lam_ref, gsub_ref, o_ref, l_ref, acc_ref,
                        *, tn, past, lam_init):
    q = q_ref[0]
    kn = _pad_rows(kn_ref[0], LANES)
    vn = _pad_rows(vn_ref[0], LANES)
    ck = ck_ref[0].astype(BF16)
    cv = cv_ref[0].astype(BF16)
    row_c = lax.broadcasted_iota(jnp.int32, (tn, past), 0)
    col_c = lax.broadcasted_iota(jnp.int32, (tn, past), 1)
    dist_c = (past + row_c - col_c).astype(F32)
    row_n = lax.broadcasted_iota(jnp.int32, (tn, LANES), 0)
    col_n = lax.broadcasted_iota(jnp.int32, (tn, LANES), 1)
    dist_n = jnp.abs(row_n - col_n).astype(F32)
    real = col_n < tn
    for hd in range(N_HEADS):
        pr, e = divmod(hd, 2)
        slope = _alibi_slope(hd)
        for jm in range(2):
            idx = 4 * pr + 2 * e + jm
            qh = q[:, LANES * idx:LANES * idx + LANES]
            s_c = _dot(qh, ck[LANES * pr:LANES * pr + LANES, :]) - slope * dist_c
            s_n = jnp.where(real, _nt_dot(qh, kn[:, LANES * pr:LANES * pr + LANES]) - slope * dist_n, NEG)
            acc, l = _two_part_softmax(s_c, s_n, cv[LANES * pr:LANES * pr + LANES, :],
                                       vn[:, LANES * pr:LANES * pr + LANES])
            acc_ref[2 * hd + jm] = acc
            l_ref[2 * hd + jm] = l
    lam = _diff_lambda(lam_ref[...], lam_init)
    _diff_finish(acc_ref, l_ref, lam, gsub_ref[...], lam_init, tn, o_ref)


def _diff_sample(qv, kb, vb, ck, cv, lam_p, gsub, *, lam_init):
    b, tn, _ = qv.shape
    past = ck.shape[2]
    per_b = lambda i: (i, 0, 0)
    return pl.pallas_call(
        functools.partial(_diff_sample_kernel, tn=tn, past=past, lam_init=lam_init),
        grid=(b,),
        in_specs=[pl.BlockSpec((1, tn, 8 * LANES), per_b), pl.BlockSpec((1, tn, GROUP), per_b),
                  pl.BlockSpec((1, tn, GROUP), per_b), pl.BlockSpec((1, GROUP, past), per_b),
                  pl.BlockSpec((1, GROUP, past), per_b), pl.BlockSpec((4, DIFF_HALF), lambda i: (0, 0)),
                  pl.BlockSpec((1, LANES), lambda i: (0, 0))],
        out_specs=pl.BlockSpec((1, tn, GROUP), per_b),
        out_shape=jax.ShapeDtypeStruct((b, tn, GROUP), BF16),
        scratch_shapes=[pltpu.VMEM((2 * N_HEADS, tn, 1), F32), pltpu.VMEM((2 * N_HEADS, tn, LANES), F32)],
        compiler_params=_cparams(("parallel",)),
        name="diff_sample",
    )(qv, kb, vb, ck, cv, lam_p, gsub)


def _band_sample_kernel(q_ref, kn_ref, vn_ref, ck_ref, cv_ref, bias_ref, o_ref, *, tn, past):
    q = q_ref[0]
    kn = _pad_rows(kn_ref[0], LANES)
    vn = _pad_rows(vn_ref[0], LANES)
    ck = ck_ref[0].astype(BF16)
    cv = cv_ref[0].astype(BF16)
    lane = lax.broadcasted_iota(jnp.int32, (tn, LANES), 1)
    real = lane < tn
    for pr in range(2):
        outs = []
        for e in range(2):
            hd = 2 * pr + e
            qh = q[:, LANES * hd:LANES * hd + LANES]
            s_c = _dot(qh, ck[LANES * pr:LANES * pr + LANES, :]) + bias_ref[hd, 0:tn, 0:past]
            s_n = _nt_dot(qh, kn[:, LANES * pr:LANES * pr + LANES]) + bias_ref[hd, 0:tn, past:past + LANES]
            s_n = jnp.where(real, s_n, NEG)
            acc, l = _two_part_softmax(s_c, s_n, cv[LANES * pr:LANES * pr + LANES, :],
                                       vn[:, LANES * pr:LANES * pr + LANES])
            outs.append(acc / l)
        o_ref[0, :, LANES * pr:LANES * pr + LANES] = jnp.where(lane < HEAD_DIM, outs[0], outs[1]).astype(BF16)


def _band_sample(qm, kb, vb, ck, cv, bias):
    b, tn, _ = qm.shape
    past = ck.shape[2]
    per_b = lambda i: (i, 0, 0)
    return pl.pallas_call(
        functools.partial(_band_sample_kernel, tn=tn, past=past),
        grid=(b,),
        in_specs=[pl.BlockSpec((1, tn, 4 * LANES), per_b), pl.BlockSpec((1, tn, GROUP), per_b),
                  pl.BlockSpec((1, tn, GROUP), per_b), pl.BlockSpec((1, GROUP, past), per_b),
                  pl.BlockSpec((1, GROUP, past), per_b),
                  pl.BlockSpec((N_HEADS, CHUNK, BAND_WIN), lambda i: (0, 0, 0))],
        out_specs=pl.BlockSpec((1, tn, GROUP), per_b),
        out_shape=jax.ShapeDtypeStruct((b, tn, GROUP), BF16),
        compiler_params=_cparams(("parallel",)),
        name="band_sample",
    )(qm, kb, vb, ck, cv, bias)


def _out_proj_kernel(x_ref, gt_ref, a_ref, b_ref, c_ref, d_ref, w_ref, o_ref, *, bb, tt):
    acc = _dot(a_ref[...], w_ref[0:GROUP, :])
    acc += _dot(b_ref[...], w_ref[GROUP:2 * GROUP, :])
    acc += _dot(c_ref[...], w_ref[2 * GROUP:3 * GROUP, :])
    acc += _dot(d_ref[...], w_ref[3 * GROUP:4 * GROUP, :])
    o_ref[...] = x_ref[...] + gt_ref[...] * acc.reshape(bb, tt, D_MODEL)


def _out_proj(x, gt, o_fox, o_diff, o_ml, o_band, w_out, *, bb, tt):
    bx, tx, d = x.shape
    nb, nt = bx // bb, tx // tt
    tm = bb * tt
    tok = lambda i, t: (i * nt + t, 0)
    return pl.pallas_call(
        functools.partial(_out_proj_kernel, bb=bb, tt=tt),
        grid=(nb, nt),
        in_specs=[pl.BlockSpec((bb, tt, d), lambda i, t: (i, t, 0)),
                  pl.BlockSpec((bb, 1, d), lambda i, t: (i, 0, 0)),
                  pl.BlockSpec((tm, GROUP), tok), pl.BlockSpec((tm, GROUP), tok),
                  pl.BlockSpec((tm, GROUP), tok), pl.BlockSpec((tm, GROUP), tok),
                  pl.BlockSpec((d, d), lambda i, t: (0, 0))],
        out_specs=pl.BlockSpec((bb, tt, d), lambda i, t: (i, t, 0)),
        out_shape=jax.ShapeDtypeStruct(x.shape, F32),
        compiler_params=_cparams(("parallel", "parallel")),
        name="out_proj",
    )(x, gt, o_fox, o_diff, o_ml, o_band, w_out)


def _ffn_kernel(x_ref, sh_ref, sc_ref, gt_ref, g2_ref, wg_ref, wu_ref, wd_ref, o_ref, hb_ref, acc_ref,
                *, bb, tt):
    f = pl.program_id(2)
    tm = bb * tt

    @pl.when(f == 0)
    def _():
        x = x_ref[...]
        ms = jnp.mean(x * x, axis=-1, keepdims=True)
        h = x * lax.rsqrt(ms + EPS) * g2_ref[...]
        h = h * (1.0 + sc_ref[...]) + sh_ref[...]
        hb_ref[...] = h.reshape(tm, D_MODEL).astype(BF16)
        acc_ref[...] = jnp.zeros_like(acc_ref)

    hb = hb_ref[...]
    g = _dot(hb, wg_ref[...])
    u = _dot(hb, wu_ref[...])
    a = (g * jax.nn.sigmoid(g) * u).astype(BF16)
    acc_ref[...] += _dot(a, wd_ref[...])

    @pl.when(f == pl.num_programs(2) - 1)
    def _():
        o_ref[...] = x_ref[...] + gt_ref[...] * acc_ref[...].reshape(bb, tt, D_MODEL)


def _ffn(x, sh, sc, gt, g2, wg, wu, wd, *, bb, tt, tf):
    bx, tx, d = x.shape
    nb, nt = bx // bb, tx // tt
    nf = D_FF // tf
    tm = bb * tt
    xs = pl.BlockSpec((bb, tt, d), lambda i, t, f: (i, t, 0))
    ms = pl.BlockSpec((bb, 1, d), lambda i, t, f: (i, 0, 0))
    return pl.pallas_call(
        functools.partial(_ffn_kernel, bb=bb, tt=tt),
        grid=(nb, nt, nf),
        in_specs=[xs, ms, ms, ms, pl.BlockSpec((1, d), lambda i, t, f: (0, 0)),
                  pl.BlockSpec((d, tf), lambda i, t, f: (0, f)),
                  pl.BlockSpec((d, tf), lambda i, t, f: (0, f)),
                  pl.BlockSpec((tf, d), lambda i, t, f: (f, 0))],
        out_specs=xs,
        out_shape=jax.ShapeDtypeStruct(x.shape, F32),
        scratch_shapes=[pltpu.VMEM((tm, d), BF16), pltpu.VMEM((tm, d), F32)],
        compiler_params=_cparams(("parallel", "parallel", "arbitrary")),
        name="ffn",
    )(x, sh, sc, gt, g2, wg, wu, wd)


def _consts(seg):
    r = np.arange(GROUP)
    s64 = (r[:, None] // HEAD_DIM == r[None, :] // HEAD_DIM)
    s32 = (r[:, None] // DIFF_HALF == r[None, :] // DIFF_HALF)
    q = np.arange(LANES)
    tril = (q[None, :] <= q[:, None]) & (q[:, None] // seg == q[None, :] // seg)
    as_bf16 = lambda m: jnp.asarray(m.astype(np.float32), dtype=BF16)
    return as_bf16(s64), as_bf16(s32), as_bf16(tril), jnp.asarray(s64.astype(np.float32))


def _prep_layer(l, norm1_g, norm2_g, w_in_g, b_in, qk_g_fox, qk_g_diff, qk_g_band, conv_w, conv_b,
                diff_lambda, diff_subln_g, mlstm_norm_g, band_rel_bias, w_out, w_ffn_gate, w_ffn_up,
                w_ffn_down):
    starts = np.concatenate([[0], np.cumsum(IN_SPLIT_SIZES)]).tolist()
    order = list(_FULL_GROUPS) + list(_GATE_GROUPS)
    bl = b_in[l]
    b_cols = [bl[starts[g]:starts[g + 1]] for g in order]
    pad = N_IN_PAD - N_FULL - N_GATES
    b_cols.append(jnp.zeros((pad,), F32))
    gains = jnp.stack([
        jnp.tile(qk_g_fox[l, 0], N_HEADS), jnp.tile(qk_g_fox[l, 1], N_HEADS),
        jnp.tile(qk_g_diff[l, 0], 2 * N_HEADS), jnp.tile(qk_g_diff[l, 1], 2 * N_HEADS),
        jnp.tile(qk_g_band[l, 0], N_HEADS), jnp.tile(qk_g_band[l, 1], N_HEADS),
        jnp.zeros((GROUP,), F32), jnp.zeros((GROUP,), F32)])
    return dict(
        g1=norm1_g[l].reshape(1, D_MODEL), g2=norm2_g[l].reshape(1, D_MODEL),
        w_in=w_in_g[l],
        b_in=jnp.concatenate(b_cols).reshape(1, N_IN_PAD),
        gains=gains, conv_w=conv_w[l], conv_b=conv_b[l].reshape(1, 2 * GROUP),
        lam_p=diff_lambda[l], gsub=jnp.tile(diff_subln_g[l], 2).reshape(1, LANES),
        gmh=jnp.tile(mlstm_norm_g[l], N_HEADS).reshape(1, GROUP),
        table=band_rel_bias[l],
        w_out=w_out[l].astype(BF16), wg=w_ffn_gate[l].astype(BF16), wu=w_ffn_up[l].astype(BF16),
        wd=w_ffn_down[l].astype(BF16),
        lam_init=0.8 - 0.6 * math.exp(-0.3 * l))


def _layer(x, mod, lp, caches, *, bb, tt, layer, depth, prev_state=None):
    bx, tx, _ = x.shape
    n = bx * tx
    prompt = caches is None
    sh1, sc1, gt1, sh2, sc2, gt2 = [mod[:, i:i + 1, :] for i in range(6)]
    s64, s32, tril, bd = _consts(tx if prompt else min(tx, LANES))
    z, big = _in_proj(x, sh1, sc1, lp["g1"], lp["w_in"], lp["b_in"], lp["gains"], s64, s32, tril,
                      bb=bb, tt=tt, running=prompt, ml_blocks=ML_CHUNK // LANES,
                      layer=layer, depth=depth, prev_state=prev_state)
    r3 = lambda a: a.reshape(bx, tx, a.shape[-1])
    bias_p, bias_s = _band_bias(lp["table"])

    if prompt:
        tq = 512
        o_fox = _fox_prompt(r3(z["fqa"]), r3(z["fka"]), r3(z["fvb"]), tq=tq)
        o_diff = _diff_prompt(r3(z["dqv"]), r3(z["dkb"]), r3(z["dvb"]), lp["lam_p"], lp["gsub"],
                              tq=tq, lam_init=lp["lam_init"])
        o_band = _band_prompt(r3(z["bqm"]), r3(z["bkb"]), r3(z["bvb"]), bias_p, tq=tq)
        c0 = jnp.zeros((bx, GROUP, GROUP), F32)
        n0 = jnp.zeros((bx, 1, GROUP), F32)
        m0 = jnp.zeros((bx, 1, GROUP), F32)
        conv0 = jnp.zeros((bx, 8, 2 * GROUP), F32)
        ml_valid, ml_rows = ML_CHUNK, ML_CHUNK
    else:
        (c_fk, c_fv, c_flf, c_dk, c_dv, c_bk, c_bv, s_c, s_n, s_m, s_conv) = caches
        past = c_fk.shape[2]
        assert past % CHUNK == 0 and tx <= CHUNK
        jj = np.arange(past)
        lstr = jnp.asarray((jj[:, None] > jj[None, :]).astype(np.float32), dtype=BF16)
        o_fox = _fox_sample(r3(z["fqa"]), r3(z["fka"]), r3(z["fvb"]), c_fk, c_fv, c_flf, lstr)
        o_diff = _diff_sample(r3(z["dqv"]), r3(z["dkb"]), r3(z["dvb"]), c_dk, c_dv,
                              lp["lam_p"], lp["gsub"], lam_init=lp["lam_init"])
        o_band = _band_sample(r3(z["bqm"]), r3(z["bkb"]), r3(z["bvb"]), c_bk, c_bv, bias_s)
        eye = jnp.eye(N_HEADS, dtype=F32)
        c0 = (s_c[:, :, :, None, :] * eye[None, :, None, :, None]).reshape(bx, GROUP, GROUP)
        n0 = s_n.reshape(bx, 1, GROUP)
        m0 = jnp.repeat(s_m, HEAD_DIM, axis=-1).reshape(bx, 1, GROUP)
        conv0 = jnp.pad(s_conv, ((0, 0), (8 - (CONV_W - 1), 0), (0, 0)))
        ml_valid, ml_rows = tx, ML_PAD
    o_ml, c_new, n_new, m_new = _mlstm(
        z["mqk"], z["mvb"], z["gates"], z["cum"], z["mos"], lp["conv_w"], lp["conv_b"], lp["gmh"],
        s64, bd, c0, n0, m0, conv0, nb=bx, t=tx, valid=ml_valid, rows=ml_rows)

    flat2 = lambda a: a.reshape(n, GROUP)
    x1 = _out_proj(x, gt1, flat2(o_fox), flat2(o_diff), o_ml, flat2(o_band), lp["w_out"], bb=bb, tt=tt)
    x2 = _ffn(x1, sh2, sc2, gt2, lp["g2"], lp["wg"], lp["wu"], lp["wd"], bb=bb, tt=tt, tf=D_FF // 2)

    mc = jnp.stack([c_new[:, HEAD_DIM * h:HEAD_DIM * h + HEAD_DIM, HEAD_DIM * h:HEAD_DIM * h + HEAD_DIM]
                    for h in range(N_HEADS)], axis=1)
    small = (r3(z["gates"])[:, :, 0:N_HEADS], mc, n_new.reshape(bx, N_HEADS, HEAD_DIM),
             m_new.reshape(bx, N_HEADS, HEAD_DIM)[:, :, 0], r3(z["mqk"])[:, tx - (CONV_W - 1):, :])
    return x2, big, small


def _channel_major(c):
    perm = (0, 1) + tuple(range(3, c.ndim)) + (2,)
    return jnp.transpose(c, perm).reshape(c.shape[0], c.shape[1], -1, c.shape[2])


def _token_major(a, inner):
    d, b, _, t = a.shape
    k = len(inner)
    return jnp.transpose(a.reshape(d, b, *inner, t), (0, 1, 2 + k) + tuple(range(2, 2 + k)))


def kernel(x_prompt, x_sample, c_prompt, c_sample, cache_fox_k, cache_fox_v, cache_fox_logf, cache_diff_k, cache_diff_v, cache_band_k, cache_band_v, state_mlstm_c, state_mlstm_n, state_mlstm_m, state_conv, norm1_g, norm2_g, w_mod, b_mod, w_in, b_in, qk_g_fox, qk_g_diff, qk_g_band, conv_w, conv_b, diff_lambda, diff_subln_g, mlstm_norm_g, band_rel_bias, w_out, w_ffn_gate, w_ffn_up, w_ffn_down):
    depth = w_in.shape[0]
    bp, bs = x_prompt.shape[0], x_sample.shape[0]
    ts = x_sample.shape[1]
    caches = tuple(_channel_major(c) for c in (cache_fox_k, cache_fox_v, cache_fox_logf, cache_diff_k,
                                               cache_diff_v, cache_band_k, cache_band_v))
    caches += (state_mlstm_c, state_mlstm_n, state_mlstm_m, state_conv)
    mod = _modulation(jnp.concatenate([c_prompt, c_sample], axis=0), w_mod, b_mod)
    mod = mod.reshape(depth, bp + bs, 6, D_MODEL)
    w_in_g = _regroup_w_in(w_in)
    layers = [_prep_layer(l, norm1_g, norm2_g, w_in_g, b_in, qk_g_fox, qk_g_diff, qk_g_band, conv_w,
                          conv_b, diff_lambda, diff_subln_g, mlstm_norm_g, band_rel_bias, w_out,
                          w_ffn_gate, w_ffn_up, w_ffn_down) for l in range(depth)]
    sample_bb = 512 // ts
    xp, xs = x_prompt, x_sample
    p_big, p_small, s_big, s_small = None, [], [], []
    for l in range(depth):
        xp, p_big, sm = _layer(xp, mod[l, :bp], layers[l], None, bb=1, tt=512, layer=l, depth=depth,
                               prev_state=p_big)
        p_small.append(sm)
    for l in range(depth):
        xs, big, sm = _layer(xs, mod[l, bp:], layers[l], tuple(c[l] for c in caches), bb=sample_bb, tt=ts,
                             layer=l, depth=depth)
        s_big.append(big)
        s_small.append(sm)
    head = (N_HEADS, HEAD_DIM)
    half = (N_HEADS, 2, DIFF_HALF)
    inner = (head, head, half, head, head, head)
    p_fk, p_fv, p_dk, p_dv, p_bk, p_bv = [_token_major(a, inn) for a, inn in zip(p_big, inner)]
    p_flf, p_c, p_n, p_m, p_conv = [jnp.stack(zs) for zs in zip(*p_small)]
    s_fk, s_fv, s_dk, s_dv, s_bk, s_bv = [jnp.stack(zs).reshape((depth, bs, ts) + inn)
                                          for zs, inn in zip(zip(*s_big), inner)]
    s_flf, s_c, s_n, s_m, s_conv = [jnp.stack(zs) for zs in zip(*s_small)]
    return (xp, xs, p_fk, p_fv, p_flf, p_dk, p_dv, p_bk, p_bv, p_c, p_n, p_m, p_conv,
            s_fk, s_fv, s_flf, s_dk, s_dv, s_bk, s_bv, s_c, s_n, s_m, s_conv)
```

```python
import functools
import math

import numpy as np
import jax
import jax.numpy as jnp
from jax import lax
from jax.experimental import pallas as pl
from jax.experimental.pallas import tpu as pltpu

F32 = jnp.float32
BF16 = jnp.bfloat16

D_MODEL = 1024
HEAD_DIM = 64
N_HEADS = 4
GROUP = N_HEADS * HEAD_DIM
DIFF_HALF = HEAD_DIM // 2
CHUNK = 64
BAND_CHUNKS = 8
BAND_ROWS = BAND_CHUNKS * CHUNK
REL_CLIP = 128
CONV_W = 4
D_FF = 2816
EPS = 1e-6
NEG = -1e30
LOG2E = math.log2(math.e)

LANES = 128
PAIR = 2 * HEAD_DIM
N_GATES = 3 * N_HEADS
BAND_WIN = (BAND_CHUNKS + 2) * CHUNK
ML_CHUNK = 256
ML_PAD = 128
ML_SEQS = 4
VMEM_LIMIT = 56 * 1024 * 1024

IN_SPLIT_SIZES = (GROUP, GROUP, GROUP, N_HEADS, GROUP, GROUP, GROUP, 2 * GROUP, GROUP,
                  N_HEADS, N_HEADS, GROUP, GROUP, GROUP, GROUP)
_FULL_GROUPS = (0, 1, 2, 4, 5, 6, 7, 8, 11, 12, 13, 14)
_GATE_GROUPS = (3, 9, 10)
N_FULL = 13 * GROUP
N_IN_PAD = N_FULL + LANES


def _cparams(sem):
    return pltpu.CompilerParams(dimension_semantics=sem, vmem_limit_bytes=VMEM_LIMIT)


def _nt_dot(a, b):
    return lax.dot_general(a, b, (((1,), (1,)), ((), ())), preferred_element_type=F32)


def _dot(a, b):
    return jnp.dot(a, b, preferred_element_type=F32)


def _split3(x):
    hi = x.astype(BF16)
    r = x - hi.astype(F32)
    mid = r.astype(BF16)
    lo = (r - mid.astype(F32)).astype(BF16)
    return hi, mid, lo


def _log_sigmoid(x):
    return jnp.minimum(x, 0.0) - jnp.log1p(jnp.exp(-jnp.abs(x)))


def _mod_kernel(c_ref, w_ref, b_ref, o_ref):
    c = c_ref[...]
    a = (c * jax.nn.sigmoid(c)).astype(BF16)
    o_ref[0] = _dot(a, w_ref[0].astype(BF16)) + b_ref[0]


def _modulation(c_all, w_mod, b_mod):
    depth, d, n = w_mod.shape
    rows = c_all.shape[0]
    tn = 768
    return pl.pallas_call(
        _mod_kernel,
        grid=(depth, n // tn),
        in_specs=[pl.BlockSpec((rows, d), lambda l, j: (0, 0)),
                  pl.BlockSpec((1, d, tn), lambda l, j: (l, 0, j)),
                  pl.BlockSpec((1, 1, tn), lambda l, j: (l, 0, j))],
        out_specs=pl.BlockSpec((1, rows, tn), lambda l, j: (l, 0, j)),
        out_shape=jax.ShapeDtypeStruct((depth, rows, n), F32),
        compiler_params=_cparams(("parallel", "parallel")),
        name="modulation",
    )(c_all, w_mod, b_mod.reshape(depth, 1, n))


def _regroup_kernel(w_ref, o_ref):
    starts = np.concatenate([[0], np.cumsum(IN_SPLIT_SIZES)]).tolist()
    col = 0
    for g in _FULL_GROUPS + _GATE_GROUPS:
        width = IN_SPLIT_SIZES[g]
        o_ref[0, :, col:col + width] = w_ref[0, :, starts[g]:starts[g] + width].astype(BF16)
        col += width
    o_ref[0, :, col:N_IN_PAD] = jnp.zeros((o_ref.shape[1], N_IN_PAD - col), BF16)


def _regroup_w_in(w_in):
    depth, d, n_in = w_in.shape
    tr = 256
    return pl.pallas_call(
        _regroup_kernel,
        grid=(depth, d // tr),
        in_specs=[pl.BlockSpec((1, tr, n_in), lambda l, r: (l, r, 0))],
        out_specs=pl.BlockSpec((1, tr, N_IN_PAD), lambda l, r: (l, r, 0)),
        out_shape=jax.ShapeDtypeStruct((depth, d, N_IN_PAD), BF16),
        compiler_params=_cparams(("parallel", "parallel")),
        name="regroup_w_in",
    )(w_in)


_IN_OUTS = (
    ("fqa", 4 * LANES, BF16),
    ("fka", 4 * LANES, BF16),
    ("fvb", GROUP, BF16),
    ("dqv", 8 * LANES, BF16),
    ("dkb", GROUP, BF16), ("dvb", GROUP, BF16),
    ("mqk", 2 * GROUP, F32), ("mvb", GROUP, BF16), ("mos", GROUP, F32),
    ("bqm", 4 * LANES, BF16),
    ("bkb", GROUP, BF16), ("bvb", GROUP, BF16),
    ("gates", LANES, F32),
    ("cum", LANES, F32),
)
_IN_STATE = ("fk", "fv", "dk", "dv", "bk", "bv")
N_IN_ARGS = 10


def _in_proj_kernel(*refs, bb, tt, running, ml_blocks, n_prev):
    (x_ref, sh_ref, sc_ref, g1_ref, w_ref, b_ref, gains_ref, s64_ref, s32_ref, tril_ref) = refs[:N_IN_ARGS]
    outs = refs[N_IN_ARGS + n_prev:]
    (fqa_ref, fka_ref, fvb_ref, dqv_ref, dkb_ref, dvb_ref, mqk_ref, mvb_ref, mos_ref, bqm_ref, bkb_ref,
     bvb_ref, gates_ref, cum_ref) = outs[:len(_IN_OUTS)]
    state = dict(zip(_IN_STATE, outs[len(_IN_OUTS):]))
    carry_ref = outs[len(_IN_OUTS) + len(_IN_STATE)]
    tm = bb * tt

    def put_state(name, val):
        ref = state[name]
        if not running:
            ref[...] = val
        elif name in ("bk", "bv"):
            @pl.when(pl.program_id(1) == pl.num_programs(1) - 1)
            def _():
                ref[0, 0] = val.T
        else:
            ref[0, 0] = val.T

    x = x_ref[...]
    ms = jnp.mean(x * x, axis=-1, keepdims=True)
    h = x * lax.rsqrt(ms + EPS) * g1_ref[...]
    h = h * (1.0 + sc_ref[...]) + sh_ref[...]
    hb = h.reshape(tm, D_MODEL).astype(BF16)

    group_order = iter([13] + list(range(13)))
    pending = []

    def issue():
        g = next(group_order, None)
        if g is not None:
            c0, width = GROUP * g, (LANES if g == 13 else GROUP)
            pending.append((g, _dot(hb, w_ref[0, :, c0:c0 + width]) + b_ref[:, c0:c0 + width]))

    def proj(g, width=GROUP):
        got, z = pending.pop(0)
        assert got == g and z.shape[1] == width
        issue()
        return z

    issue()

    def rms_seg(z, s_ref, row, n):
        ss = _dot((z * z).astype(BF16), s_ref[...])
        return z * lax.rsqrt(ss * (1.0 / n) + EPS) * gains_ref[row:row + 1, :]

    lane = lax.broadcasted_iota(jnp.int32, (tm, LANES), 1)

    zg = proj(13, LANES)
    is_ls = (lane < N_HEADS) | ((lane >= 2 * N_HEADS) & (lane < N_GATES))
    gates = jnp.where(is_ls, _log_sigmoid(zg), zg)
    gates_ref[...] = gates
    ghi, gmid, glo = _split3(gates)
    tril = tril_ref[...]
    lane_b = lax.broadcasted_iota(jnp.int32, (LANES, LANES), 1)
    if running:
        @pl.when(pl.program_id(1) == 0)
        def _():
            carry_ref[...] = jnp.zeros_like(carry_ref)
        carry_a = carry_ref[0:1, :]
    carry_b = None
    cums = []
    for blk in range(tm // LANES):
        sl = slice(LANES * blk, LANES * blk + LANES)
        p = _dot(tril, ghi[sl]) + _dot(tril, gmid[sl]) + _dot(tril, glo[sl])
        if running:
            ca = p + carry_a
            carry_a = ca[LANES - 1:LANES, :]
            cb = p if blk % ml_blocks == 0 else p + carry_b
            carry_b = cb[LANES - 1:LANES, :]
            p = jnp.where(lane_b < N_HEADS, ca, cb)
        cums.append(p)
    if running:
        carry_ref[0:1, :] = carry_a
    cum = jnp.concatenate(cums, axis=0)
    cum_ref[...] = cum

    fqn = rms_seg(proj(0), s64_ref, 0, HEAD_DIM) * (HEAD_DIM ** -0.5 * LOG2E)
    fkn = rms_seg(proj(1), s64_ref, 1, HEAD_DIM)
    put_state("fk", fkn)
    fv = proj(2)
    put_state("fv", fv)
    fvb_ref[...] = fv.astype(BF16)
    ones_q = jnp.where((lane >= HEAD_DIM + 3) & (lane < HEAD_DIM + 6), 1.0, 0.0)
    ones_k = jnp.where((lane >= HEAD_DIM) & (lane < HEAD_DIM + 3), 1.0, 0.0)
    for hd in range(N_HEADS):
        pr, e = divmod(hd, 2)
        bq = fqn[:, LANES * pr:LANES * pr + LANES]
        bk = fkn[:, LANES * pr:LANES * pr + LANES]
        if e:
            bq = pltpu.roll(bq, HEAD_DIM, axis=1)
            bk = pltpu.roll(bk, HEAD_DIM, axis=1)
        cbc = jnp.broadcast_to(cum[:, hd:hd + 1], (tm, LANES)) * LOG2E
        chi = cbc.astype(BF16).astype(F32)
        r = cbc - chi
        cmid = r.astype(BF16).astype(F32)
        clo = (r - cmid).astype(BF16).astype(F32)
        aq = jnp.where(lane == HEAD_DIM, chi,
                       jnp.where(lane == HEAD_DIM + 1, cmid, jnp.where(lane == HEAD_DIM + 2, clo, ones_q)))
        ak = jnp.where(lane == HEAD_DIM + 3, -chi,
                       jnp.where(lane == HEAD_DIM + 4, -cmid, jnp.where(lane == HEAD_DIM + 5, -clo, ones_k)))
        fqa_ref[:, LANES * hd:LANES * hd + LANES] = jnp.where(lane < HEAD_DIM, bq, aq).astype(BF16)
        fka_ref[:, LANES * hd:LANES * hd + LANES] = jnp.where(lane < HEAD_DIM, bk, ak).astype(BF16)

    dqn = rms_seg(proj(3), s32_ref, 2, DIFF_HALF) * (DIFF_HALF ** -0.5 * LOG2E)
    dkn = rms_seg(proj(4), s32_ref, 3, DIFF_HALF)
    put_state("dk", dkn)
    dkb_ref[...] = dkn.astype(BF16)
    dv = proj(5)
    put_state("dv", dv)
    dvb_ref[...] = dv.astype(BF16)
    for pr in range(2):
        blk = dqn[:, LANES * pr:LANES * pr + LANES]
        for e in range(2):
            for jm in range(2):
                l0 = HEAD_DIM * e + DIFF_HALF * jm
                idx = 4 * pr + 2 * e + jm
                dqv_ref[:, LANES * idx:LANES * idx + LANES] = jnp.where(
                    (lane >= l0) & (lane < l0 + DIFF_HALF), blk, 0.0).astype(BF16)

    mqk_ref[:, 0:GROUP] = proj(6)
    mqk_ref[:, GROUP:2 * GROUP] = proj(7)
    mvb_ref[...] = proj(8).astype(BF16)
    mos_ref[...] = jax.nn.sigmoid(proj(9))

    bqn = rms_seg(proj(10), s64_ref, 4, HEAD_DIM) * (HEAD_DIM ** -0.5 * LOG2E)
    bkn = rms_seg(proj(11), s64_ref, 5, HEAD_DIM)
    put_state("bk", bkn)
    bkb_ref[...] = bkn.astype(BF16)
    bv = proj(12)
    put_state("bv", bv)
    bvb_ref[...] = bv.astype(BF16)
    for hd in range(N_HEADS):
        pr, e = divmod(hd, 2)
        blk = bqn[:, LANES * pr:LANES * pr + LANES]
        bqm_ref[:, LANES * hd:LANES * hd + LANES] = jnp.where(
            (lane >= HEAD_DIM * e) & (lane < HEAD_DIM * e + HEAD_DIM), blk, 0.0).astype(BF16)


def _in_proj(x, sh, sc, g1, w, b, gains, s64, s32, tril, *, bb, tt, running, ml_blocks,
             layer=0, depth=1, prev_state=None):
    bx, tx, d = x.shape
    n = bx * tx
    nb, nt = bx // bb, tx // tt
    tm = bb * tt
    const = lambda i, t: (0, 0)
    tok = lambda i, t: (i * nt + t, 0)
    keep = min(BAND_ROWS, tx)
    if running:
        assert bb == 1 and keep == tm
        st_shapes = [(depth, bx, GROUP, tx)] * 4 + [(depth, bx, GROUP, keep)] * 2
        st_specs = ([pl.BlockSpec((1, 1, GROUP, tm), lambda i, t: (layer, i, 0, t))] * 4
                    + [pl.BlockSpec((1, 1, GROUP, keep), lambda i, t: (layer, i, 0, 0))] * 2)
    else:
        st_shapes = [(n, GROUP)] * 6
        st_specs = [pl.BlockSpec((tm, GROUP), tok)] * 6
    prev = list(prev_state) if prev_state is not None else []
    aliases = {N_IN_ARGS + k: len(_IN_OUTS) + k for k in range(len(prev))}
    in_specs = [
        pl.BlockSpec((bb, tt, d), lambda i, t: (i, t, 0)),
        pl.BlockSpec((bb, 1, d), lambda i, t: (i, 0, 0)),
        pl.BlockSpec((bb, 1, d), lambda i, t: (i, 0, 0)),
        pl.BlockSpec((1, d), const),
        pl.BlockSpec((1, d, N_IN_PAD), lambda i, t: (layer, 0, 0), pipeline_mode=pl.Buffered(1)),
        pl.BlockSpec((1, N_IN_PAD), const),
        pl.BlockSpec((8, GROUP), const),
        pl.BlockSpec((GROUP, GROUP), const),
        pl.BlockSpec((GROUP, GROUP), const),
        pl.BlockSpec((LANES, LANES), const),
    ]
    in_specs += [pl.BlockSpec(memory_space=pl.ANY)] * len(prev)
    out_specs = [pl.BlockSpec((tm, wd), tok) for _, wd, _ in _IN_OUTS] + st_specs
    out_shape = ([jax.ShapeDtypeStruct((n, wd), dt) for _, wd, dt in _IN_OUTS]
                 + [jax.ShapeDtypeStruct(s, F32) for s in st_shapes])
    outs = pl.pallas_call(
        functools.partial(_in_proj_kernel, bb=bb, tt=tt, running=running, ml_blocks=ml_blocks,
                          n_prev=len(prev)),
        grid=(nb, nt),
        in_specs=in_specs,
        out_specs=out_specs,
        out_shape=out_shape,
        scratch_shapes=[pltpu.VMEM((8, LANES), F32)],
        input_output_aliases=aliases,
        compiler_params=_cparams(("arbitrary", "arbitrary")),
        name="in_proj",
    )(x, sh, sc, g1, w, b, gains, s64, s32, tril, *prev)
    ops = {name: o for (name, _, _), o in zip(_IN_OUTS, outs)}
    return ops, list(outs[len(_IN_OUTS):])


def _softmax_step(s, v, m_ref, l_ref, acc_ref, c, shift=None):
    nt = s.shape[1] // LANES
    tiles = [s[:, LANES * t:LANES * t + LANES] for t in range(nt)]
    m_prev = m_ref[c]
    m_cur = jnp.max(s, axis=-1, keepdims=True)
    if shift is not None:
        m_cur = m_cur - shift
    m_new = jnp.maximum(m_prev, m_cur)
    alpha = jnp.exp2(m_prev - m_new)
    m_sub = m_new if shift is None else m_new + shift
    ps = [jnp.exp2(t - m_sub) for t in tiles]
    psum = ps[0]
    for p in ps[1:]:
        psum = psum + p
    l_ref[c] = alpha * l_ref[c] + psum
    p = jnp.concatenate([p.astype(BF16) for p in ps], axis=1)
    acc_ref[c] = alpha * acc_ref[c] + _dot(p, v)
    m_ref[c] = m_new


def _pipelined(n, make_s, consume):
    s_next = make_s(0)
    for c in range(n):
        s_cur = s_next
        if c + 1 < n:
            s_next = make_s(c + 1)
        consume(c, s_cur)


def _row_sum(l):
    return jnp.sum(l, axis=-1, keepdims=True)


def _causal_pairs(nq):
    qi = np.array([i for i in range(nq) for _ in range(i + 1)], np.int32)
    kj = np.array([j for i in range(nq) for j in range(i + 1)], np.int32)
    return jnp.asarray(qi), jnp.asarray(kj)


def _init_softmax_state(m_ref, l_ref, acc_ref):
    m_ref[...] = jnp.full_like(m_ref, NEG)
    l_ref[...] = jnp.zeros_like(l_ref)
    acc_ref[...] = jnp.zeros_like(acc_ref)


def _fox_kernel(qi_ref, kj_ref, q_ref, k_ref, v_ref, o_ref, m_ref, l_ref, acc_ref, *, tq):
    p = pl.program_id(1)
    i = qi_ref[p]
    j = kj_ref[p]

    @pl.when(j == 0)
    def _():
        _init_softmax_state(m_ref, l_ref, acc_ref)

    def step(diag):
        q = q_ref[0]
        k = k_ref[0]
        v = v_ref[0]
        if diag:
            row = lax.broadcasted_iota(jnp.int32, (tq, tq), 0)
            col = lax.broadcasted_iota(jnp.int32, (tq, tq), 1)
            keep = col <= row

        def make_s(hd):
            s = _nt_dot(q[:, LANES * hd:LANES * hd + LANES], k[:, LANES * hd:LANES * hd + LANES])
            return jnp.where(keep, s, NEG) if diag else s

        def consume(hd, s):
            pr = hd // 2
            _softmax_step(s, v[:, LANES * pr:LANES * pr + LANES], m_ref, l_ref, acc_ref, hd)

        _pipelined(N_HEADS, make_s, consume)

    @pl.when(j < i)
    def _():
        step(False)

    @pl.when(j == i)
    def _():
        step(True)
        lane = lax.broadcasted_iota(jnp.int32, (tq, LANES), 1)
        for pr in range(2):
            oe = acc_ref[2 * pr] / _row_sum(l_ref[2 * pr])
            oo = acc_ref[2 * pr + 1] / _row_sum(l_ref[2 * pr + 1])
            o_ref[0, :, LANES * pr:LANES * pr + LANES] = jnp.where(lane < HEAD_DIM, oe, oo).astype(BF16)


def _fox_prompt(qa, ka, vb, *, tq):
    b, t, _ = qa.shape
    qi, kj = _causal_pairs(t // tq)
    q_map = lambda bi, p, qi, kj: (bi, qi[p], 0)
    k_map = lambda bi, p, qi, kj: (bi, kj[p], 0)
    return pl.pallas_call(
        functools.partial(_fox_kernel, tq=tq),
        grid_spec=pltpu.PrefetchScalarGridSpec(
            num_scalar_prefetch=2,
            grid=(b, qi.shape[0]),
            in_specs=[pl.BlockSpec((1, tq, 4 * LANES), q_map), pl.BlockSpec((1, tq, 4 * LANES), k_map),
                      pl.BlockSpec((1, tq, GROUP), k_map)],
            out_specs=pl.BlockSpec((1, tq, GROUP), q_map),
            scratch_shapes=[pltpu.VMEM((N_HEADS, tq, LANES), F32), pltpu.VMEM((N_HEADS, tq, LANES), F32),
                            pltpu.VMEM((N_HEADS, tq, LANES), F32)]),
        out_shape=jax.ShapeDtypeStruct((b, t, GROUP), BF16),
        compiler_params=_cparams(("parallel", "arbitrary")),
        name="fox_prompt",
    )(qi, kj, qa, ka, vb)


def _diff_lambda(lp, lam_init):
    a = jnp.sum(lp[0:1, :] * lp[1:2, :], axis=-1, keepdims=True)
    b = jnp.sum(lp[2:3, :] * lp[3:4, :], axis=-1, keepdims=True)
    return jnp.exp(a) - jnp.exp(b) + lam_init


def _diff_finish(acc_ref, l_ref, lam, gsub, lam_init, rows, o_ref):
    lane = lax.broadcasted_iota(jnp.int32, (rows, LANES), 1)
    for pr in range(2):
        outs = []
        for e in range(2):
            hd = 2 * pr + e
            o = (acc_ref[2 * hd] / _row_sum(l_ref[2 * hd])
                 - lam * (acc_ref[2 * hd + 1] / _row_sum(l_ref[2 * hd + 1])))
            valid = (lane >= HEAD_DIM * e) & (lane < HEAD_DIM * e + HEAD_DIM)
            ms = jnp.sum(jnp.where(valid, o * o, 0.0), axis=-1, keepdims=True) * (1.0 / HEAD_DIM)
            outs.append(o * lax.rsqrt(ms + EPS) * gsub * (1.0 - lam_init))
        o_ref[0, :, LANES * pr:LANES * pr + LANES] = jnp.where(lane < HEAD_DIM, outs[0], outs[1]).astype(BF16)


def _alibi_slope(hd):
    return 2.0 ** (-8.0 * (hd + 1) / N_HEADS) * LOG2E


def _alibi_features(tq):
    pos = np.arange(tq, dtype=np.float32)[:, None]
    lane = np.arange(LANES)[None, :]
    rnd = lambda v: v.astype(BF16).astype(np.float32)

    def split(v):
        v = v.astype(np.float32)
        hi = rnd(v)
        mid = rnd(v - hi)
        lo = rnd(v - hi - mid)
        return hi, mid, lo

    def block(e, first, second):
        out = np.zeros((tq, LANES), np.float32)
        for n, val in enumerate(first + second):
            out = np.where(lane == 8 * e + n, val, out)
        return out

    one = (np.ones((tq, 1), np.float32),) * 3
    qa = [block(hd % 2, split(-_alibi_slope(hd) * pos), one) for hd in range(N_HEADS)]
    ka = [block(0, one, split(_alibi_slope(2 * pr) * pos)) + block(1, one, split(_alibi_slope(2 * pr + 1) * pos))
          for pr in range(2)]
    return (jnp.asarray(np.concatenate(qa, axis=1), dtype=BF16),
            jnp.asarray(np.concatenate(ka, axis=1), dtype=BF16))


def _diff_kernel(qi_ref, kj_ref, q_ref, k_ref, v_ref, qa_ref, ka_ref, lam_ref, gsub_ref, o_ref,
                 m_ref, l_ref, acc_ref, *, tq, lam_init):
    p = pl.program_id(1)
    i = qi_ref[p]
    j = kj_ref[p]

    @pl.when(j == 0)
    def _():
        _init_softmax_state(m_ref, l_ref, acc_ref)

    def step(diag):
        q = q_ref[0]
        k = k_ref[0]
        v = v_ref[0]
        if diag:
            row = lax.broadcasted_iota(jnp.int32, (tq, tq), 0)
            col = lax.broadcasted_iota(jnp.int32, (tq, tq), 1)
            seen = (col // CHUNK) <= (row // CHUNK)
            ahead = jnp.maximum(col - row, 0).astype(F32)
        else:
            tile_gap = ((i - j) * tq).astype(F32)

        def make_s(c):
            hd, jm = divmod(c, 2)
            pr, e = divmod(hd, 2)
            idx = 4 * pr + 2 * e + jm
            qx = jnp.concatenate([q[:, LANES * idx:LANES * idx + LANES], qa_ref[:, LANES * hd:LANES * hd + LANES]],
                                 axis=1)
            kx = jnp.concatenate([k[:, LANES * pr:LANES * pr + LANES], ka_ref[:, LANES * pr:LANES * pr + LANES]],
                                 axis=1)
            s = _nt_dot(qx, kx)
            if diag:
                s = jnp.where(seen, s - (2.0 * _alibi_slope(hd)) * ahead, NEG)
            return s

        def consume(c, s):
            pr = c // 4
            shift = None if diag else _alibi_slope(c // 2) * tile_gap
            _softmax_step(s, v[:, LANES * pr:LANES * pr + LANES], m_ref, l_ref, acc_ref, c, shift)

        _pipelined(2 * N_HEADS, make_s, consume)

    @pl.when(j < i)
    def _():
        step(False)

    @pl.when(j == i)
    def _():
        step(True)
        lam = _diff_lambda(lam_ref[...], lam_init)
        _diff_finish(acc_ref, l_ref, lam, gsub_ref[...], lam_init, tq, o_ref)


def _diff_prompt(qv, kb, vb, lam_p, gsub, *, tq, lam_init):
    b, t, _ = qv.shape
    qi, kj = _causal_pairs(t // tq)
    qa, ka = _alibi_features(tq)
    q_map = lambda bi, p, qi, kj: (bi, qi[p], 0)
    k_map = lambda bi, p, qi, kj: (bi, kj[p], 0)
    const = lambda bi, p, qi, kj: (0, 0)
    return pl.pallas_call(
        functools.partial(_diff_kernel, tq=tq, lam_init=lam_init),
        grid_spec=pltpu.PrefetchScalarGridSpec(
            num_scalar_prefetch=2,
            grid=(b, qi.shape[0]),
            in_specs=[pl.BlockSpec((1, tq, 8 * LANES), q_map), pl.BlockSpec((1, tq, GROUP), k_map),
                      pl.BlockSpec((1, tq, GROUP), k_map),
                      pl.BlockSpec((tq, N_HEADS * LANES), const), pl.BlockSpec((tq, 2 * LANES), const),
                      pl.BlockSpec((4, DIFF_HALF), const), pl.BlockSpec((1, LANES), const)],
            out_specs=pl.BlockSpec((1, tq, GROUP), q_map),
            scratch_shapes=[pltpu.VMEM((2 * N_HEADS, tq, LANES), F32),
                            pltpu.VMEM((2 * N_HEADS, tq, LANES), F32),
                            pltpu.VMEM((2 * N_HEADS, tq, LANES), F32)]),
        out_shape=jax.ShapeDtypeStruct((b, t, GROUP), BF16),
        compiler_params=_cparams(("parallel", "arbitrary")),
        name="diff_prompt",
    )(qi, kj, qv, kb, vb, qa, ka, lam_p, gsub)


def _band_bias_kernel(g_ref, bp_ref, bs_ref):
    lane = lax.broadcasted_iota(jnp.int32, (CHUNK, BAND_WIN), 1)
    width = g_ref.shape[1]
    for hd in range(N_HEADS):
        x = jnp.broadcast_to(g_ref[hd:hd + 1, :], (CHUNK, width))
        rp = pltpu.roll(x, width - CHUNK + 1, axis=1, stride=1, stride_axis=0)
        bp_ref[hd] = jnp.where(lane < CHUNK, NEG, rp[:, :BAND_WIN] * LOG2E)
        rs = pltpu.roll(x, width - 2 * CHUNK + 1, axis=1, stride=1, stride_axis=0)
        bs_ref[hd] = rs[:, :BAND_WIN] * LOG2E


def _band_bias(table):
    gv = jnp.concatenate([jnp.broadcast_to(table[:, 2 * REL_CLIP:], (N_HEADS, 4 * LANES)),
                          table[:, 2 * REL_CLIP - 1::-1]], axis=1)
    shp = jax.ShapeDtypeStruct((N_HEADS, CHUNK, BAND_WIN), F32)
    return pl.pallas_call(_band_bias_kernel, out_shape=(shp, shp), name="band_bias")(gv)


def _band_kernel(q_ref, k_ref, v_ref, bias_ref, o_ref, kpad_ref, vpad_ref, *, tq):
    t = pl.program_id(1)
    n_sub = tq // CHUNK
    lane = lax.broadcasted_iota(jnp.int32, (CHUNK, LANES), 1)
    col = lax.broadcasted_iota(jnp.int32, (CHUNK, BAND_WIN), 1)

    @pl.when(t == 0)
    def _():
        for src, dst in ((k_ref, kpad_ref), (v_ref, vpad_ref)):
            dst[0:BAND_WIN, :] = jnp.zeros((BAND_WIN, GROUP), BF16)
            dst[BAND_WIN:, :] = src[0]

    def window(ref, c, pr):
        start = pl.multiple_of((t * n_sub + c + 1) * CHUNK, CHUNK)
        return ref[pl.ds(start, BAND_WIN), LANES * pr:LANES * pr + LANES]

    def make_s(idx):
        c, hd = divmod(idx, N_HEADS)
        q = q_ref[0, CHUNK * c:CHUNK * c + CHUNK, LANES * hd:LANES * hd + LANES]
        valid = col + (t * n_sub + c - BAND_CHUNKS - 1) * CHUNK >= 0
        return jnp.where(valid, _nt_dot(q, window(kpad_ref, c, hd // 2)) + bias_ref[hd], NEG)

    even = {}

    def consume(idx, s):
        c, hd = divmod(idx, N_HEADS)
        pr, e = divmod(hd, 2)
        m = jnp.max(s, axis=-1, keepdims=True)
        p = jnp.exp2(s - m)
        l = jnp.sum(p, axis=-1, keepdims=True)
        o = _dot(p.astype(BF16), window(vpad_ref, c, pr)) / l
        if e == 0:
            even[pr] = o
        else:
            o_ref[0, CHUNK * c:CHUNK * c + CHUNK, LANES * pr:LANES * pr + LANES] = jnp.where(
                lane < HEAD_DIM, even[pr], o).astype(BF16)

    _pipelined(n_sub * N_HEADS, make_s, consume)


def _band_prompt(qm, kb, vb, bias, *, tq):
    b, t, _ = qm.shape
    return pl.pallas_call(
        functools.partial(_band_kernel, tq=tq),
        grid=(b, t // tq),
        in_specs=[pl.BlockSpec((1, tq, 4 * LANES), lambda bi, i: (bi, i, 0)),
                  pl.BlockSpec((1, t, GROUP), lambda bi, i: (bi, 0, 0)),
                  pl.BlockSpec((1, t, GROUP), lambda bi, i: (bi, 0, 0)),
                  pl.BlockSpec((N_HEADS, CHUNK, BAND_WIN), lambda bi, i: (0, 0, 0))],
        out_specs=pl.BlockSpec((1, tq, GROUP), lambda bi, i: (bi, i, 0)),
        out_shape=jax.ShapeDtypeStruct((b, t, GROUP), BF16),
        scratch_shapes=[pltpu.VMEM((BAND_WIN + t, GROUP), BF16), pltpu.VMEM((BAND_WIN + t, GROUP), BF16)],
        compiler_params=_cparams(("arbitrary", "arbitrary")),
        name="band_prompt",
    )(qm, kb, vb, bias)


def _mlstm_kernel(mqk_ref, mv_ref, gates_ref, cum_ref, mos_ref, cw_ref, cb_ref, gmh_ref, shead_ref, bd_ref,
                  c0_ref, n0_ref, m0_ref, conv0_ref,
                  o_ref, c_out_ref, n_out_ref, m_out_ref,
                  cbuf_ref, c_ref, n_ref, m_ref, *, rows, valid, ns):
    t = pl.program_id(1)

    @pl.when(t == 0)
    def _():
        for s in range(ns):
            c_ref[s] = c0_ref[s]
            n_ref[s] = n0_ref[s]
            m_ref[s] = m0_ref[s]
            cbuf_ref[s, 0:8, :] = conv0_ref[s]

    for s in range(ns):
        _mlstm_chunk(mqk_ref.at[s], mv_ref.at[s], gates_ref.at[s], cum_ref.at[s], mos_ref.at[s], cw_ref, cb_ref,
                     gmh_ref, shead_ref, bd_ref, o_ref.at[s], cbuf_ref.at[s], c_ref.at[s], n_ref.at[s],
                     m_ref.at[s], rows=rows, valid=valid)

    @pl.when(t == pl.num_programs(1) - 1)
    def _():
        for s in range(ns):
            c_out_ref[s] = c_ref[s]
            n_out_ref[s] = n_ref[s]
            m_out_ref[s] = m_ref[s]


def _mlstm_chunk(mqk_ref, mv_ref, gates_ref, cum_ref, mos_ref, cw_ref, cb_ref, gmh_ref, shead_ref, bd_ref,
                 o_ref, cbuf_ref, c_ref, n_ref, m_ref, *, rows, valid):
    def padded(a, fill=0.0):
        if valid == rows:
            return a
        return jnp.concatenate([a, jnp.full((rows - valid, a.shape[1]), fill, a.dtype)], axis=0)

    u = padded(mqk_ref[...])
    cbuf_ref[8:8 + rows, :] = u
    y = cb_ref[...] + cw_ref[3:4, :] * u
    for jw in range(CONV_W - 1):
        y = y + cw_ref[jw:jw + 1, :] * cbuf_ref[5 + jw:5 + jw + rows, :]
    cbuf_ref[0:8, :] = cbuf_ref[valid:valid + 8, :]
    qk = y * jax.nn.sigmoid(y)
    q = qk[:, 0:GROUP]
    k = qk[:, GROUP:2 * GROUP] * (HEAD_DIM ** -0.5)
    qb = q.astype(BF16)
    kb = k.astype(BF16)
    v = padded(mv_ref[...])
    mos = padded(mos_ref[...])

    g = gates_ref[...]
    cm = cum_ref[...]
    if valid != rows:
        g = padded(g, NEG)
        cm = jnp.concatenate([cm, jnp.broadcast_to(cm[valid - 1:valid, :], (rows - valid, LANES))], axis=0)
    g_t = g.T
    cm_t = cm.T

    row = lax.broadcasted_iota(jnp.int32, (rows, rows), 0)
    col = lax.broadcasted_iota(jnp.int32, (rows, rows), 1)
    causal = col <= row
    lane_g = lax.broadcasted_iota(jnp.int32, (rows, GROUP), 1)
    lane_r = lax.broadcasted_iota(jnp.int32, (1, GROUP), 1)
    m_prev_all = m_ref[...]

    zeros = jnp.zeros((rows, GROUP), F32)
    num, a_full, ws_full, mt_full, wend_full = zeros, zeros, zeros, zeros, zeros
    decay_lane = jnp.zeros((1, GROUP), F32)
    mnew_lane = jnp.zeros((1, GROUP), F32)
    head_masks = [(lane_g >= HEAD_DIM * hd) & (lane_g < HEAD_DIM * hd + HEAD_DIM) for hd in range(N_HEADS)]
    sqks = [_nt_dot(jnp.where(hm, q, 0.0).astype(BF16), kb) for hm in head_masks]
    c_old = c_ref[...]
    n_old = n_ref[...]
    shead = shead_ref[...]
    q_c = _dot(qb, c_old.astype(BF16))
    q_n = _dot((q * n_old).astype(BF16), shead)
    for hd in range(N_HEADS):
        hm = head_masks[hd]
        hm_r = (lane_r >= HEAD_DIM * hd) & (lane_r < HEAD_DIM * hd + HEAD_DIM)
        f_c = cm[:, 2 * N_HEADS + hd:2 * N_HEADS + hd + 1]
        ig_c = g[:, N_HEADS + hd:N_HEADS + hd + 1]
        f_r = cm_t[2 * N_HEADS + hd:2 * N_HEADS + hd + 1, :]
        ig_r = g_t[N_HEADS + hd:N_HEADS + hd + 1, :]
        m_prev = m_prev_all[:, HEAD_DIM * hd:HEAD_DIM * hd + 1]
        logw = jnp.where(causal, f_c - (f_r - ig_r), NEG)
        logb = f_c + m_prev
        m_t = jnp.maximum(logb, jnp.max(logw, axis=-1, keepdims=True))
        w = jnp.exp(logw - m_t) * sqks[hd]
        a = jnp.exp(logb - m_t)
        num = jnp.where(hm, _dot(w.astype(BF16), v), num)
        a_full = jnp.where(hm, a, a_full)
        ws_full = jnp.where(hm, jnp.sum(w, axis=-1, keepdims=True), ws_full)
        mt_full = jnp.where(hm, m_t, mt_full)
        f_end = f_c[rows - 1:rows, :]
        log_end = f_end - f_c + ig_c
        m_new = jnp.maximum(f_end + m_prev, jnp.max(log_end, axis=0, keepdims=True))
        wend_full = jnp.where(hm, jnp.exp(log_end - m_new), wend_full)
        decay_lane = jnp.where(hm_r, jnp.exp(f_end + m_prev - m_new), decay_lane)
        mnew_lane = jnp.where(hm_r, m_new, mnew_lane)

    num = num + a_full * q_c
    den = ws_full + a_full * q_n
    h = num / jnp.maximum(jnp.abs(den), jnp.exp(-mt_full))
    o = mos * h
    ss = _dot((o * o).astype(BF16), shead)
    on = o * lax.rsqrt(ss * (1.0 / HEAD_DIM) + EPS) * gmh_ref[...]
    o_ref[...] = on[0:valid, :].astype(BF16)

    wk = k * wend_full
    kv = lax.dot_general(wk.astype(BF16), v, (((0,), (0,)), ((), ())), preferred_element_type=F32)
    c_ref[...] = c_old * decay_lane + jnp.where(bd_ref[...] > 0.5, kv, 0.0)
    n_ref[...] = n_old * decay_lane + jnp.sum(wk, axis=0, keepdims=True)
    m_ref[...] = mnew_lane


def _mlstm(mqk, mvb, gates, cum, mos, cw, cb, gmh, shead, bd, c0, n0, m0, conv0, *, nb, t, valid, rows, ns):
    nc = t // valid
    seq = lambda a: a.reshape(nb, t, a.shape[-1])
    tok = lambda bi, i: (bi, i, 0)
    const = lambda bi, i: (0, 0)
    per_b = lambda bi, i: (bi, 0, 0)
    o, c_new, n_new, m_new = pl.pallas_call(
        functools.partial(_mlstm_kernel, rows=rows, valid=valid, ns=ns),
        grid=(nb // ns, nc),
        in_specs=[pl.BlockSpec((ns, valid, 2 * GROUP), tok), pl.BlockSpec((ns, valid, GROUP), tok),
                  pl.BlockSpec((ns, valid, LANES), tok), pl.BlockSpec((ns, valid, LANES), tok),
                  pl.BlockSpec((ns, valid, GROUP), tok),
                  pl.BlockSpec((CONV_W, 2 * GROUP), const), pl.BlockSpec((1, 2 * GROUP), const),
                  pl.BlockSpec((1, GROUP), const), pl.BlockSpec((GROUP, GROUP), const),
                  pl.BlockSpec((GROUP, GROUP), const),
                  pl.BlockSpec((ns, GROUP, GROUP), per_b), pl.BlockSpec((ns, 1, GROUP), per_b),
                  pl.BlockSpec((ns, 1, GROUP), per_b), pl.BlockSpec((ns, 8, 2 * GROUP), per_b)],
        out_specs=[pl.BlockSpec((ns, valid, GROUP), tok), pl.BlockSpec((ns, GROUP, GROUP), per_b),
                   pl.BlockSpec((ns, 1, GROUP), per_b), pl.BlockSpec((ns, 1, GROUP), per_b)],
        out_shape=[jax.ShapeDtypeStruct((nb, t, GROUP), BF16), jax.ShapeDtypeStruct((nb, GROUP, GROUP), F32),
                   jax.ShapeDtypeStruct((nb, 1, GROUP), F32), jax.ShapeDtypeStruct((nb, 1, GROUP), F32)],
        scratch_shapes=[pltpu.VMEM((ns, 8 + rows, 2 * GROUP), F32), pltpu.VMEM((ns, GROUP, GROUP), F32),
                        pltpu.VMEM((ns, 1, GROUP), F32), pltpu.VMEM((ns, 1, GROUP), F32)],
        compiler_params=_cparams(("parallel", "arbitrary")),
        name="mlstm",
    )(seq(mqk), seq(mvb), seq(gates), seq(cum), seq(mos), cw, cb, gmh, shead, bd, c0, n0, m0, conv0)
    return o.reshape(nb * t, GROUP), c_new, n_new, m_new


def _pad_rows(a, rows):
    return jnp.concatenate([a, jnp.zeros((rows - a.shape[0], a.shape[1]), a.dtype)], axis=0)


def _two_part_softmax(s_c, s_n, vt_c, v_n):
    m = jnp.maximum(jnp.max(s_c, axis=-1, keepdims=True), jnp.max(s_n, axis=-1, keepdims=True))
    p_c = jnp.exp2(s_c - m)
    p_n = jnp.exp2(s_n - m)
    l = jnp.sum(p_c, axis=-1, keepdims=True) + jnp.sum(p_n, axis=-1, keepdims=True)
    return _nt_dot(p_c.astype(BF16), vt_c) + _dot(p_n.astype(BF16), v_n), l


def _fox_sample_kernel(q_ref, kn_ref, vn_ref, ck_ref, cv_ref, clf_ref, lstr_ref, o_ref, *, tn, past):
    q = q_ref[0]
    kn = _pad_rows(kn_ref[0], LANES)
    vn = _pad_rows(vn_ref[0], LANES)
    ck = ck_ref[0]
    cv = cv_ref[0].astype(BF16)
    x = _pad_rows(clf_ref[0], 8)
    xhi, xmid, xlo = _split3(x)
    lstr = lstr_ref[...]
    suf = _dot(xhi, lstr) + _dot(xmid, lstr) + _dot(xlo, lstr)

    sub = lax.broadcasted_iota(jnp.int32, (8, past), 0)
    lane_o = lax.broadcasted_iota(jnp.int32, (tn, LANES), 1)
    row = lax.broadcasted_iota(jnp.int32, (tn, LANES), 0)
    causal = lane_o <= row
    fill = jnp.zeros((LANES - HEAD_DIM - 8, past), F32)
    for pr in range(2):
        outs = []
        for e in range(2):
            hd = 2 * pr + e
            sb = jnp.broadcast_to(suf[hd:hd + 1, :], (8, past)) * LOG2E
            shi = sb.astype(BF16).astype(F32)
            r = sb - shi
            smid = r.astype(BF16).astype(F32)
            slo = (r - smid).astype(BF16).astype(F32)
            aug = jnp.where(sub < 3, 1.0,
                            jnp.where(sub == 3, shi, jnp.where(sub == 4, smid, jnp.where(sub == 5, slo, 0.0))))
            kc = jnp.concatenate([ck[HEAD_DIM * hd:HEAD_DIM * hd + HEAD_DIM, :], aug, fill], axis=0).astype(BF16)
            qh = q[:, LANES * hd:LANES * hd + LANES]
            s_c = _dot(qh, kc)
            s_n = jnp.where(causal, _nt_dot(qh, kn[:, LANES * hd:LANES * hd + LANES]), NEG)
            acc, l = _two_part_softmax(s_c, s_n, cv[LANES * pr:LANES * pr + LANES, :],
                                       vn[:, LANES * pr:LANES * pr + LANES])
            outs.append(acc / l)
        o_ref[0, :, LANES * pr:LANES * pr + LANES] = jnp.where(lane_o < HEAD_DIM, outs[0], outs[1]).astype(BF16)


def _fox_sample(qa, ka, vb, ck, cv, clf, lstr):
    b, tn, _ = qa.shape
    past = ck.shape[2]
    per_b = lambda i: (i, 0, 0)
    return pl.pallas_call(
        functools.partial(_fox_sample_kernel, tn=tn, past=past),
        grid=(b,),
        in_specs=[pl.BlockSpec((1, tn, 4 * LANES), per_b), pl.BlockSpec((1, tn, 4 * LANES), per_b),
                  pl.BlockSpec((1, tn, GROUP), per_b), pl.BlockSpec((1, GROUP, past), per_b),
                  pl.BlockSpec((1, GROUP, past), per_b), pl.BlockSpec((1, N_HEADS, past), per_b),
                  pl.BlockSpec((past, past), lambda i: (0, 0))],
        out_specs=pl.BlockSpec((1, tn, GROUP), per_b),
        out_shape=jax.ShapeDtypeStruct((b, tn, GROUP), BF16),
        compiler_params=_cparams(("parallel",)),
        name="fox_sample",
    )(qa, ka, vb, ck, cv, clf, lstr)


def _diff_sample_kernel(q_ref, kn_ref, vn_ref, ck_ref, cv_ref, lam_ref, gsub_ref, o_ref, l_ref, acc_ref,
                        *, tn, past, lam_init):
    q = q_ref[0]
    kn = _pad_rows(kn_ref[0], LANES)
    vn = _pad_rows(vn_ref[0], LANES)
    ck = ck_ref[0].astype(BF16)
    cv = cv_ref[0].astype(BF16)
    row_c = lax.broadcasted_iota(jnp.int32, (tn, past), 0)
    col_c = lax.broadcasted_iota(jnp.int32, (tn, past), 1)
    dist_c = (past + row_c - col_c).astype(F32)
    row_n = lax.broadcasted_iota(jnp.int32, (tn, LANES), 0)
    col_n = lax.broadcasted_iota(jnp.int32, (tn, LANES), 1)
    dist_n = jnp.abs(row_n - col_n).astype(F32)
    real = col_n < tn
    for hd in range(N_HEADS):
        pr, e = divmod(hd, 2)
        slope = _alibi_slope(hd)
        for jm in range(2):
            idx = 4 * pr + 2 * e + jm
            qh = q[:, LANES * idx:LANES * idx + LANES]
            s_c = _dot(qh, ck[LANES * pr:LANES * pr + LANES, :]) - slope * dist_c
            s_n = jnp.where(real, _nt_dot(qh, kn[:, LANES * pr:LANES * pr + LANES]) - slope * dist_n, NEG)
            acc, l = _two_part_softmax(s_c, s_n, cv[LANES * pr:LANES * pr + LANES, :],
                                       vn[:, LANES * pr:LANES * pr + LANES])
            acc_ref[2 * hd + jm] = acc
            l_ref[2 * hd + jm] = l
    lam = _diff_lambda(lam_ref[...], lam_init)
    _diff_finish(acc_ref, l_ref, lam, gsub_ref[...], lam_init, tn, o_ref)


def _diff_sample(qv, kb, vb, ck, cv, lam_p, gsub, *, lam_init):
    b, tn, _ = qv.shape
    past = ck.shape[2]
    per_b = lambda i: (i, 0, 0)
    return pl.pallas_call(
        functools.partial(_diff_sample_kernel, tn=tn, past=past, lam_init=lam_init),
        grid=(b,),
        in_specs=[pl.BlockSpec((1, tn, 8 * LANES), per_b), pl.BlockSpec((1, tn, GROUP), per_b),
                  pl.BlockSpec((1, tn, GROUP), per_b), pl.BlockSpec((1, GROUP, past), per_b),
                  pl.BlockSpec((1, GROUP, past), per_b), pl.BlockSpec((4, DIFF_HALF), lambda i: (0, 0)),
                  pl.BlockSpec((1, LANES), lambda i: (0, 0))],
        out_specs=pl.BlockSpec((1, tn, GROUP), per_b),
        out_shape=jax.ShapeDtypeStruct((b, tn, GROUP), BF16),
        scratch_shapes=[pltpu.VMEM((2 * N_HEADS, tn, 1), F32), pltpu.VMEM((2 * N_HEADS, tn, LANES), F32)],
        compiler_params=_cparams(("parallel",)),
        name="diff_sample",
    )(qv, kb, vb, ck, cv, lam_p, gsub)


def _band_sample_kernel(q_ref, kn_ref, vn_ref, ck_ref, cv_ref, bias_ref, o_ref, *, tn, past):
    q = q_ref[0]
    kn = _pad_rows(kn_ref[0], LANES)
    vn = _pad_rows(vn_ref[0], LANES)
    ck = ck_ref[0].astype(BF16)
    cv = cv_ref[0].astype(BF16)
    lane = lax.broadcasted_iota(jnp.int32, (tn, LANES), 1)
    real = lane < tn
    for pr in range(2):
        outs = []
        for e in range(2):
            hd = 2 * pr + e
            qh = q[:, LANES * hd:LANES * hd + LANES]
            s_c = _dot(qh, ck[LANES * pr:LANES * pr + LANES, :]) + bias_ref[hd, 0:tn, 0:past]
            s_n = _nt_dot(qh, kn[:, LANES * pr:LANES * pr + LANES]) + bias_ref[hd, 0:tn, past:past + LANES]
            s_n = jnp.where(real, s_n, NEG)
            acc, l = _two_part_softmax(s_c, s_n, cv[LANES * pr:LANES * pr + LANES, :],
                                       vn[:, LANES * pr:LANES * pr + LANES])
            outs.append(acc / l)
        o_ref[0, :, LANES * pr:LANES * pr + LANES] = jnp.where(lane < HEAD_DIM, outs[0], outs[1]).astype(BF16)


def _band_sample(qm, kb, vb, ck, cv, bias):
    b, tn, _ = qm.shape
    past = ck.shape[2]
    per_b = lambda i: (i, 0, 0)
    return pl.pallas_call(
        functools.partial(_band_sample_kernel, tn=tn, past=past),
        grid=(b,),
        in_specs=[pl.BlockSpec((1, tn, 4 * LANES), per_b), pl.BlockSpec((1, tn, GROUP), per_b),
                  pl.BlockSpec((1, tn, GROUP), per_b), pl.BlockSpec((1, GROUP, past), per_b),
                  pl.BlockSpec((1, GROUP, past), per_b),
                  pl.BlockSpec((N_HEADS, CHUNK, BAND_WIN), lambda i: (0, 0, 0))],
        out_specs=pl.BlockSpec((1, tn, GROUP), per_b),
        out_shape=jax.ShapeDtypeStruct((b, tn, GROUP), BF16),
        compiler_params=_cparams(("parallel",)),
        name="band_sample",
    )(qm, kb, vb, ck, cv, bias)


def _out_proj_kernel(x_ref, gt_ref, a_ref, b_ref, c_ref, d_ref, w_ref, o_ref, *, bb, tt):
    acc = _dot(a_ref[...], w_ref[0, 0:GROUP, :])
    acc += _dot(b_ref[...], w_ref[0, GROUP:2 * GROUP, :])
    acc += _dot(c_ref[...], w_ref[0, 2 * GROUP:3 * GROUP, :])
    acc += _dot(d_ref[...], w_ref[0, 3 * GROUP:4 * GROUP, :])
    o_ref[...] = x_ref[...] + gt_ref[...] * acc.reshape(bb, tt, D_MODEL)


def _out_proj(x, gt, o_fox, o_diff, o_ml, o_band, w_out, *, bb, tt, layer):
    bx, tx, d = x.shape
    nb, nt = bx // bb, tx // tt
    tm = bb * tt
    tok = lambda i, t: (i * nt + t, 0)
    return pl.pallas_call(
        functools.partial(_out_proj_kernel, bb=bb, tt=tt),
        grid=(nb, nt),
        in_specs=[pl.BlockSpec((bb, tt, d), lambda i, t: (i, t, 0)),
                  pl.BlockSpec((bb, 1, d), lambda i, t: (i, 0, 0)),
                  pl.BlockSpec((tm, GROUP), tok), pl.BlockSpec((tm, GROUP), tok),
                  pl.BlockSpec((tm, GROUP), tok), pl.BlockSpec((tm, GROUP), tok),
                  pl.BlockSpec((1, d, d), lambda i, t: (layer, 0, 0), pipeline_mode=pl.Buffered(1))],
        out_specs=pl.BlockSpec((bb, tt, d), lambda i, t: (i, t, 0)),
        out_shape=jax.ShapeDtypeStruct(x.shape, F32),
        compiler_params=_cparams(("parallel", "parallel")),
        name="out_proj",
    )(x, gt, o_fox, o_diff, o_ml, o_band, w_out)


def _ffn_kernel(x_ref, sh_ref, sc_ref, gt_ref, g2_ref, wg_ref, wu_ref, wd_ref, o_ref, hb_ref, acc_ref,
                *, bb, tt):
    f = pl.program_id(2)
    tm = bb * tt

    @pl.when(f == 0)
    def _():
        x = x_ref[...]
        ms = jnp.mean(x * x, axis=-1, keepdims=True)
        h = x * lax.rsqrt(ms + EPS) * g2_ref[...]
        h = h * (1.0 + sc_ref[...]) + sh_ref[...]
        hb_ref[...] = h.reshape(tm, D_MODEL).astype(BF16)
        acc_ref[...] = jnp.zeros_like(acc_ref)

    hb = hb_ref[...]
    g = _dot(hb, wg_ref[0])
    u = _dot(hb, wu_ref[0])
    a = (g * jax.nn.sigmoid(g) * u).astype(BF16)
    acc_ref[...] += _dot(a, wd_ref[0])

    @pl.when(f == pl.num_programs(2) - 1)
    def _():
        o_ref[...] = x_ref[...] + gt_ref[...] * acc_ref[...].reshape(bb, tt, D_MODEL)


def _ffn(x, sh, sc, gt, g2, wg, wu, wd, *, bb, tt, tf, layer):
    bx, tx, d = x.shape
    nb, nt = bx // bb, tx // tt
    nf = D_FF // tf
    tm = bb * tt
    xs = pl.BlockSpec((bb, tt, d), lambda i, t, f: (i, t, 0))
    ms = pl.BlockSpec((bb, 1, d), lambda i, t, f: (i, 0, 0))
    return pl.pallas_call(
        functools.partial(_ffn_kernel, bb=bb, tt=tt),
        grid=(nb, nt, nf),
        in_specs=[xs, ms, ms, ms, pl.BlockSpec((1, d), lambda i, t, f: (0, 0)),
                  pl.BlockSpec((1, d, tf), lambda i, t, f: (layer, 0, f)),
                  pl.BlockSpec((1, d, tf), lambda i, t, f: (layer, 0, f)),
                  pl.BlockSpec((1, tf, d), lambda i, t, f: (layer, f, 0))],
        out_specs=xs,
        out_shape=jax.ShapeDtypeStruct(x.shape, F32),
        scratch_shapes=[pltpu.VMEM((tm, d), BF16), pltpu.VMEM((tm, d), F32)],
        compiler_params=_cparams(("parallel", "parallel", "arbitrary")),
        name="ffn",
    )(x, sh, sc, gt, g2, wg, wu, wd)


def _consts(seg):
    r = np.arange(GROUP)
    s64 = (r[:, None] // HEAD_DIM == r[None, :] // HEAD_DIM)
    s32 = (r[:, None] // DIFF_HALF == r[None, :] // DIFF_HALF)
    q = np.arange(LANES)
    tril = (q[None, :] <= q[:, None]) & (q[:, None] // seg == q[None, :] // seg)
    as_bf16 = lambda m: jnp.asarray(m.astype(np.float32), dtype=BF16)
    return as_bf16(s64), as_bf16(s32), as_bf16(tril), jnp.asarray(s64.astype(np.float32))


def _prep_layer(l, norm1_g, norm2_g, w_in_g, b_in, qk_g_fox, qk_g_diff, qk_g_band, conv_w, conv_b,
                diff_lambda, diff_subln_g, mlstm_norm_g, band_rel_bias, w_out, w_ffn_gate, w_ffn_up,
                w_ffn_down):
    starts = np.concatenate([[0], np.cumsum(IN_SPLIT_SIZES)]).tolist()
    order = list(_FULL_GROUPS) + list(_GATE_GROUPS)
    bl = b_in[l]
    b_cols = [bl[starts[g]:starts[g + 1]] for g in order]
    pad = N_IN_PAD - N_FULL - N_GATES
    b_cols.append(jnp.zeros((pad,), F32))
    gains = jnp.stack([
        jnp.tile(qk_g_fox[l, 0], N_HEADS), jnp.tile(qk_g_fox[l, 1], N_HEADS),
        jnp.tile(qk_g_diff[l, 0], 2 * N_HEADS), jnp.tile(qk_g_diff[l, 1], 2 * N_HEADS),
        jnp.tile(qk_g_band[l, 0], N_HEADS), jnp.tile(qk_g_band[l, 1], N_HEADS),
        jnp.zeros((GROUP,), F32), jnp.zeros((GROUP,), F32)])
    return dict(
        g1=norm1_g[l].reshape(1, D_MODEL), g2=norm2_g[l].reshape(1, D_MODEL),
        w_in=w_in_g,
        b_in=jnp.concatenate(b_cols).reshape(1, N_IN_PAD),
        gains=gains, conv_w=conv_w[l], conv_b=conv_b[l].reshape(1, 2 * GROUP),
        lam_p=diff_lambda[l], gsub=jnp.tile(diff_subln_g[l], 2).reshape(1, LANES),
        gmh=jnp.tile(mlstm_norm_g[l], N_HEADS).reshape(1, GROUP),
        table=band_rel_bias[l],
        w_out=w_out, wg=w_ffn_gate, wu=w_ffn_up, wd=w_ffn_down,
        lam_init=0.8 - 0.6 * math.exp(-0.3 * l))


def _layer(x, mod, lp, caches, *, bb, tt, layer, depth, prev_state=None):
    bx, tx, _ = x.shape
    n = bx * tx
    prompt = caches is None
    sh1, sc1, gt1, sh2, sc2, gt2 = [mod[:, i:i + 1, :] for i in range(6)]
    s64, s32, tril, bd = _consts(tx if prompt else min(tx, LANES))
    z, big = _in_proj(x, sh1, sc1, lp["g1"], lp["w_in"], lp["b_in"], lp["gains"], s64, s32, tril,
                      bb=bb, tt=tt, running=prompt, ml_blocks=ML_CHUNK // LANES,
                      layer=layer, depth=depth, prev_state=prev_state)
    r3 = lambda a: a.reshape(bx, tx, a.shape[-1])
    bias_p, bias_s = _band_bias(lp["table"])

    if prompt:
        tq = 512
        o_fox = _fox_prompt(r3(z["fqa"]), r3(z["fka"]), r3(z["fvb"]), tq=tq)
        o_diff = _diff_prompt(r3(z["dqv"]), r3(z["dkb"]), r3(z["dvb"]), lp["lam_p"], lp["gsub"],
                              tq=tq, lam_init=lp["lam_init"])
        o_band = _band_prompt(r3(z["bqm"]), r3(z["bkb"]), r3(z["bvb"]), bias_p, tq=tq)
        c0 = jnp.zeros((bx, GROUP, GROUP), F32)
        n0 = jnp.zeros((bx, 1, GROUP), F32)
        m0 = jnp.zeros((bx, 1, GROUP), F32)
        conv0 = jnp.zeros((bx, 8, 2 * GROUP), F32)
        ml_valid, ml_rows = ML_CHUNK, ML_CHUNK
    else:
        (c_fk, c_fv, c_flf, c_dk, c_dv, c_bk, c_bv, s_c, s_n, s_m, s_conv) = caches
        past = c_fk.shape[2]
        assert past % CHUNK == 0 and tx <= CHUNK
        jj = np.arange(past)
        lstr = jnp.asarray((jj[:, None] > jj[None, :]).astype(np.float32), dtype=BF16)
        o_fox = _fox_sample(r3(z["fqa"]), r3(z["fka"]), r3(z["fvb"]), c_fk, c_fv, c_flf, lstr)
        o_diff = _diff_sample(r3(z["dqv"]), r3(z["dkb"]), r3(z["dvb"]), c_dk, c_dv,
                              lp["lam_p"], lp["gsub"], lam_init=lp["lam_init"])
        o_band = _band_sample(r3(z["bqm"]), r3(z["bkb"]), r3(z["bvb"]), c_bk, c_bv, bias_s)
        eye = jnp.eye(N_HEADS, dtype=F32)
        c0 = (s_c[:, :, :, None, :] * eye[None, :, None, :, None]).reshape(bx, GROUP, GROUP)
        n0 = s_n.reshape(bx, 1, GROUP)
        m0 = jnp.repeat(s_m, HEAD_DIM, axis=-1).reshape(bx, 1, GROUP)
        conv0 = jnp.pad(s_conv, ((0, 0), (8 - (CONV_W - 1), 0), (0, 0)))
        ml_valid, ml_rows = tx, ML_PAD
    o_ml, c_new, n_new, m_new = _mlstm(
        z["mqk"], z["mvb"], z["gates"], z["cum"], z["mos"], lp["conv_w"], lp["conv_b"], lp["gmh"],
        s64, bd, c0, n0, m0, conv0, nb=bx, t=tx, valid=ml_valid, rows=ml_rows, ns=min(bx, ML_SEQS))

    flat2 = lambda a: a.reshape(n, GROUP)
    x1 = _out_proj(x, gt1, flat2(o_fox), flat2(o_diff), o_ml, flat2(o_band), lp["w_out"], bb=bb, tt=tt,
                   layer=layer)
    x2 = _ffn(x1, sh2, sc2, gt2, lp["g2"], lp["wg"], lp["wu"], lp["wd"], bb=bb, tt=tt, tf=D_FF // 2,
              layer=layer)

    mc = jnp.stack([c_new[:, HEAD_DIM * h:HEAD_DIM * h + HEAD_DIM, HEAD_DIM * h:HEAD_DIM * h + HEAD_DIM]
                    for h in range(N_HEADS)], axis=1)
    small = (r3(z["gates"])[:, :, 0:N_HEADS], mc, n_new.reshape(bx, N_HEADS, HEAD_DIM),
             m_new.reshape(bx, N_HEADS, HEAD_DIM)[:, :, 0], r3(z["mqk"])[:, tx - (CONV_W - 1):, :])
    return x2, big, small


def _channel_major(c):
    perm = (0, 1) + tuple(range(3, c.ndim)) + (2,)
    return jnp.transpose(c, perm).reshape(c.shape[0], c.shape[1], -1, c.shape[2])


def _token_major(a, inner):
    d, b, _, t = a.shape
    k = len(inner)
    return jnp.transpose(a.reshape(d, b, *inner, t), (0, 1, 2 + k) + tuple(range(2, 2 + k)))


def kernel(x_prompt, x_sample, c_prompt, c_sample, cache_fox_k, cache_fox_v, cache_fox_logf, cache_diff_k, cache_diff_v, cache_band_k, cache_band_v, state_mlstm_c, state_mlstm_n, state_mlstm_m, state_conv, norm1_g, norm2_g, w_mod, b_mod, w_in, b_in, qk_g_fox, qk_g_diff, qk_g_band, conv_w, conv_b, diff_lambda, diff_subln_g, mlstm_norm_g, band_rel_bias, w_out, w_ffn_gate, w_ffn_up, w_ffn_down):
    depth = w_in.shape[0]
    bp, bs = x_prompt.shape[0], x_sample.shape[0]
    ts = x_sample.shape[1]
    caches = tuple(_channel_major(c) for c in (cache_fox_k, cache_fox_v, cache_fox_logf, cache_diff_k,
                                               cache_diff_v, cache_band_k, cache_band_v))
    caches += (state_mlstm_c, state_mlstm_n, state_mlstm_m, state_conv)
    mod = _modulation(jnp.concatenate([c_prompt, c_sample], axis=0), w_mod, b_mod)
    mod = mod.reshape(depth, bp + bs, 6, D_MODEL)
    w_in_g = _regroup_w_in(w_in)
    big_w = [w.astype(BF16) for w in (w_out, w_ffn_gate, w_ffn_up, w_ffn_down)]
    layers = [_prep_layer(l, norm1_g, norm2_g, w_in_g, b_in, qk_g_fox, qk_g_diff, qk_g_band, conv_w,
                          conv_b, diff_lambda, diff_subln_g, mlstm_norm_g, band_rel_bias, *big_w)
              for l in range(depth)]
    sample_bb = 512 // ts
    xp, xs = x_prompt, x_sample
    p_big, p_small, s_big, s_small = None, [], [], []
    for l in range(depth):
        xp, p_big, sm = _layer(xp, mod[l, :bp], layers[l], None, bb=1, tt=512, layer=l, depth=depth,
                               prev_state=p_big)
        p_small.append(sm)
    for l in range(depth):
        xs, big, sm = _layer(xs, mod[l, bp:], layers[l], tuple(c[l] for c in caches), bb=sample_bb, tt=ts,
                             layer=l, depth=depth)
        s_big.append(big)
        s_small.append(sm)
    head = (N_HEADS, HEAD_DIM)
    half = (N_HEADS, 2, DIFF_HALF)
    inner = (head, head, half, head, head, head)
    p_fk, p_fv, p_dk, p_dv, p_bk, p_bv = [_token_major(a, inn) for a, inn in zip(p_big, inner)]
    p_flf, p_c, p_n, p_m, p_conv = [jnp.stack(zs) for zs in zip(*p_small)]
    s_fk, s_fv, s_dk, s_dv, s_bk, s_bv = [jnp.stack(zs).reshape((depth, bs, ts) + inn)
                                          for zs, inn in zip(zip(*s_big), inner)]
    s_flf, s_c, s_n, s_m, s_conv = [jnp.stack(zs) for zs in zip(*s_small)]
    return (xp, xs, p_fk, p_fv, p_flf, p_dk, p_dv, p_bk, p_bv, p_c, p_n, p_m, p_conv,
            s_fk, s_fv, s_flf, s_dk, s_dv, s_bk, s_bv, s_c, s_n, s_m, s_conv)
```

```python
import functools
import math

import numpy as np
import jax
import jax.numpy as jnp
from jax import lax
from jax.experimental import pallas as pl
from jax.experimental.pallas import tpu as pltpu

F32 = jnp.float32
BF16 = jnp.bfloat16

D_MODEL = 1024
HEAD_DIM = 64
N_HEADS = 4
GROUP = N_HEADS * HEAD_DIM
DIFF_HALF = HEAD_DIM // 2
CHUNK = 64
BAND_CHUNKS = 8
BAND_ROWS = BAND_CHUNKS * CHUNK
REL_CLIP = 128
CONV_W = 4
D_FF = 2816
EPS = 1e-6
NEG = -1e30
LOG2E = math.log2(math.e)

LANES = 128
PAIR = 2 * HEAD_DIM
N_GATES = 3 * N_HEADS
BAND_WIN = (BAND_CHUNKS + 2) * CHUNK
ML_CHUNK = 256
ML_PAD = 128
ML_SEQS = 4
BAND_LOOKAHEAD = 5
SAMPLE_LOOKAHEAD = 2
FFN_CHUNK = 256
VMEM_LIMIT = 56 * 1024 * 1024

IN_SPLIT_SIZES = (GROUP, GROUP, GROUP, N_HEADS, GROUP, GROUP, GROUP, 2 * GROUP, GROUP,
                  N_HEADS, N_HEADS, GROUP, GROUP, GROUP, GROUP)
_FULL_GROUPS = (0, 1, 2, 4, 5, 6, 7, 8, 11, 12, 13, 14)
_GATE_GROUPS = (3, 9, 10)
N_FULL = 13 * GROUP
N_IN_PAD = N_FULL + LANES


def _cparams(sem):
    return pltpu.CompilerParams(dimension_semantics=sem, vmem_limit_bytes=VMEM_LIMIT)


def _nt_dot(a, b):
    return lax.dot_general(a, b, (((1,), (1,)), ((), ())), preferred_element_type=F32)


def _dot(a, b):
    return jnp.dot(a, b, preferred_element_type=F32)


def _split3(x):
    hi = x.astype(BF16)
    r = x - hi.astype(F32)
    mid = r.astype(BF16)
    lo = (r - mid.astype(F32)).astype(BF16)
    return hi, mid, lo


def _log_sigmoid(x):
    return jnp.minimum(x, 0.0) - jnp.log1p(jnp.exp(-jnp.abs(x)))


def _mod_kernel(c_ref, w_ref, b_ref, o_ref):
    c = c_ref[...]
    a = (c * jax.nn.sigmoid(c)).astype(BF16)
    o_ref[0] = _dot(a, w_ref[0].astype(BF16)) + b_ref[0]


def _modulation(c_all, w_mod, b_mod):
    depth, d, n = w_mod.shape
    rows = c_all.shape[0]
    tn = 768
    return pl.pallas_call(
        _mod_kernel,
        grid=(depth, n // tn),
        in_specs=[pl.BlockSpec((rows, d), lambda l, j: (0, 0)),
                  pl.BlockSpec((1, d, tn), lambda l, j: (l, 0, j)),
                  pl.BlockSpec((1, 1, tn), lambda l, j: (l, 0, j))],
        out_specs=pl.BlockSpec((1, rows, tn), lambda l, j: (l, 0, j)),
        out_shape=jax.ShapeDtypeStruct((depth, rows, n), F32),
        compiler_params=_cparams(("parallel", "parallel")),
        name="modulation",
    )(c_all, w_mod, b_mod.reshape(depth, 1, n))


def _regroup_kernel(w_ref, o_ref):
    starts = np.concatenate([[0], np.cumsum(IN_SPLIT_SIZES)]).tolist()
    col = 0
    for g in _FULL_GROUPS + _GATE_GROUPS:
        width = IN_SPLIT_SIZES[g]
        o_ref[0, :, col:col + width] = w_ref[0, :, starts[g]:starts[g] + width].astype(BF16)
        col += width
    o_ref[0, :, col:N_IN_PAD] = jnp.zeros((o_ref.shape[1], N_IN_PAD - col), BF16)


def _regroup_w_in(w_in):
    depth, d, n_in = w_in.shape
    tr = 256
    return pl.pallas_call(
        _regroup_kernel,
        grid=(depth, d // tr),
        in_specs=[pl.BlockSpec((1, tr, n_in), lambda l, r: (l, r, 0))],
        out_specs=pl.BlockSpec((1, tr, N_IN_PAD), lambda l, r: (l, r, 0)),
        out_shape=jax.ShapeDtypeStruct((depth, d, N_IN_PAD), BF16),
        compiler_params=_cparams(("parallel", "parallel")),
        name="regroup_w_in",
    )(w_in)


_IN_OUTS = (
    ("fqa", 4 * LANES, BF16),
    ("fka", 4 * LANES, BF16),
    ("fvb", GROUP, BF16),
    ("dqv", 8 * LANES, BF16),
    ("dkb", GROUP, BF16), ("dvb", GROUP, BF16),
    ("mqk", 2 * GROUP, F32), ("mvb", GROUP, BF16), ("mos", GROUP, F32),
    ("bqm", 4 * LANES, BF16),
    ("bkb", GROUP, BF16), ("bvb", GROUP, BF16),
    ("gates", LANES, F32),
    ("cum", LANES, F32),
)
_IN_STATE = ("fk", "fv", "dk", "dv", "bk", "bv")
N_IN_ARGS = 10


def _in_proj_kernel(*refs, bb, tt, running, ml_blocks, n_prev):
    (x_ref, sh_ref, sc_ref, g1_ref, w_ref, b_ref, gains_ref, s64_ref, s32_ref, tril_ref) = refs[:N_IN_ARGS]
    outs = refs[N_IN_ARGS + n_prev:]
    (fqa_ref, fka_ref, fvb_ref, dqv_ref, dkb_ref, dvb_ref, mqk_ref, mvb_ref, mos_ref, bqm_ref, bkb_ref,
     bvb_ref, gates_ref, cum_ref) = outs[:len(_IN_OUTS)]
    state = dict(zip(_IN_STATE, outs[len(_IN_OUTS):]))
    carry_ref = outs[len(_IN_OUTS) + len(_IN_STATE)]
    tm = bb * tt

    def put_state(name, val):
        ref = state[name]
        if not running:
            ref[...] = val
        elif name in ("bk", "bv"):
            @pl.when(pl.program_id(1) == pl.num_programs(1) - 1)
            def _():
                ref[0, 0] = val.T
        else:
            ref[0, 0] = val.T

    x = x_ref[...]
    ms = jnp.mean(x * x, axis=-1, keepdims=True)
    h = x * lax.rsqrt(ms + EPS) * g1_ref[...]
    h = h * (1.0 + sc_ref[...]) + sh_ref[...]
    hb = h.reshape(tm, D_MODEL).astype(BF16)

    group_order = iter([13] + list(range(13)))
    pending = []

    def issue():
        g = next(group_order, None)
        if g is not None:
            c0, width = GROUP * g, (LANES if g == 13 else GROUP)
            pending.append((g, _dot(hb, w_ref[0, :, c0:c0 + width]) + b_ref[:, c0:c0 + width]))

    def proj(g, width=GROUP):
        got, z = pending.pop(0)
        assert got == g and z.shape[1] == width
        issue()
        return z

    issue()

    def rms_seg(z, s_ref, row, n):
        ss = _dot((z * z).astype(BF16), s_ref[...])
        return z * lax.rsqrt(ss * (1.0 / n) + EPS) * gains_ref[row:row + 1, :]

    lane = lax.broadcasted_iota(jnp.int32, (tm, LANES), 1)

    zg = proj(13, LANES)
    is_ls = (lane < N_HEADS) | ((lane >= 2 * N_HEADS) & (lane < N_GATES))
    gates = jnp.where(is_ls, _log_sigmoid(zg), zg)
    gates_ref[...] = gates
    ghi, gmid, glo = _split3(gates)
    tril = tril_ref[...]
    lane_b = lax.broadcasted_iota(jnp.int32, (LANES, LANES), 1)
    if running:
        @pl.when(pl.program_id(1) == 0)
        def _():
            carry_ref[...] = jnp.zeros_like(carry_ref)
        carry_a = carry_ref[0:1, :]
    carry_b = None
    cums = []
    for blk in range(tm // LANES):
        sl = slice(LANES * blk, LANES * blk + LANES)
        p = _dot(tril, ghi[sl]) + _dot(tril, gmid[sl]) + _dot(tril, glo[sl])
        if running:
            ca = p + carry_a
            carry_a = ca[LANES - 1:LANES, :]
            cb = p if blk % ml_blocks == 0 else p + carry_b
            carry_b = cb[LANES - 1:LANES, :]
            p = jnp.where(lane_b < N_HEADS, ca, cb)
        cums.append(p)
    if running:
        carry_ref[0:1, :] = carry_a
    cum = jnp.concatenate(cums, axis=0)
    cum_ref[...] = cum

    fqn = rms_seg(proj(0), s64_ref, 0, HEAD_DIM) * (HEAD_DIM ** -0.5 * LOG2E)
    fkn = rms_seg(proj(1), s64_ref, 1, HEAD_DIM)
    put_state("fk", fkn)
    fv = proj(2)
    put_state("fv", fv)
    fvb_ref[...] = fv.astype(BF16)
    ones_q = jnp.where((lane >= HEAD_DIM + 3) & (lane < HEAD_DIM + 6), 1.0, 0.0)
    ones_k = jnp.where((lane >= HEAD_DIM) & (lane < HEAD_DIM + 3), 1.0, 0.0)
    for hd in range(N_HEADS):
        pr, e = divmod(hd, 2)
        bq = fqn[:, LANES * pr:LANES * pr + LANES]
        bk = fkn[:, LANES * pr:LANES * pr + LANES]
        if e:
            bq = pltpu.roll(bq, HEAD_DIM, axis=1)
            bk = pltpu.roll(bk, HEAD_DIM, axis=1)
        cbc = jnp.broadcast_to(cum[:, hd:hd + 1], (tm, LANES)) * LOG2E
        chi = cbc.astype(BF16).astype(F32)
        r = cbc - chi
        cmid = r.astype(BF16).astype(F32)
        clo = (r - cmid).astype(BF16).astype(F32)
        aq = jnp.where(lane == HEAD_DIM, chi,
                       jnp.where(lane == HEAD_DIM + 1, cmid, jnp.where(lane == HEAD_DIM + 2, clo, ones_q)))
        ak = jnp.where(lane == HEAD_DIM + 3, -chi,
                       jnp.where(lane == HEAD_DIM + 4, -cmid, jnp.where(lane == HEAD_DIM + 5, -clo, ones_k)))
        fqa_ref[:, LANES * hd:LANES * hd + LANES] = jnp.where(lane < HEAD_DIM, bq, aq).astype(BF16)
        fka_ref[:, LANES * hd:LANES * hd + LANES] = jnp.where(lane < HEAD_DIM, bk, ak).astype(BF16)

    dqn = rms_seg(proj(3), s32_ref, 2, DIFF_HALF) * (DIFF_HALF ** -0.5 * LOG2E)
    dkn = rms_seg(proj(4), s32_ref, 3, DIFF_HALF)
    put_state("dk", dkn)
    dkb_ref[...] = dkn.astype(BF16)
    dv = proj(5)
    put_state("dv", dv)
    dvb_ref[...] = dv.astype(BF16)
    for pr in range(2):
        blk = dqn[:, LANES * pr:LANES * pr + LANES]
        for e in range(2):
            for jm in range(2):
                l0 = HEAD_DIM * e + DIFF_HALF * jm
                idx = 4 * pr + 2 * e + jm
                dqv_ref[:, LANES * idx:LANES * idx + LANES] = jnp.where(
                    (lane >= l0) & (lane < l0 + DIFF_HALF), blk, 0.0).astype(BF16)

    mqk_ref[:, 0:GROUP] = proj(6)
    mqk_ref[:, GROUP:2 * GROUP] = proj(7)
    mvb_ref[...] = proj(8).astype(BF16)
    mos_ref[...] = jax.nn.sigmoid(proj(9))

    bqn = rms_seg(proj(10), s64_ref, 4, HEAD_DIM) * (HEAD_DIM ** -0.5 * LOG2E)
    bkn = rms_seg(proj(11), s64_ref, 5, HEAD_DIM)
    put_state("bk", bkn)
    bkb_ref[...] = bkn.astype(BF16)
    bv = proj(12)
    put_state("bv", bv)
    bvb_ref[...] = bv.astype(BF16)
    for hd in range(N_HEADS):
        pr, e = divmod(hd, 2)
        blk = bqn[:, LANES * pr:LANES * pr + LANES]
        bqm_ref[:, LANES * hd:LANES * hd + LANES] = jnp.where(
            (lane >= HEAD_DIM * e) & (lane < HEAD_DIM * e + HEAD_DIM), blk, 0.0).astype(BF16)


def _in_proj(x, sh, sc, g1, w, b, gains, s64, s32, tril, *, bb, tt, running, ml_blocks,
             layer=0, depth=1, prev_state=None):
    bx, tx, d = x.shape
    n = bx * tx
    nb, nt = bx // bb, tx // tt
    tm = bb * tt
    const = lambda i, t: (0, 0)
    tok = lambda i, t: (i * nt + t, 0)
    keep = min(BAND_ROWS, tx)
    if running:
        assert bb == 1 and keep == tm
        st_shapes = [(depth, bx, GROUP, tx)] * 4 + [(depth, bx, GROUP, keep)] * 2
        st_specs = ([pl.BlockSpec((1, 1, GROUP, tm), lambda i, t: (layer, i, 0, t))] * 4
                    + [pl.BlockSpec((1, 1, GROUP, keep), lambda i, t: (layer, i, 0, 0))] * 2)
    else:
        st_shapes = [(n, GROUP)] * 6
        st_specs = [pl.BlockSpec((tm, GROUP), tok)] * 6
    prev = list(prev_state) if prev_state is not None else []
    aliases = {N_IN_ARGS + k: len(_IN_OUTS) + k for k in range(len(prev))}
    in_specs = [
        pl.BlockSpec((bb, tt, d), lambda i, t: (i, t, 0)),
        pl.BlockSpec((bb, 1, d), lambda i, t: (i, 0, 0)),
        pl.BlockSpec((bb, 1, d), lambda i, t: (i, 0, 0)),
        pl.BlockSpec((1, d), const),
        pl.BlockSpec((1, d, N_IN_PAD), lambda i, t: (layer, 0, 0), pipeline_mode=pl.Buffered(1)),
        pl.BlockSpec((1, N_IN_PAD), const),
        pl.BlockSpec((8, GROUP), const),
        pl.BlockSpec((GROUP, GROUP), const),
        pl.BlockSpec((GROUP, GROUP), const),
        pl.BlockSpec((LANES, LANES), const),
    ]
    in_specs += [pl.BlockSpec(memory_space=pl.ANY)] * len(prev)
    out_specs = [pl.BlockSpec((tm, wd), tok) for _, wd, _ in _IN_OUTS] + st_specs
    out_shape = ([jax.ShapeDtypeStruct((n, wd), dt) for _, wd, dt in _IN_OUTS]
                 + [jax.ShapeDtypeStruct(s, F32) for s in st_shapes])
    outs = pl.pallas_call(
        functools.partial(_in_proj_kernel, bb=bb, tt=tt, running=running, ml_blocks=ml_blocks,
                          n_prev=len(prev)),
        grid=(nb, nt),
        in_specs=in_specs,
        out_specs=out_specs,
        out_shape=out_shape,
        scratch_shapes=[pltpu.VMEM((8, LANES), F32)],
        input_output_aliases=aliases,
        compiler_params=_cparams(("arbitrary", "arbitrary")),
        name="in_proj",
    )(x, sh, sc, g1, w, b, gains, s64, s32, tril, *prev)
    ops = {name: o for (name, _, _), o in zip(_IN_OUTS, outs)}
    return ops, list(outs[len(_IN_OUTS):])


def _softmax_step(s, v, m_ref, l_ref, acc_ref, c, shift=None):
    nt = s.shape[1] // LANES
    tiles = [s[:, LANES * t:LANES * t + LANES] for t in range(nt)]
    m_prev = m_ref[c]
    m_cur = jnp.max(s, axis=-1, keepdims=True)
    if shift is not None:
        m_cur = m_cur - shift
    m_new = jnp.maximum(m_prev, m_cur)
    alpha = jnp.exp2(m_prev - m_new)
    m_sub = m_new if shift is None else m_new + shift
    ps = [jnp.exp2(t - m_sub) for t in tiles]
    psum = ps[0]
    for p in ps[1:]:
        psum = psum + p
    l_ref[c] = alpha * l_ref[c] + psum
    p = jnp.concatenate([p.astype(BF16) for p in ps], axis=1)
    acc_ref[c] = alpha * acc_ref[c] + _dot(p, v)
    m_ref[c] = m_new


def _pipelined(n, make_s, consume, depth=1):
    queue = [make_s(c) for c in range(min(depth, n))]
    for c in range(n):
        if c + depth < n:
            queue.append(make_s(c + depth))
        consume(c, queue.pop(0))


def _row_sum(l):
    return jnp.sum(l, axis=-1, keepdims=True)


def _causal_pairs(nq):
    qi = np.array([i for i in range(nq) for _ in range(i + 1)], np.int32)
    kj = np.array([j for i in range(nq) for j in range(i + 1)], np.int32)
    return jnp.asarray(qi), jnp.asarray(kj)


def _init_softmax_state(m_ref, l_ref, acc_ref):
    m_ref[...] = jnp.full_like(m_ref, NEG)
    l_ref[...] = jnp.zeros_like(l_ref)
    acc_ref[...] = jnp.zeros_like(acc_ref)


def _fox_kernel(qi_ref, kj_ref, q_ref, k_ref, v_ref, o_ref, m_ref, l_ref, acc_ref, *, tq):
    p = pl.program_id(1)
    i = qi_ref[p]
    j = kj_ref[p]

    @pl.when(j == 0)
    def _():
        _init_softmax_state(m_ref, l_ref, acc_ref)

    def step(diag):
        q = q_ref[0]
        k = k_ref[0]
        v = v_ref[0]
        if diag:
            row = lax.broadcasted_iota(jnp.int32, (tq, tq), 0)
            col = lax.broadcasted_iota(jnp.int32, (tq, tq), 1)
            keep = col <= row

        def make_s(hd):
            s = _nt_dot(q[:, LANES * hd:LANES * hd + LANES], k[:, LANES * hd:LANES * hd + LANES])
            return jnp.where(keep, s, NEG) if diag else s

        def consume(hd, s):
            pr = hd // 2
            _softmax_step(s, v[:, LANES * pr:LANES * pr + LANES], m_ref, l_ref, acc_ref, hd)

        _pipelined(N_HEADS, make_s, consume)

    @pl.when(j < i)
    def _():
        step(False)

    @pl.when(j == i)
    def _():
        step(True)
        lane = lax.broadcasted_iota(jnp.int32, (tq, LANES), 1)
        for pr in range(2):
            oe = acc_ref[2 * pr] / _row_sum(l_ref[2 * pr])
            oo = acc_ref[2 * pr + 1] / _row_sum(l_ref[2 * pr + 1])
            o_ref[0, :, LANES * pr:LANES * pr + LANES] = jnp.where(lane < HEAD_DIM, oe, oo).astype(BF16)


def _fox_prompt(qa, ka, vb, *, tq):
    b, t, _ = qa.shape
    qi, kj = _causal_pairs(t // tq)
    q_map = lambda bi, p, qi, kj: (bi, qi[p], 0)
    k_map = lambda bi, p, qi, kj: (bi, kj[p], 0)
    return pl.pallas_call(
        functools.partial(_fox_kernel, tq=tq),
        grid_spec=pltpu.PrefetchScalarGridSpec(
            num_scalar_prefetch=2,
            grid=(b, qi.shape[0]),
            in_specs=[pl.BlockSpec((1, tq, 4 * LANES), q_map), pl.BlockSpec((1, tq, 4 * LANES), k_map),
                      pl.BlockSpec((1, tq, GROUP), k_map)],
            out_specs=pl.BlockSpec((1, tq, GROUP), q_map),
            scratch_shapes=[pltpu.VMEM((N_HEADS, tq, LANES), F32), pltpu.VMEM((N_HEADS, tq, LANES), F32),
                            pltpu.VMEM((N_HEADS, tq, LANES), F32)]),
        out_shape=jax.ShapeDtypeStruct((b, t, GROUP), BF16),
        compiler_params=_cparams(("parallel", "arbitrary")),
        name="fox_prompt",
    )(qi, kj, qa, ka, vb)


def _diff_lambda(lp, lam_init):
    a = jnp.sum(lp[0:1, :] * lp[1:2, :], axis=-1, keepdims=True)
    b = jnp.sum(lp[2:3, :] * lp[3:4, :], axis=-1, keepdims=True)
    return jnp.exp(a) - jnp.exp(b) + lam_init


def _diff_finish(acc_ref, l_ref, lam, gsub, lam_init, rows, o_ref):
    lane = lax.broadcasted_iota(jnp.int32, (rows, LANES), 1)
    for pr in range(2):
        outs = []
        for e in range(2):
            hd = 2 * pr + e
            o = (acc_ref[2 * hd] / _row_sum(l_ref[2 * hd])
                 - lam * (acc_ref[2 * hd + 1] / _row_sum(l_ref[2 * hd + 1])))
            valid = (lane >= HEAD_DIM * e) & (lane < HEAD_DIM * e + HEAD_DIM)
            ms = jnp.sum(jnp.where(valid, o * o, 0.0), axis=-1, keepdims=True) * (1.0 / HEAD_DIM)
            outs.append(o * lax.rsqrt(ms + EPS) * gsub * (1.0 - lam_init))
        o_ref[0, :, LANES * pr:LANES * pr + LANES] = jnp.where(lane < HEAD_DIM, outs[0], outs[1]).astype(BF16)


def _alibi_slope(hd):
    return 2.0 ** (-8.0 * (hd + 1) / N_HEADS) * LOG2E


def _alibi_features(tq):
    pos = np.arange(tq, dtype=np.float32)[:, None]
    lane = np.arange(LANES)[None, :]
    rnd = lambda v: v.astype(BF16).astype(np.float32)

    def split(v):
        v = v.astype(np.float32)
        hi = rnd(v)
        mid = rnd(v - hi)
        lo = rnd(v - hi - mid)
        return hi, mid, lo

    def block(e, first, second):
        out = np.zeros((tq, LANES), np.float32)
        for n, val in enumerate(first + second):
            out = np.where(lane == 8 * e + n, val, out)
        return out

    one = (np.ones((tq, 1), np.float32),) * 3
    qa = [block(hd % 2, split(-_alibi_slope(hd) * pos), one) for hd in range(N_HEADS)]
    ka = [block(0, one, split(_alibi_slope(2 * pr) * pos)) + block(1, one, split(_alibi_slope(2 * pr + 1) * pos))
          for pr in range(2)]
    return (jnp.asarray(np.concatenate(qa, axis=1), dtype=BF16),
            jnp.asarray(np.concatenate(ka, axis=1), dtype=BF16))


def _diff_kernel(qi_ref, kj_ref, q_ref, k_ref, v_ref, qa_ref, ka_ref, lam_ref, gsub_ref, o_ref,
                 m_ref, l_ref, acc_ref, *, tq, lam_init):
    p = pl.program_id(1)
    i = qi_ref[p]
    j = kj_ref[p]

    @pl.when(j == 0)
    def _():
        _init_softmax_state(m_ref, l_ref, acc_ref)

    def step(diag):
        q = q_ref[0]
        k = k_ref[0]
        v = v_ref[0]
        if diag:
            row = lax.broadcasted_iota(jnp.int32, (tq, tq), 0)
            col = lax.broadcasted_iota(jnp.int32, (tq, tq), 1)
            seen = (col // CHUNK) <= (row // CHUNK)
            ahead = jnp.maximum(col - row, 0).astype(F32)
        else:
            tile_gap = ((i - j) * tq).astype(F32)

        def make_s(c):
            hd, jm = divmod(c, 2)
            pr, e = divmod(hd, 2)
            idx = 4 * pr + 2 * e + jm
            qx = jnp.concatenate([q[:, LANES * idx:LANES * idx + LANES], qa_ref[:, LANES * hd:LANES * hd + LANES]],
                                 axis=1)
            kx = jnp.concatenate([k[:, LANES * pr:LANES * pr + LANES], ka_ref[:, LANES * pr:LANES * pr + LANES]],
                                 axis=1)
            s = _nt_dot(qx, kx)
            if diag:
                s = jnp.where(seen, s - (2.0 * _alibi_slope(hd)) * ahead, NEG)
            return s

        def consume(c, s):
            pr = c // 4
            shift = None if diag else _alibi_slope(c // 2) * tile_gap
            _softmax_step(s, v[:, LANES * pr:LANES * pr + LANES], m_ref, l_ref, acc_ref, c, shift)

        _pipelined(2 * N_HEADS, make_s, consume)

    @pl.when(j < i)
    def _():
        step(False)

    @pl.when(j == i)
    def _():
        step(True)
        lam = _diff_lambda(lam_ref[...], lam_init)
        _diff_finish(acc_ref, l_ref, lam, gsub_ref[...], lam_init, tq, o_ref)


def _diff_prompt(qv, kb, vb, lam_p, gsub, *, tq, lam_init):
    b, t, _ = qv.shape
    qi, kj = _causal_pairs(t // tq)
    qa, ka = _alibi_features(tq)
    q_map = lambda bi, p, qi, kj: (bi, qi[p], 0)
    k_map = lambda bi, p, qi, kj: (bi, kj[p], 0)
    const = lambda bi, p, qi, kj: (0, 0)
    return pl.pallas_call(
        functools.partial(_diff_kernel, tq=tq, lam_init=lam_init),
        grid_spec=pltpu.PrefetchScalarGridSpec(
            num_scalar_prefetch=2,
            grid=(b, qi.shape[0]),
            in_specs=[pl.BlockSpec((1, tq, 8 * LANES), q_map), pl.BlockSpec((1, tq, GROUP), k_map),
                      pl.BlockSpec((1, tq, GROUP), k_map),
                      pl.BlockSpec((tq, N_HEADS * LANES), const), pl.BlockSpec((tq, 2 * LANES), const),
                      pl.BlockSpec((4, DIFF_HALF), const), pl.BlockSpec((1, LANES), const)],
            out_specs=pl.BlockSpec((1, tq, GROUP), q_map),
            scratch_shapes=[pltpu.VMEM((2 * N_HEADS, tq, LANES), F32),
                            pltpu.VMEM((2 * N_HEADS, tq, LANES), F32),
                            pltpu.VMEM((2 * N_HEADS, tq, LANES), F32)]),
        out_shape=jax.ShapeDtypeStruct((b, t, GROUP), BF16),
        compiler_params=_cparams(("parallel", "arbitrary")),
        name="diff_prompt",
    )(qi, kj, qv, kb, vb, qa, ka, lam_p, gsub)


def _band_bias_kernel(g_ref, bp_ref, bs_ref):
    lane = lax.broadcasted_iota(jnp.int32, (CHUNK, BAND_WIN), 1)
    width = g_ref.shape[1]
    for hd in range(N_HEADS):
        x = jnp.broadcast_to(g_ref[hd:hd + 1, :], (CHUNK, width))
        rp = pltpu.roll(x, width - CHUNK + 1, axis=1, stride=1, stride_axis=0)
        bp_ref[hd] = jnp.where(lane < CHUNK, NEG, rp[:, :BAND_WIN] * LOG2E)
        rs = pltpu.roll(x, width - 2 * CHUNK + 1, axis=1, stride=1, stride_axis=0)
        bs_ref[hd] = rs[:, :BAND_WIN] * LOG2E


def _band_bias(table):
    gv = jnp.concatenate([jnp.broadcast_to(table[:, 2 * REL_CLIP:], (N_HEADS, 4 * LANES)),
                          table[:, 2 * REL_CLIP - 1::-1]], axis=1)
    shp = jax.ShapeDtypeStruct((N_HEADS, CHUNK, BAND_WIN), F32)
    return pl.pallas_call(_band_bias_kernel, out_shape=(shp, shp), name="band_bias")(gv)


def _band_kernel(q_ref, k_ref, v_ref, bias_ref, o_ref, kpad_ref, vpad_ref, *, tq):
    t = pl.program_id(1)
    n_sub = tq // CHUNK
    lane = lax.broadcasted_iota(jnp.int32, (CHUNK, LANES), 1)
    col = lax.broadcasted_iota(jnp.int32, (CHUNK, BAND_WIN), 1)

    @pl.when(t == 0)
    def _():
        for src, dst in ((k_ref, kpad_ref), (v_ref, vpad_ref)):
            dst[0:BAND_WIN, :] = jnp.zeros((BAND_WIN, GROUP), BF16)
            dst[BAND_WIN:, :] = src[0]

    def window(ref, c, pr):
        start = pl.multiple_of((t * n_sub + c + 1) * CHUNK, CHUNK)
        return ref[pl.ds(start, BAND_WIN), LANES * pr:LANES * pr + LANES]

    def make_s(idx):
        c, hd = divmod(idx, N_HEADS)
        q = q_ref[0, CHUNK * c:CHUNK * c + CHUNK, LANES * hd:LANES * hd + LANES]
        valid = col + (t * n_sub + c - BAND_CHUNKS - 1) * CHUNK >= 0
        return jnp.where(valid, _nt_dot(q, window(kpad_ref, c, hd // 2)) + bias_ref[hd], NEG)

    even = {}

    def consume(idx, s):
        c, hd = divmod(idx, N_HEADS)
        pr, e = divmod(hd, 2)
        m = jnp.max(s, axis=-1, keepdims=True)
        p = jnp.exp2(s - m)
        l = jnp.sum(p, axis=-1, keepdims=True)
        o = _dot(p.astype(BF16), window(vpad_ref, c, pr)) / l
        if e == 0:
            even[pr] = o
        else:
            o_ref[0, CHUNK * c:CHUNK * c + CHUNK, LANES * pr:LANES * pr + LANES] = jnp.where(
                lane < HEAD_DIM, even[pr], o).astype(BF16)

    _pipelined(n_sub * N_HEADS, make_s, consume, depth=BAND_LOOKAHEAD)


def _band_prompt(qm, kb, vb, bias, *, tq):
    b, t, _ = qm.shape
    return pl.pallas_call(
        functools.partial(_band_kernel, tq=tq),
        grid=(b, t // tq),
        in_specs=[pl.BlockSpec((1, tq, 4 * LANES), lambda bi, i: (bi, i, 0)),
                  pl.BlockSpec((1, t, GROUP), lambda bi, i: (bi, 0, 0)),
                  pl.BlockSpec((1, t, GROUP), lambda bi, i: (bi, 0, 0)),
                  pl.BlockSpec((N_HEADS, CHUNK, BAND_WIN), lambda bi, i: (0, 0, 0))],
        out_specs=pl.BlockSpec((1, tq, GROUP), lambda bi, i: (bi, i, 0)),
        out_shape=jax.ShapeDtypeStruct((b, t, GROUP), BF16),
        scratch_shapes=[pltpu.VMEM((BAND_WIN + t, GROUP), BF16), pltpu.VMEM((BAND_WIN + t, GROUP), BF16)],
        compiler_params=_cparams(("arbitrary", "arbitrary")),
        name="band_prompt",
    )(qm, kb, vb, bias)


def _mlstm_kernel(mqk_ref, mv_ref, gates_ref, cum_ref, mos_ref, cw_ref, cb_ref, gmh_ref, shead_ref, bd_ref,
                  c0_ref, n0_ref, m0_ref, conv0_ref,
                  o_ref, c_out_ref, n_out_ref, m_out_ref,
                  cbuf_ref, c_ref, n_ref, m_ref, *, rows, valid, ns):
    t = pl.program_id(1)

    @pl.when(t == 0)
    def _():
        for s in range(ns):
            c_ref[s] = c0_ref[s]
            n_ref[s] = n0_ref[s]
            m_ref[s] = m0_ref[s]
            cbuf_ref[s, 0:8, :] = conv0_ref[s]

    for s in range(ns):
        _mlstm_chunk(mqk_ref.at[s], mv_ref.at[s], gates_ref.at[s], cum_ref.at[s], mos_ref.at[s], cw_ref, cb_ref,
                     gmh_ref, shead_ref, bd_ref, o_ref.at[s], cbuf_ref.at[s], c_ref.at[s], n_ref.at[s],
                     m_ref.at[s], rows=rows, valid=valid)

    @pl.when(t == pl.num_programs(1) - 1)
    def _():
        for s in range(ns):
            c_out_ref[s] = c_ref[s]
            n_out_ref[s] = n_ref[s]
            m_out_ref[s] = m_ref[s]


def _mlstm_chunk(mqk_ref, mv_ref, gates_ref, cum_ref, mos_ref, cw_ref, cb_ref, gmh_ref, shead_ref, bd_ref,
                 o_ref, cbuf_ref, c_ref, n_ref, m_ref, *, rows, valid):
    def padded(a, fill=0.0):
        if valid == rows:
            return a
        return jnp.concatenate([a, jnp.full((rows - valid, a.shape[1]), fill, a.dtype)], axis=0)

    u = padded(mqk_ref[...])
    cbuf_ref[8:8 + rows, :] = u
    y = cb_ref[...] + cw_ref[3:4, :] * u
    for jw in range(CONV_W - 1):
        y = y + cw_ref[jw:jw + 1, :] * cbuf_ref[5 + jw:5 + jw + rows, :]
    cbuf_ref[0:8, :] = cbuf_ref[valid:valid + 8, :]
    qk = y * jax.nn.sigmoid(y)
    q = qk[:, 0:GROUP]
    k = qk[:, GROUP:2 * GROUP] * (HEAD_DIM ** -0.5)
    qb = q.astype(BF16)
    kb = k.astype(BF16)
    v = padded(mv_ref[...])
    mos = padded(mos_ref[...])

    g = gates_ref[...]
    cm = cum_ref[...]
    if valid != rows:
        g = padded(g, NEG)
        cm = jnp.concatenate([cm, jnp.broadcast_to(cm[valid - 1:valid, :], (rows - valid, LANES))], axis=0)
    g_t = g.T
    cm_t = cm.T

    row = lax.broadcasted_iota(jnp.int32, (rows, rows), 0)
    col = lax.broadcasted_iota(jnp.int32, (rows, rows), 1)
    causal = col <= row
    lane_g = lax.broadcasted_iota(jnp.int32, (rows, GROUP), 1)
    lane_r = lax.broadcasted_iota(jnp.int32, (1, GROUP), 1)
    m_prev_all = m_ref[...]

    zeros = jnp.zeros((rows, GROUP), F32)
    num, a_full, ws_full, mt_full, wend_full = zeros, zeros, zeros, zeros, zeros
    decay_lane = jnp.zeros((1, GROUP), F32)
    mnew_lane = jnp.zeros((1, GROUP), F32)
    head_masks = [(lane_g >= HEAD_DIM * hd) & (lane_g < HEAD_DIM * hd + HEAD_DIM) for hd in range(N_HEADS)]
    sqks = [_nt_dot(jnp.where(hm, q, 0.0).astype(BF16), kb) for hm in head_masks]
    c_old = c_ref[...]
    n_old = n_ref[...]
    shead = shead_ref[...]
    q_c = _dot(qb, c_old.astype(BF16))
    q_n = _dot((q * n_old).astype(BF16), shead)
    for hd in range(N_HEADS):
        hm = head_masks[hd]
        hm_r = (lane_r >= HEAD_DIM * hd) & (lane_r < HEAD_DIM * hd + HEAD_DIM)
        f_c = cm[:, 2 * N_HEADS + hd:2 * N_HEADS + hd + 1]
        ig_c = g[:, N_HEADS + hd:N_HEADS + hd + 1]
        f_r = cm_t[2 * N_HEADS + hd:2 * N_HEADS + hd + 1, :]
        ig_r = g_t[N_HEADS + hd:N_HEADS + hd + 1, :]
        m_prev = m_prev_all[:, HEAD_DIM * hd:HEAD_DIM * hd + 1]
        logw = jnp.where(causal, f_c - (f_r - ig_r), NEG)
        logb = f_c + m_prev
        m_t = jnp.maximum(logb, jnp.max(logw, axis=-1, keepdims=True))
        w = jnp.exp(logw - m_t) * sqks[hd]
        a = jnp.exp(logb - m_t)
        num = jnp.where(hm, _dot(w.astype(BF16), v), num)
        a_full = jnp.where(hm, a, a_full)
        ws_full = jnp.where(hm, jnp.sum(w, axis=-1, keepdims=True), ws_full)
        mt_full = jnp.where(hm, m_t, mt_full)
        f_end = f_c[rows - 1:rows, :]
        log_end = f_end - f_c + ig_c
        m_new = jnp.maximum(f_end + m_prev, jnp.max(log_end, axis=0, keepdims=True))
        wend_full = jnp.where(hm, jnp.exp(log_end - m_new), wend_full)
        decay_lane = jnp.where(hm_r, jnp.exp(f_end + m_prev - m_new), decay_lane)
        mnew_lane = jnp.where(hm_r, m_new, mnew_lane)

    num = num + a_full * q_c
    den = ws_full + a_full * q_n
    h = num / jnp.maximum(jnp.abs(den), jnp.exp(-mt_full))
    o = mos * h
    ss = _dot((o * o).astype(BF16), shead)
    on = o * lax.rsqrt(ss * (1.0 / HEAD_DIM) + EPS) * gmh_ref[...]
    o_ref[...] = on[0:valid, :].astype(BF16)

    wk = k * wend_full
    kv = lax.dot_general(wk.astype(BF16), v, (((0,), (0,)), ((), ())), preferred_element_type=F32)
    c_ref[...] = c_old * decay_lane + jnp.where(bd_ref[...] > 0.5, kv, 0.0)
    n_ref[...] = n_old * decay_lane + jnp.sum(wk, axis=0, keepdims=True)
    m_ref[...] = mnew_lane


def _mlstm(mqk, mvb, gates, cum, mos, cw, cb, gmh, shead, bd, c0, n0, m0, conv0, *, nb, t, valid, rows, ns):
    nc = t // valid
    seq = lambda a: a.reshape(nb, t, a.shape[-1])
    tok = lambda bi, i: (bi, i, 0)
    const = lambda bi, i: (0, 0)
    per_b = lambda bi, i: (bi, 0, 0)
    o, c_new, n_new, m_new = pl.pallas_call(
        functools.partial(_mlstm_kernel, rows=rows, valid=valid, ns=ns),
        grid=(nb // ns, nc),
        in_specs=[pl.BlockSpec((ns, valid, 2 * GROUP), tok), pl.BlockSpec((ns, valid, GROUP), tok),
                  pl.BlockSpec((ns, valid, LANES), tok), pl.BlockSpec((ns, valid, LANES), tok),
                  pl.BlockSpec((ns, valid, GROUP), tok),
                  pl.BlockSpec((CONV_W, 2 * GROUP), const), pl.BlockSpec((1, 2 * GROUP), const),
                  pl.BlockSpec((1, GROUP), const), pl.BlockSpec((GROUP, GROUP), const),
                  pl.BlockSpec((GROUP, GROUP), const),
                  pl.BlockSpec((ns, GROUP, GROUP), per_b), pl.BlockSpec((ns, 1, GROUP), per_b),
                  pl.BlockSpec((ns, 1, GROUP), per_b), pl.BlockSpec((ns, 8, 2 * GROUP), per_b)],
        out_specs=[pl.BlockSpec((ns, valid, GROUP), tok), pl.BlockSpec((ns, GROUP, GROUP), per_b),
                   pl.BlockSpec((ns, 1, GROUP), per_b), pl.BlockSpec((ns, 1, GROUP), per_b)],
        out_shape=[jax.ShapeDtypeStruct((nb, t, GROUP), BF16), jax.ShapeDtypeStruct((nb, GROUP, GROUP), F32),
                   jax.ShapeDtypeStruct((nb, 1, GROUP), F32), jax.ShapeDtypeStruct((nb, 1, GROUP), F32)],
        scratch_shapes=[pltpu.VMEM((ns, 8 + rows, 2 * GROUP), F32), pltpu.VMEM((ns, GROUP, GROUP), F32),
                        pltpu.VMEM((ns, 1, GROUP), F32), pltpu.VMEM((ns, 1, GROUP), F32)],
        compiler_params=_cparams(("parallel", "arbitrary")),
        name="mlstm",
    )(seq(mqk), seq(mvb), seq(gates), seq(cum), seq(mos), cw, cb, gmh, shead, bd, c0, n0, m0, conv0)
    return o.reshape(nb * t, GROUP), c_new, n_new, m_new


def _pad_rows(a, rows):
    return jnp.concatenate([a, jnp.zeros((rows - a.shape[0], a.shape[1]), a.dtype)], axis=0)


def _two_part_softmax(s_c, s_n, vt_c, v_n):
    m = jnp.maximum(jnp.max(s_c, axis=-1, keepdims=True), jnp.max(s_n, axis=-1, keepdims=True))
    p_c = jnp.exp2(s_c - m)
    p_n = jnp.exp2(s_n - m)
    l = jnp.sum(p_c, axis=-1, keepdims=True) + jnp.sum(p_n, axis=-1, keepdims=True)
    return _nt_dot(p_c.astype(BF16), vt_c) + _dot(p_n.astype(BF16), v_n), l


def _fox_sample_kernel(q_ref, kn_ref, vn_ref, ck_ref, cv_ref, clf_ref, lstr_ref, o_ref, *, tn, past):
    q = q_ref[0]
    kn = _pad_rows(kn_ref[0], LANES)
    vn = _pad_rows(vn_ref[0], LANES)
    ck = ck_ref[0, 0]
    cv = cv_ref[0, 0].astype(BF16)
    x = _pad_rows(clf_ref[0, 0], 8)
    xhi, xmid, xlo = _split3(x)
    lstr = lstr_ref[...]
    suf = _dot(xhi, lstr) + _dot(xmid, lstr) + _dot(xlo, lstr)

    sub = lax.broadcasted_iota(jnp.int32, (8, past), 0)
    lane_o = lax.broadcasted_iota(jnp.int32, (tn, LANES), 1)
    row = lax.broadcasted_iota(jnp.int32, (tn, LANES), 0)
    causal = lane_o <= row
    fill = jnp.zeros((LANES - HEAD_DIM - 8, past), F32)

    def logits(hd):
        sb = jnp.broadcast_to(suf[hd:hd + 1, :], (8, past)) * LOG2E
        shi = sb.astype(BF16).astype(F32)
        r = sb - shi
        smid = r.astype(BF16).astype(F32)
        slo = (r - smid).astype(BF16).astype(F32)
        aug = jnp.where(sub < 3, 1.0,
                        jnp.where(sub == 3, shi, jnp.where(sub == 4, smid, jnp.where(sub == 5, slo, 0.0))))
        kc = jnp.concatenate([ck[HEAD_DIM * hd:HEAD_DIM * hd + HEAD_DIM, :], aug, fill], axis=0).astype(BF16)
        qh = q[:, LANES * hd:LANES * hd + LANES]
        return _dot(qh, kc), jnp.where(causal, _nt_dot(qh, kn[:, LANES * hd:LANES * hd + LANES]), NEG)

    even = {}

    def attend(hd, s):
        pr, e = divmod(hd, 2)
        acc, l = _two_part_softmax(s[0], s[1], cv[LANES * pr:LANES * pr + LANES, :],
                                   vn[:, LANES * pr:LANES * pr + LANES])
        if e == 0:
            even[pr] = acc / l
        else:
            o_ref[0, :, LANES * pr:LANES * pr + LANES] = jnp.where(lane_o < HEAD_DIM, even[pr], acc / l).astype(BF16)

    _pipelined(N_HEADS, logits, attend, depth=SAMPLE_LOOKAHEAD)


def _fox_sample(qa, ka, vb, ck, cv, clf, lstr, *, layer):
    b, tn, _ = qa.shape
    past = ck.shape[3]
    per_b = lambda i: (i, 0, 0)
    cache_b = lambda i: (layer, i, 0, 0)
    return pl.pallas_call(
        functools.partial(_fox_sample_kernel, tn=tn, past=past),
        grid=(b,),
        in_specs=[pl.BlockSpec((1, tn, 4 * LANES), per_b), pl.BlockSpec((1, tn, 4 * LANES), per_b),
                  pl.BlockSpec((1, tn, GROUP), per_b), pl.BlockSpec((1, 1, GROUP, past), cache_b),
                  pl.BlockSpec((1, 1, GROUP, past), cache_b), pl.BlockSpec((1, 1, N_HEADS, past), cache_b),
                  pl.BlockSpec((past, past), lambda i: (0, 0))],
        out_specs=pl.BlockSpec((1, tn, GROUP), per_b),
        out_shape=jax.ShapeDtypeStruct((b, tn, GROUP), BF16),
        compiler_params=_cparams(("parallel",)),
        name="fox_sample",
    )(qa, ka, vb, ck, cv, clf, lstr)


def _diff_sample_kernel(q_ref, kn_ref, vn_ref, ck_ref, cv_ref, lam_ref, gsub_ref, o_ref, l_ref, acc_ref,
                        *, tn, past, lam_init):
    q = q_ref[0]
    kn = _pad_rows(kn_ref[0], LANES)
    vn = _pad_rows(vn_ref[0], LANES)
    ck = ck_ref[0, 0].astype(BF16)
    cv = cv_ref[0, 0].astype(BF16)
    row_c = lax.broadcasted_iota(jnp.int32, (tn, past), 0)
    col_c = lax.broadcasted_iota(jnp.int32, (tn, past), 1)
    dist_c = (past + row_c - col_c).astype(F32)
    row_n = lax.broadcasted_iota(jnp.int32, (tn, LANES), 0)
    col_n = lax.broadcasted_iota(jnp.int32, (tn, LANES), 1)
    dist_n = jnp.abs(row_n - col_n).astype(F32)
    real = col_n < tn

    def logits(c):
        hd, jm = divmod(c, 2)
        pr, e = divmod(hd, 2)
        slope = _alibi_slope(hd)
        idx = 4 * pr + 2 * e + jm
        qh = q[:, LANES * idx:LANES * idx + LANES]
        s_c = _dot(qh, ck[LANES * pr:LANES * pr + LANES, :]) - slope * dist_c
        s_n = jnp.where(real, _nt_dot(qh, kn[:, LANES * pr:LANES * pr + LANES]) - slope * dist_n, NEG)
        return s_c, s_n

    def attend(c, s):
        pr = c // 4
        acc, l = _two_part_softmax(s[0], s[1], cv[LANES * pr:LANES * pr + LANES, :],
                                   vn[:, LANES * pr:LANES * pr + LANES])
        acc_ref[c] = acc
        l_ref[c] = l

    _pipelined(2 * N_HEADS, logits, attend, depth=SAMPLE_LOOKAHEAD)
    lam = _diff_lambda(lam_ref[...], lam_init)
    _diff_finish(acc_ref, l_ref, lam, gsub_ref[...], lam_init, tn, o_ref)


def _diff_sample(qv, kb, vb, ck, cv, lam_p, gsub, *, lam_init, layer):
    b, tn, _ = qv.shape
    past = ck.shape[3]
    per_b = lambda i: (i, 0, 0)
    cache_b = lambda i: (layer, i, 0, 0)
    return pl.pallas_call(
        functools.partial(_diff_sample_kernel, tn=tn, past=past, lam_init=lam_init),
        grid=(b,),
        in_specs=[pl.BlockSpec((1, tn, 8 * LANES), per_b), pl.BlockSpec((1, tn, GROUP), per_b),
                  pl.BlockSpec((1, tn, GROUP), per_b), pl.BlockSpec((1, 1, GROUP, past), cache_b),
                  pl.BlockSpec((1, 1, GROUP, past), cache_b), pl.BlockSpec((4, DIFF_HALF), lambda i: (0, 0)),
                  pl.BlockSpec((1, LANES), lambda i: (0, 0))],
        out_specs=pl.BlockSpec((1, tn, GROUP), per_b),
        out_shape=jax.ShapeDtypeStruct((b, tn, GROUP), BF16),
        scratch_shapes=[pltpu.VMEM((2 * N_HEADS, tn, 1), F32), pltpu.VMEM((2 * N_HEADS, tn, LANES), F32)],
        compiler_params=_cparams(("parallel",)),
        name="diff_sample",
    )(qv, kb, vb, ck, cv, lam_p, gsub)


def _band_sample_kernel(q_ref, kn_ref, vn_ref, ck_ref, cv_ref, bias_ref, o_ref, *, tn, past):
    q = q_ref[0]
    kn = _pad_rows(kn_ref[0], LANES)
    vn = _pad_rows(vn_ref[0], LANES)
    ck = ck_ref[0, 0].astype(BF16)
    cv = cv_ref[0, 0].astype(BF16)
    lane = lax.broadcasted_iota(jnp.int32, (tn, LANES), 1)
    real = lane < tn

    def logits(hd):
        pr = hd // 2
        qh = q[:, LANES * hd:LANES * hd + LANES]
        s_c = _dot(qh, ck[LANES * pr:LANES * pr + LANES, :]) + bias_ref[hd, 0:tn, 0:past]
        s_n = _nt_dot(qh, kn[:, LANES * pr:LANES * pr + LANES]) + bias_ref[hd, 0:tn, past:past + LANES]
        return s_c, jnp.where(real, s_n, NEG)

    even = {}

    def attend(hd, s):
        pr, e = divmod(hd, 2)
        acc, l = _two_part_softmax(s[0], s[1], cv[LANES * pr:LANES * pr + LANES, :],
                                   vn[:, LANES * pr:LANES * pr + LANES])
        if e == 0:
            even[pr] = acc / l
        else:
            o_ref[0, :, LANES * pr:LANES * pr + LANES] = jnp.where(lane < HEAD_DIM, even[pr], acc / l).astype(BF16)

    _pipelined(N_HEADS, logits, attend, depth=SAMPLE_LOOKAHEAD)


def _band_sample(qm, kb, vb, ck, cv, bias, *, layer):
    b, tn, _ = qm.shape
    past = ck.shape[3]
    per_b = lambda i: (i, 0, 0)
    cache_b = lambda i: (layer, i, 0, 0)
    return pl.pallas_call(
        functools.partial(_band_sample_kernel, tn=tn, past=past),
        grid=(b,),
        in_specs=[pl.BlockSpec((1, tn, 4 * LANES), per_b), pl.BlockSpec((1, tn, GROUP), per_b),
                  pl.BlockSpec((1, tn, GROUP), per_b), pl.BlockSpec((1, 1, GROUP, past), cache_b),
                  pl.BlockSpec((1, 1, GROUP, past), cache_b),
                  pl.BlockSpec((N_HEADS, CHUNK, BAND_WIN), lambda i: (0, 0, 0))],
        out_specs=pl.BlockSpec((1, tn, GROUP), per_b),
        out_shape=jax.ShapeDtypeStruct((b, tn, GROUP), BF16),
        compiler_params=_cparams(("parallel",)),
        name="band_sample",
    )(qm, kb, vb, ck, cv, bias)


def _out_proj_kernel(x_ref, gt_ref, a_ref, b_ref, c_ref, d_ref, w_ref, o_ref, *, bb, tt):
    acc = _dot(a_ref[...], w_ref[0, 0:GROUP, :])
    acc += _dot(b_ref[...], w_ref[0, GROUP:2 * GROUP, :])
    acc += _dot(c_ref[...], w_ref[0, 2 * GROUP:3 * GROUP, :])
    acc += _dot(d_ref[...], w_ref[0, 3 * GROUP:4 * GROUP, :])
    o_ref[...] = x_ref[...] + gt_ref[...] * acc.reshape(bb, tt, D_MODEL)


def _out_proj(x, gt, o_fox, o_diff, o_ml, o_band, w_out, *, bb, tt, layer):
    bx, tx, d = x.shape
    nb, nt = bx // bb, tx // tt
    tm = bb * tt
    tok = lambda i, t: (i * nt + t, 0)
    return pl.pallas_call(
        functools.partial(_out_proj_kernel, bb=bb, tt=tt),
        grid=(nb, nt),
        in_specs=[pl.BlockSpec((bb, tt, d), lambda i, t: (i, t, 0)),
                  pl.BlockSpec((bb, 1, d), lambda i, t: (i, 0, 0)),
                  pl.BlockSpec((tm, GROUP), tok), pl.BlockSpec((tm, GROUP), tok),
                  pl.BlockSpec((tm, GROUP), tok), pl.BlockSpec((tm, GROUP), tok),
                  pl.BlockSpec((1, d, d), lambda i, t: (layer, 0, 0), pipeline_mode=pl.Buffered(1))],
        out_specs=pl.BlockSpec((bb, tt, d), lambda i, t: (i, t, 0)),
        out_shape=jax.ShapeDtypeStruct(x.shape, F32),
        compiler_params=_cparams(("parallel", "parallel")),
        name="out_proj",
    )(x, gt, o_fox, o_diff, o_ml, o_band, w_out)


def _ffn_kernel(x_ref, sh_ref, sc_ref, gt_ref, g2_ref, wg_ref, wu_ref, wd_ref, o_ref, acc_ref, *, bb, tt, tf):
    tm = bb * tt
    x = x_ref[...]
    ms = jnp.mean(x * x, axis=-1, keepdims=True)
    h = x * lax.rsqrt(ms + EPS) * g2_ref[...]
    h = h * (1.0 + sc_ref[...]) + sh_ref[...]
    hb = h.reshape(tm, D_MODEL).astype(BF16)

    def gate_up(f):
        cols = slice(tf * f, tf * f + tf)
        return _dot(hb, wg_ref[0, :, cols]), _dot(hb, wu_ref[0, :, cols])

    def down(f, gu):
        g, u = gu
        a = (g * jax.nn.sigmoid(g) * u).astype(BF16)
        part = _dot(a, wd_ref[0, tf * f:tf * f + tf, :])
        if f == 0:
            acc_ref[...] = part
        else:
            acc_ref[...] += part

    _pipelined(D_FF // tf, gate_up, down)
    o_ref[...] = x + gt_ref[...] * acc_ref[...].reshape(bb, tt, D_MODEL)


def _ffn(x, sh, sc, gt, g2, wg, wu, wd, *, bb, tt, tf, layer):
    bx, tx, d = x.shape
    nb, nt = bx // bb, tx // tt
    tm = bb * tt
    xs = pl.BlockSpec((bb, tt, d), lambda i, t: (i, t, 0))
    ms = pl.BlockSpec((bb, 1, d), lambda i, t: (i, 0, 0))
    resident = lambda shape: pl.BlockSpec((1,) + shape, lambda i, t: (layer, 0, 0), pipeline_mode=pl.Buffered(1))
    return pl.pallas_call(
        functools.partial(_ffn_kernel, bb=bb, tt=tt, tf=tf),
        grid=(nb, nt),
        in_specs=[xs, ms, ms, ms, pl.BlockSpec((1, d), lambda i, t: (0, 0)),
                  resident((d, D_FF)), resident((d, D_FF)), resident((D_FF, d))],
        out_specs=xs,
        out_shape=jax.ShapeDtypeStruct(x.shape, F32),
        scratch_shapes=[pltpu.VMEM((tm, d), F32)],
        compiler_params=_cparams(("parallel", "parallel")),
        name="ffn",
    )(x, sh, sc, gt, g2, wg, wu, wd)


def _consts(seg):
    r = np.arange(GROUP)
    s64 = (r[:, None] // HEAD_DIM == r[None, :] // HEAD_DIM)
    s32 = (r[:, None] // DIFF_HALF == r[None, :] // DIFF_HALF)
    q = np.arange(LANES)
    tril = (q[None, :] <= q[:, None]) & (q[:, None] // seg == q[None, :] // seg)
    as_bf16 = lambda m: jnp.asarray(m.astype(np.float32), dtype=BF16)
    return as_bf16(s64), as_bf16(s32), as_bf16(tril), jnp.asarray(s64.astype(np.float32))


def _prep_layer(l, norm1_g, norm2_g, w_in_g, b_in, qk_g_fox, qk_g_diff, qk_g_band, conv_w, conv_b,
                diff_lambda, diff_subln_g, mlstm_norm_g, band_rel_bias, w_out, w_ffn_gate, w_ffn_up,
                w_ffn_down):
    starts = np.concatenate([[0], np.cumsum(IN_SPLIT_SIZES)]).tolist()
    order = list(_FULL_GROUPS) + list(_GATE_GROUPS)
    bl = b_in[l]
    b_cols = [bl[starts[g]:starts[g + 1]] for g in order]
    pad = N_IN_PAD - N_FULL - N_GATES
    b_cols.append(jnp.zeros((pad,), F32))
    gains = jnp.stack([
        jnp.tile(qk_g_fox[l, 0], N_HEADS), jnp.tile(qk_g_fox[l, 1], N_HEADS),
        jnp.tile(qk_g_diff[l, 0], 2 * N_HEADS), jnp.tile(qk_g_diff[l, 1], 2 * N_HEADS),
        jnp.tile(qk_g_band[l, 0], N_HEADS), jnp.tile(qk_g_band[l, 1], N_HEADS),
        jnp.zeros((GROUP,), F32), jnp.zeros((GROUP,), F32)])
    return dict(
        g1=norm1_g[l].reshape(1, D_MODEL), g2=norm2_g[l].reshape(1, D_MODEL),
        w_in=w_in_g,
        b_in=jnp.concatenate(b_cols).reshape(1, N_IN_PAD),
        gains=gains, conv_w=conv_w[l], conv_b=conv_b[l].reshape(1, 2 * GROUP),
        lam_p=diff_lambda[l], gsub=jnp.tile(diff_subln_g[l], 2).reshape(1, LANES),
        gmh=jnp.tile(mlstm_norm_g[l], N_HEADS).reshape(1, GROUP),
        table=band_rel_bias[l],
        w_out=w_out, wg=w_ffn_gate, wu=w_ffn_up, wd=w_ffn_down,
        lam_init=0.8 - 0.6 * math.exp(-0.3 * l))


def _layer(x, mod, lp, caches, *, bb, tt, layer, depth, prev_state=None):
    bx, tx, _ = x.shape
    n = bx * tx
    prompt = caches is None
    sh1, sc1, gt1, sh2, sc2, gt2 = [mod[:, i:i + 1, :] for i in range(6)]
    s64, s32, tril, bd = _consts(tx if prompt else min(tx, LANES))
    z, big = _in_proj(x, sh1, sc1, lp["g1"], lp["w_in"], lp["b_in"], lp["gains"], s64, s32, tril,
                      bb=bb, tt=tt, running=prompt, ml_blocks=ML_CHUNK // LANES,
                      layer=layer, depth=depth, prev_state=prev_state)
    r3 = lambda a: a.reshape(bx, tx, a.shape[-1])
    bias_p, bias_s = _band_bias(lp["table"])

    if prompt:
        tq = 512
        o_fox = _fox_prompt(r3(z["fqa"]), r3(z["fka"]), r3(z["fvb"]), tq=tq)
        o_diff = _diff_prompt(r3(z["dqv"]), r3(z["dkb"]), r3(z["dvb"]), lp["lam_p"], lp["gsub"],
                              tq=tq, lam_init=lp["lam_init"])
        o_band = _band_prompt(r3(z["bqm"]), r3(z["bkb"]), r3(z["bvb"]), bias_p, tq=tq)
        c0 = jnp.zeros((bx, GROUP, GROUP), F32)
        n0 = jnp.zeros((bx, 1, GROUP), F32)
        m0 = jnp.zeros((bx, 1, GROUP), F32)
        conv0 = jnp.zeros((bx, 8, 2 * GROUP), F32)
        ml_valid, ml_rows = ML_CHUNK, ML_CHUNK
    else:
        (c_fk, c_fv, c_flf, c_dk, c_dv, c_bk, c_bv, s_c, s_n, s_m, s_conv) = caches
        past = c_fk.shape[3]
        assert past % CHUNK == 0 and tx <= CHUNK
        jj = np.arange(past)
        lstr = jnp.asarray((jj[:, None] > jj[None, :]).astype(np.float32), dtype=BF16)
        o_fox = _fox_sample(r3(z["fqa"]), r3(z["fka"]), r3(z["fvb"]), c_fk, c_fv, c_flf, lstr, layer=layer)
        o_diff = _diff_sample(r3(z["dqv"]), r3(z["dkb"]), r3(z["dvb"]), c_dk, c_dv,
                              lp["lam_p"], lp["gsub"], lam_init=lp["lam_init"], layer=layer)
        o_band = _band_sample(r3(z["bqm"]), r3(z["bkb"]), r3(z["bvb"]), c_bk, c_bv, bias_s, layer=layer)
        eye = jnp.eye(N_HEADS, dtype=F32)
        c0 = (s_c[:, :, :, None, :] * eye[None, :, None, :, None]).reshape(bx, GROUP, GROUP)
        n0 = s_n.reshape(bx, 1, GROUP)
        m0 = jnp.repeat(s_m, HEAD_DIM, axis=-1).reshape(bx, 1, GROUP)
        conv0 = jnp.pad(s_conv, ((0, 0), (8 - (CONV_W - 1), 0), (0, 0)))
        ml_valid, ml_rows = tx, ML_PAD
    o_ml, c_new, n_new, m_new = _mlstm(
        z["mqk"], z["mvb"], z["gates"], z["cum"], z["mos"], lp["conv_w"], lp["conv_b"], lp["gmh"],
        s64, bd, c0, n0, m0, conv0, nb=bx, t=tx, valid=ml_valid, rows=ml_rows, ns=min(bx, ML_SEQS))

    flat2 = lambda a: a.reshape(n, GROUP)
    x1 = _out_proj(x, gt1, flat2(o_fox), flat2(o_diff), o_ml, flat2(o_band), lp["w_out"], bb=bb, tt=tt,
                   layer=layer)
    x2 = _ffn(x1, sh2, sc2, gt2, lp["g2"], lp["wg"], lp["wu"], lp["wd"], bb=bb, tt=tt, tf=FFN_CHUNK,
              layer=layer)

    mc = jnp.stack([c_new[:, HEAD_DIM * h:HEAD_DIM * h + HEAD_DIM, HEAD_DIM * h:HEAD_DIM * h + HEAD_DIM]
                    for h in range(N_HEADS)], axis=1)
    small = (r3(z["gates"])[:, :, 0:N_HEADS], mc, n_new.reshape(bx, N_HEADS, HEAD_DIM),
             m_new.reshape(bx, N_HEADS, HEAD_DIM)[:, :, 0], r3(z["mqk"])[:, tx - (CONV_W - 1):, :])
    return x2, big, small


def _channel_major(c):
    perm = (0, 1) + tuple(range(3, c.ndim)) + (2,)
    return jnp.transpose(c, perm).reshape(c.shape[0], c.shape[1], -1, c.shape[2])


def _token_major(a, inner):
    d, b, _, t = a.shape
    k = len(inner)
    return jnp.transpose(a.reshape(d, b, *inner, t), (0, 1, 2 + k) + tuple(range(2, 2 + k)))


def kernel(x_prompt, x_sample, c_prompt, c_sample, cache_fox_k, cache_fox_v, cache_fox_logf, cache_diff_k, cache_diff_v, cache_band_k, cache_band_v, state_mlstm_c, state_mlstm_n, state_mlstm_m, state_conv, norm1_g, norm2_g, w_mod, b_mod, w_in, b_in, qk_g_fox, qk_g_diff, qk_g_band, conv_w, conv_b, diff_lambda, diff_subln_g, mlstm_norm_g, band_rel_bias, w_out, w_ffn_gate, w_ffn_up, w_ffn_down):
    depth = w_in.shape[0]
    bp, bs = x_prompt.shape[0], x_sample.shape[0]
    ts = x_sample.shape[1]
    caches = tuple(_channel_major(c) for c in (cache_fox_k, cache_fox_v, cache_fox_logf, cache_diff_k,
                                               cache_diff_v, cache_band_k, cache_band_v))
    caches += (state_mlstm_c, state_mlstm_n, state_mlstm_m, state_conv)
    mod = _modulation(jnp.concatenate([c_prompt, c_sample], axis=0), w_mod, b_mod)
    mod = mod.reshape(depth, bp + bs, 6, D_MODEL)
    w_in_g = _regroup_w_in(w_in)
    big_w = [w.astype(BF16) for w in (w_out, w_ffn_gate, w_ffn_up, w_ffn_down)]
    layers = [_prep_layer(l, norm1_g, norm2_g, w_in_g, b_in, qk_g_fox, qk_g_diff, qk_g_band, conv_w,
                          conv_b, diff_lambda, diff_subln_g, mlstm_norm_g, band_rel_bias, *big_w)
              for l in range(depth)]
    sample_bb = 512 // ts
    xp, xs = x_prompt, x_sample
    p_big, p_small, s_big, s_small = None, [], [], []
    for l in range(depth):
        xp, p_big, sm = _layer(xp, mod[l, :bp], layers[l], None, bb=1, tt=512, layer=l, depth=depth,
                               prev_state=p_big)
        p_small.append(sm)
    for l in range(depth):
        xs, big, sm = _layer(xs, mod[l, bp:], layers[l], caches[:7] + tuple(c[l] for c in caches[7:]),
                             bb=sample_bb, tt=ts,
                             layer=l, depth=depth)
        s_big.append(big)
        s_small.append(sm)
    head = (N_HEADS, HEAD_DIM)
    half = (N_HEADS, 2, DIFF_HALF)
    inner = (head, head, half, head, head, head)
    p_fk, p_fv, p_dk, p_dv, p_bk, p_bv = [_token_major(a, inn) for a, inn in zip(p_big, inner)]
    p_flf, p_c, p_n, p_m, p_conv = [jnp.stack(zs) for zs in zip(*p_small)]
    s_fk, s_fv, s_dk, s_dv, s_bk, s_bv = [jnp.stack(zs).reshape((depth, bs, ts) + inn)
                                          for zs, inn in zip(zip(*s_big), inner)]
    s_flf, s_c, s_n, s_m, s_conv = [jnp.stack(zs) for zs in zip(*s_small)]
    return (xp, xs, p_fk, p_fv, p_flf, p_dk, p_dv, p_bk, p_bv, p_c, p_n, p_m, p_conv,
            s_fk, s_fv, s_flf, s_dk, s_dv, s_bk, s_bv, s_c, s_n, s_m, s_conv)
```

```python
import functools
import math

import numpy as np
import jax
import jax.numpy as jnp
from jax import lax
from jax.experimental import pallas as pl
from jax.experimental.pallas import tpu as pltpu

F32 = jnp.float32
BF16 = jnp.bfloat16

D_MODEL = 1024
HEAD_DIM = 64
N_HEADS = 4
GROUP = N_HEADS * HEAD_DIM
DIFF_HALF = HEAD_DIM // 2
CHUNK = 64
BAND_CHUNKS = 8
BAND_ROWS = BAND_CHUNKS * CHUNK
REL_CLIP = 128
CONV_W = 4
D_FF = 2816
EPS = 1e-6
NEG = -1e30
LOG2E = math.log2(math.e)

LANES = 128
PAIR = 2 * HEAD_DIM
N_GATES = 3 * N_HEADS
BAND_WIN = (BAND_CHUNKS + 2) * CHUNK
ML_CHUNK = 256
ML_PAD = 128
ML_SEQS = 4
BAND_LOOKAHEAD = 5
SAMPLE_LOOKAHEAD = 2
Q_SUBTILES = 2
FFN_CHUNK = 256
VMEM_LIMIT = 56 * 1024 * 1024

IN_SPLIT_SIZES = (GROUP, GROUP, GROUP, N_HEADS, GROUP, GROUP, GROUP, 2 * GROUP, GROUP,
                  N_HEADS, N_HEADS, GROUP, GROUP, GROUP, GROUP)
_FULL_GROUPS = (0, 1, 2, 4, 5, 6, 7, 8, 11, 12, 13, 14)
_GATE_GROUPS = (3, 9, 10)
N_FULL = 13 * GROUP
N_IN_PAD = N_FULL + LANES


def _cparams(sem):
    return pltpu.CompilerParams(dimension_semantics=sem, vmem_limit_bytes=VMEM_LIMIT)


def _nt_dot(a, b):
    return lax.dot_general(a, b, (((1,), (1,)), ((), ())), preferred_element_type=F32)


def _dot(a, b):
    return jnp.dot(a, b, preferred_element_type=F32)


def _split3(x):
    hi = x.astype(BF16)
    r = x - hi.astype(F32)
    mid = r.astype(BF16)
    lo = (r - mid.astype(F32)).astype(BF16)
    return hi, mid, lo


def _log_sigmoid(x):
    return jnp.minimum(x, 0.0) - jnp.log1p(jnp.exp(-jnp.abs(x)))


def _mod_kernel(c_ref, w_ref, b_ref, o_ref):
    c = c_ref[...]
    a = (c * jax.nn.sigmoid(c)).astype(BF16)
    o_ref[0] = _dot(a, w_ref[0].astype(BF16)) + b_ref[0]


def _modulation(c_all, w_mod, b_mod):
    depth, d, n = w_mod.shape
    rows = c_all.shape[0]
    tn = 768
    return pl.pallas_call(
        _mod_kernel,
        grid=(depth, n // tn),
        in_specs=[pl.BlockSpec((rows, d), lambda l, j: (0, 0)),
                  pl.BlockSpec((1, d, tn), lambda l, j: (l, 0, j)),
                  pl.BlockSpec((1, 1, tn), lambda l, j: (l, 0, j))],
        out_specs=pl.BlockSpec((1, rows, tn), lambda l, j: (l, 0, j)),
        out_shape=jax.ShapeDtypeStruct((depth, rows, n), F32),
        compiler_params=_cparams(("parallel", "parallel")),
        name="modulation",
    )(c_all, w_mod, b_mod.reshape(depth, 1, n))


def _regroup_kernel(w_ref, o_ref):
    starts = np.concatenate([[0], np.cumsum(IN_SPLIT_SIZES)]).tolist()
    col = 0
    for g in _FULL_GROUPS + _GATE_GROUPS:
        width = IN_SPLIT_SIZES[g]
        o_ref[0, :, col:col + width] = w_ref[0, :, starts[g]:starts[g] + width].astype(BF16)
        col += width
    o_ref[0, :, col:N_IN_PAD] = jnp.zeros((o_ref.shape[1], N_IN_PAD - col), BF16)


def _regroup_w_in(w_in):
    depth, d, n_in = w_in.shape
    tr = 256
    return pl.pallas_call(
        _regroup_kernel,
        grid=(depth, d // tr),
        in_specs=[pl.BlockSpec((1, tr, n_in), lambda l, r: (l, r, 0))],
        out_specs=pl.BlockSpec((1, tr, N_IN_PAD), lambda l, r: (l, r, 0)),
        out_shape=jax.ShapeDtypeStruct((depth, d, N_IN_PAD), BF16),
        compiler_params=_cparams(("parallel", "parallel")),
        name="regroup_w_in",
    )(w_in)


_IN_OUTS = (
    ("fqa", 4 * LANES, BF16),
    ("fka", 4 * LANES, BF16),
    ("fvb", GROUP, BF16),
    ("dqv", 8 * LANES, BF16),
    ("dkb", GROUP, BF16), ("dvb", GROUP, BF16),
    ("mqk", 2 * GROUP, F32), ("mvb", GROUP, BF16), ("mos", GROUP, F32),
    ("bqm", 4 * LANES, BF16),
    ("bkb", GROUP, BF16), ("bvb", GROUP, BF16),
    ("gates", LANES, F32),
    ("cum", LANES, F32),
)
_IN_STATE = ("fk", "fv", "dk", "dv", "bk", "bv")
N_IN_ARGS = 10


def _in_proj_kernel(*refs, bb, tt, running, ml_blocks, n_prev):
    (x_ref, sh_ref, sc_ref, g1_ref, w_ref, b_ref, gains_ref, s64_ref, s32_ref, tril_ref) = refs[:N_IN_ARGS]
    outs = refs[N_IN_ARGS + n_prev:]
    (fqa_ref, fka_ref, fvb_ref, dqv_ref, dkb_ref, dvb_ref, mqk_ref, mvb_ref, mos_ref, bqm_ref, bkb_ref,
     bvb_ref, gates_ref, cum_ref) = outs[:len(_IN_OUTS)]
    state = dict(zip(_IN_STATE, outs[len(_IN_OUTS):]))
    carry_ref = outs[len(_IN_OUTS) + len(_IN_STATE)]
    tm = bb * tt

    def put_state(name, val):
        ref = state[name]
        if not running:
            ref[...] = val
        elif name in ("bk", "bv"):
            @pl.when(pl.program_id(1) == pl.num_programs(1) - 1)
            def _():
                ref[0, 0] = val.T
        else:
            ref[0, 0] = val.T

    x = x_ref[...]
    ms = jnp.mean(x * x, axis=-1, keepdims=True)
    h = x * lax.rsqrt(ms + EPS) * g1_ref[...]
    h = h * (1.0 + sc_ref[...]) + sh_ref[...]
    hb = h.reshape(tm, D_MODEL).astype(BF16)

    group_order = iter([13] + list(range(13)))
    pending = []

    def issue():
        g = next(group_order, None)
        if g is not None:
            c0, width = GROUP * g, (LANES if g == 13 else GROUP)
            pending.append((g, _dot(hb, w_ref[0, :, c0:c0 + width]) + b_ref[:, c0:c0 + width]))

    def proj(g, width=GROUP):
        got, z = pending.pop(0)
        assert got == g and z.shape[1] == width
        issue()
        return z

    issue()

    def rms_seg(z, s_ref, row, n):
        ss = _dot((z * z).astype(BF16), s_ref[...])
        return z * lax.rsqrt(ss * (1.0 / n) + EPS) * gains_ref[row:row + 1, :]

    lane = lax.broadcasted_iota(jnp.int32, (tm, LANES), 1)

    zg = proj(13, LANES)
    is_ls = (lane < N_HEADS) | ((lane >= 2 * N_HEADS) & (lane < N_GATES))
    gates = jnp.where(is_ls, _log_sigmoid(zg), zg)
    gates_ref[...] = gates
    ghi, gmid, glo = _split3(gates)
    tril = tril_ref[...]
    lane_b = lax.broadcasted_iota(jnp.int32, (LANES, LANES), 1)
    if running:
        @pl.when(pl.program_id(1) == 0)
        def _():
            carry_ref[...] = jnp.zeros_like(carry_ref)
        carry_a = carry_ref[0:1, :]
    carry_b = None
    cums = []
    for blk in range(tm // LANES):
        sl = slice(LANES * blk, LANES * blk + LANES)
        p = _dot(tril, ghi[sl]) + _dot(tril, gmid[sl]) + _dot(tril, glo[sl])
        if running:
            ca = p + carry_a
            carry_a = ca[LANES - 1:LANES, :]
            cb = p if blk % ml_blocks == 0 else p + carry_b
            carry_b = cb[LANES - 1:LANES, :]
            p = jnp.where(lane_b < N_HEADS, ca, cb)
        cums.append(p)
    if running:
        carry_ref[0:1, :] = carry_a
    cum = jnp.concatenate(cums, axis=0)
    cum_ref[...] = cum

    fqn = rms_seg(proj(0), s64_ref, 0, HEAD_DIM) * (HEAD_DIM ** -0.5 * LOG2E)
    fkn = rms_seg(proj(1), s64_ref, 1, HEAD_DIM)
    put_state("fk", fkn)
    fv = proj(2)
    put_state("fv", fv)
    fvb_ref[...] = fv.astype(BF16)
    ones_q = jnp.where((lane >= HEAD_DIM + 3) & (lane < HEAD_DIM + 6), 1.0, 0.0)
    ones_k = jnp.where((lane >= HEAD_DIM) & (lane < HEAD_DIM + 3), 1.0, 0.0)
    for hd in range(N_HEADS):
        pr, e = divmod(hd, 2)
        bq = fqn[:, LANES * pr:LANES * pr + LANES]
        bk = fkn[:, LANES * pr:LANES * pr + LANES]
        if e:
            bq = pltpu.roll(bq, HEAD_DIM, axis=1)
            bk = pltpu.roll(bk, HEAD_DIM, axis=1)
        cbc = jnp.broadcast_to(cum[:, hd:hd + 1], (tm, LANES)) * LOG2E
        chi = cbc.astype(BF16).astype(F32)
        r = cbc - chi
        cmid = r.astype(BF16).astype(F32)
        clo = (r - cmid).astype(BF16).astype(F32)
        aq = jnp.where(lane == HEAD_DIM, chi,
                       jnp.where(lane == HEAD_DIM + 1, cmid, jnp.where(lane == HEAD_DIM + 2, clo, ones_q)))
        ak = jnp.where(lane == HEAD_DIM + 3, -chi,
                       jnp.where(lane == HEAD_DIM + 4, -cmid, jnp.where(lane == HEAD_DIM + 5, -clo, ones_k)))
        fqa_ref[:, LANES * hd:LANES * hd + LANES] = jnp.where(lane < HEAD_DIM, bq, aq).astype(BF16)
        fka_ref[:, LANES * hd:LANES * hd + LANES] = jnp.where(lane < HEAD_DIM, bk, ak).astype(BF16)

    dqn = rms_seg(proj(3), s32_ref, 2, DIFF_HALF) * (DIFF_HALF ** -0.5 * LOG2E)
    dkn = rms_seg(proj(4), s32_ref, 3, DIFF_HALF)
    put_state("dk", dkn)
    dkb_ref[...] = dkn.astype(BF16)
    dv = proj(5)
    put_state("dv", dv)
    dvb_ref[...] = dv.astype(BF16)
    for pr in range(2):
        blk = dqn[:, LANES * pr:LANES * pr + LANES]
        for e in range(2):
            for jm in range(2):
                l0 = HEAD_DIM * e + DIFF_HALF * jm
                idx = 4 * pr + 2 * e + jm
                dqv_ref[:, LANES * idx:LANES * idx + LANES] = jnp.where(
                    (lane >= l0) & (lane < l0 + DIFF_HALF), blk, 0.0).astype(BF16)

    mqk_ref[:, 0:GROUP] = proj(6)
    mqk_ref[:, GROUP:2 * GROUP] = proj(7)
    mvb_ref[...] = proj(8).astype(BF16)
    mos_ref[...] = jax.nn.sigmoid(proj(9))

    bqn = rms_seg(proj(10), s64_ref, 4, HEAD_DIM) * (HEAD_DIM ** -0.5 * LOG2E)
    bkn = rms_seg(proj(11), s64_ref, 5, HEAD_DIM)
    put_state("bk", bkn)
    bkb_ref[...] = bkn.astype(BF16)
    bv = proj(12)
    put_state("bv", bv)
    bvb_ref[...] = bv.astype(BF16)
    for hd in range(N_HEADS):
        pr, e = divmod(hd, 2)
        blk = bqn[:, LANES * pr:LANES * pr + LANES]
        bqm_ref[:, LANES * hd:LANES * hd + LANES] = jnp.where(
            (lane >= HEAD_DIM * e) & (lane < HEAD_DIM * e + HEAD_DIM), blk, 0.0).astype(BF16)


def _in_proj(x, sh, sc, g1, w, b, gains, s64, s32, tril, *, bb, tt, running, ml_blocks,
             layer=0, depth=1, prev_state=None):
    bx, tx, d = x.shape
    n = bx * tx
    nb, nt = bx // bb, tx // tt
    tm = bb * tt
    const = lambda i, t: (0, 0)
    tok = lambda i, t: (i * nt + t, 0)
    keep = min(BAND_ROWS, tx)
    if running:
        assert bb == 1 and keep == tm
        st_shapes = [(depth, bx, GROUP, tx)] * 4 + [(depth, bx, GROUP, keep)] * 2
        st_specs = ([pl.BlockSpec((1, 1, GROUP, tm), lambda i, t: (layer, i, 0, t))] * 4
                    + [pl.BlockSpec((1, 1, GROUP, keep), lambda i, t: (layer, i, 0, 0))] * 2)
    else:
        st_shapes = [(n, GROUP)] * 6
        st_specs = [pl.BlockSpec((tm, GROUP), tok)] * 6
    prev = list(prev_state) if prev_state is not None else []
    aliases = {N_IN_ARGS + k: len(_IN_OUTS) + k for k in range(len(prev))}
    in_specs = [
        pl.BlockSpec((bb, tt, d), lambda i, t: (i, t, 0)),
        pl.BlockSpec((bb, 1, d), lambda i, t: (i, 0, 0)),
        pl.BlockSpec((bb, 1, d), lambda i, t: (i, 0, 0)),
        pl.BlockSpec((1, d), const),
        pl.BlockSpec((1, d, N_IN_PAD), lambda i, t: (layer, 0, 0), pipeline_mode=pl.Buffered(1)),
        pl.BlockSpec((1, N_IN_PAD), const),
        pl.BlockSpec((8, GROUP), const),
        pl.BlockSpec((GROUP, GROUP), const),
        pl.BlockSpec((GROUP, GROUP), const),
        pl.BlockSpec((LANES, LANES), const),
    ]
    in_specs += [pl.BlockSpec(memory_space=pl.ANY)] * len(prev)
    out_specs = [pl.BlockSpec((tm, wd), tok) for _, wd, _ in _IN_OUTS] + st_specs
    out_shape = ([jax.ShapeDtypeStruct((n, wd), dt) for _, wd, dt in _IN_OUTS]
                 + [jax.ShapeDtypeStruct(s, F32) for s in st_shapes])
    outs = pl.pallas_call(
        functools.partial(_in_proj_kernel, bb=bb, tt=tt, running=running, ml_blocks=ml_blocks,
                          n_prev=len(prev)),
        grid=(nb, nt),
        in_specs=in_specs,
        out_specs=out_specs,
        out_shape=out_shape,
        scratch_shapes=[pltpu.VMEM((8, LANES), F32)],
        input_output_aliases=aliases,
        compiler_params=_cparams(("arbitrary", "arbitrary")),
        name="in_proj",
    )(x, sh, sc, g1, w, b, gains, s64, s32, tril, *prev)
    ops = {name: o for (name, _, _), o in zip(_IN_OUTS, outs)}
    return ops, list(outs[len(_IN_OUTS):])


def _softmax_step(s, v, m_ref, l_ref, acc_ref, c, shift=None):
    nt = s.shape[1] // LANES
    tiles = [s[:, LANES * t:LANES * t + LANES] for t in range(nt)]
    m_prev = m_ref[c]
    m_cur = jnp.max(s, axis=-1, keepdims=True)
    if shift is not None:
        m_cur = m_cur - shift
    m_new = jnp.maximum(m_prev, m_cur)
    alpha = jnp.exp2(m_prev - m_new)
    m_sub = m_new if shift is None else m_new + shift
    ps = [jnp.exp2(t - m_sub) for t in tiles]
    psum = ps[0]
    for p in ps[1:]:
        psum = psum + p
    l_ref[c] = alpha * l_ref[c] + psum
    p = jnp.concatenate([p.astype(BF16) for p in ps], axis=1)
    acc_ref[c] = alpha * acc_ref[c] + _dot(p, v)
    m_ref[c] = m_new


def _pipelined(n, make_s, consume, depth=1):
    queue = [make_s(c) for c in range(min(depth, n))]
    for c in range(n):
        if c + depth < n:
            queue.append(make_s(c + depth))
        consume(c, queue.pop(0))


def _row_sum(l):
    return jnp.sum(l, axis=-1, keepdims=True)


def _causal_pairs(nq, nh):
    qi = np.array([i for i in range(nq) for _ in range(nh * i + nh)], np.int32)
    kj = np.array([j for i in range(nq) for j in range(nh * i + nh)], np.int32)
    return jnp.asarray(qi), jnp.asarray(kj)


def _causal_sweep(i, j, nh, step, finalize):
    d = j - nh * i

    @pl.when(d < 0)
    def _():
        step(("full",) * nh)

    for dd in range(nh):
        @pl.when(d == dd)
        def _(dd=dd):
            step(tuple("skip" if h < dd else ("diag" if h == dd else "full") for h in range(nh)))
            if dd == nh - 1:
                finalize()


def _init_softmax_state(m_ref, l_ref, acc_ref):
    m_ref[...] = jnp.full_like(m_ref, NEG)
    l_ref[...] = jnp.zeros_like(l_ref)
    acc_ref[...] = jnp.zeros_like(acc_ref)


def _fox_kernel(qi_ref, kj_ref, q_ref, k_ref, v_ref, o_ref, m_ref, l_ref, acc_ref, *, tk, nh):
    p = pl.program_id(1)
    i = qi_ref[p]
    j = kj_ref[p]

    @pl.when(j == 0)
    def _():
        _init_softmax_state(m_ref, l_ref, acc_ref)

    def step(modes):
        k = k_ref[0]
        v = v_ref[0]
        combos = [(h, hd) for h in range(nh) if modes[h] != "skip" for hd in range(N_HEADS)]
        if "diag" in modes:
            row = lax.broadcasted_iota(jnp.int32, (tk, tk), 0)
            col = lax.broadcasted_iota(jnp.int32, (tk, tk), 1)
            keep = col <= row

        def make_s(c):
            h, hd = combos[c]
            q = q_ref[0, tk * h:tk * h + tk, LANES * hd:LANES * hd + LANES]
            s = _nt_dot(q, k[:, LANES * hd:LANES * hd + LANES])
            return jnp.where(keep, s, NEG) if modes[h] == "diag" else s

        def consume(c, s):
            h, hd = combos[c]
            pr = hd // 2
            _softmax_step(s, v[:, LANES * pr:LANES * pr + LANES], m_ref, l_ref, acc_ref, N_HEADS * h + hd)

        _pipelined(len(combos), make_s, consume)

    def finalize():
        lane = lax.broadcasted_iota(jnp.int32, (tk, LANES), 1)
        for h in range(nh):
            for pr in range(2):
                c = N_HEADS * h + 2 * pr
                oe = acc_ref[c] / _row_sum(l_ref[c])
                oo = acc_ref[c + 1] / _row_sum(l_ref[c + 1])
                o_ref[0, tk * h:tk * h + tk, LANES * pr:LANES * pr + LANES] = jnp.where(
                    lane < HEAD_DIM, oe, oo).astype(BF16)

    _causal_sweep(i, j, nh, step, finalize)


def _fox_prompt(qa, ka, vb, *, tk, nh):
    b, t, _ = qa.shape
    tq = nh * tk
    qi, kj = _causal_pairs(t // tq, nh)
    q_map = lambda bi, p, qi, kj: (bi, qi[p], 0)
    k_map = lambda bi, p, qi, kj: (bi, kj[p], 0)
    sets = nh * N_HEADS
    return pl.pallas_call(
        functools.partial(_fox_kernel, tk=tk, nh=nh),
        grid_spec=pltpu.PrefetchScalarGridSpec(
            num_scalar_prefetch=2,
            grid=(b, qi.shape[0]),
            in_specs=[pl.BlockSpec((1, tq, 4 * LANES), q_map), pl.BlockSpec((1, tk, 4 * LANES), k_map),
                      pl.BlockSpec((1, tk, GROUP), k_map)],
            out_specs=pl.BlockSpec((1, tq, GROUP), q_map),
            scratch_shapes=[pltpu.VMEM((sets, tk, LANES), F32), pltpu.VMEM((sets, tk, LANES), F32),
                            pltpu.VMEM((sets, tk, LANES), F32)]),
        out_shape=jax.ShapeDtypeStruct((b, t, GROUP), BF16),
        compiler_params=_cparams(("parallel", "arbitrary")),
        name="fox_prompt",
    )(qi, kj, qa, ka, vb)


def _diff_lambda(lp, lam_init):
    a = jnp.sum(lp[0:1, :] * lp[1:2, :], axis=-1, keepdims=True)
    b = jnp.sum(lp[2:3, :] * lp[3:4, :], axis=-1, keepdims=True)
    return jnp.exp(a) - jnp.exp(b) + lam_init


def _diff_finish(acc_ref, l_ref, lam, gsub, lam_init, rows, o_ref, base=0, row0=0):
    lane = lax.broadcasted_iota(jnp.int32, (rows, LANES), 1)
    for pr in range(2):
        outs = []
        for e in range(2):
            c = base + 2 * (2 * pr + e)
            o = acc_ref[c] / _row_sum(l_ref[c]) - lam * (acc_ref[c + 1] / _row_sum(l_ref[c + 1]))
            valid = (lane >= HEAD_DIM * e) & (lane < HEAD_DIM * e + HEAD_DIM)
            ms = jnp.sum(jnp.where(valid, o * o, 0.0), axis=-1, keepdims=True) * (1.0 / HEAD_DIM)
            outs.append(o * lax.rsqrt(ms + EPS) * gsub * (1.0 - lam_init))
        o_ref[0, row0:row0 + rows, LANES * pr:LANES * pr + LANES] = jnp.where(
            lane < HEAD_DIM, outs[0], outs[1]).astype(BF16)


def _alibi_slope(hd):
    return 2.0 ** (-8.0 * (hd + 1) / N_HEADS) * LOG2E


def _alibi_features(tq):
    pos = np.arange(tq, dtype=np.float32)[:, None]
    lane = np.arange(LANES)[None, :]
    rnd = lambda v: v.astype(BF16).astype(np.float32)

    def split(v):
        v = v.astype(np.float32)
        hi = rnd(v)
        mid = rnd(v - hi)
        lo = rnd(v - hi - mid)
        return hi, mid, lo

    def block(e, first, second):
        out = np.zeros((tq, LANES), np.float32)
        for n, val in enumerate(first + second):
            out = np.where(lane == 8 * e + n, val, out)
        return out

    one = (np.ones((tq, 1), np.float32),) * 3
    qa = [block(hd % 2, split(-_alibi_slope(hd) * pos), one) for hd in range(N_HEADS)]
    ka = [block(0, one, split(_alibi_slope(2 * pr) * pos)) + block(1, one, split(_alibi_slope(2 * pr + 1) * pos))
          for pr in range(2)]
    return (jnp.asarray(np.concatenate(qa, axis=1), dtype=BF16),
            jnp.asarray(np.concatenate(ka, axis=1), dtype=BF16))


def _diff_kernel(qi_ref, kj_ref, q_ref, k_ref, v_ref, qa_ref, ka_ref, lam_ref, gsub_ref, o_ref,
                 m_ref, l_ref, acc_ref, *, tk, nh, lam_init):
    p = pl.program_id(1)
    i = qi_ref[p]
    j = kj_ref[p]
    per_half = 2 * N_HEADS

    @pl.when(j == 0)
    def _():
        _init_softmax_state(m_ref, l_ref, acc_ref)

    def step(modes):
        k = k_ref[0]
        v = v_ref[0]
        combos = [(h, c) for h in range(nh) if modes[h] != "skip" for c in range(per_half)]
        if "diag" in modes:
            row = lax.broadcasted_iota(jnp.int32, (tk, tk), 0)
            col = lax.broadcasted_iota(jnp.int32, (tk, tk), 1)
            seen = (col // CHUNK) <= (row // CHUNK)
            ahead = jnp.maximum(col - row, 0).astype(F32)
        gaps = [((nh * i + h - j) * tk).astype(F32) for h in range(nh)]

        def make_s(n):
            h, c = combos[n]
            hd, jm = divmod(c, 2)
            pr, e = divmod(hd, 2)
            idx = 4 * pr + 2 * e + jm
            qx = jnp.concatenate([q_ref[0, tk * h:tk * h + tk, LANES * idx:LANES * idx + LANES],
                                  qa_ref[:, LANES * hd:LANES * hd + LANES]], axis=1)
            kx = jnp.concatenate([k[:, LANES * pr:LANES * pr + LANES], ka_ref[:, LANES * pr:LANES * pr + LANES]],
                                 axis=1)
            s = _nt_dot(qx, kx)
            if modes[h] == "diag":
                s = jnp.where(seen, s - (2.0 * _alibi_slope(hd)) * ahead, NEG)
            return s

        def consume(n, s):
            h, c = combos[n]
            pr = c // 4
            shift = None if modes[h] == "diag" else _alibi_slope(c // 2) * gaps[h]
            _softmax_step(s, v[:, LANES * pr:LANES * pr + LANES], m_ref, l_ref, acc_ref, per_half * h + c, shift)

        _pipelined(len(combos), make_s, consume)

    def finalize():
        lam = _diff_lambda(lam_ref[...], lam_init)
        for h in range(nh):
            _diff_finish(acc_ref, l_ref, lam, gsub_ref[...], lam_init, tk, o_ref, base=per_half * h, row0=tk * h)

    _causal_sweep(i, j, nh, step, finalize)


def _diff_prompt(qv, kb, vb, lam_p, gsub, *, tk, nh, lam_init):
    b, t, _ = qv.shape
    tq = nh * tk
    qi, kj = _causal_pairs(t // tq, nh)
    qa, ka = _alibi_features(tk)
    q_map = lambda bi, p, qi, kj: (bi, qi[p], 0)
    k_map = lambda bi, p, qi, kj: (bi, kj[p], 0)
    const = lambda bi, p, qi, kj: (0, 0)
    sets = nh * 2 * N_HEADS
    return pl.pallas_call(
        functools.partial(_diff_kernel, tk=tk, nh=nh, lam_init=lam_init),
        grid_spec=pltpu.PrefetchScalarGridSpec(
            num_scalar_prefetch=2,
            grid=(b, qi.shape[0]),
            in_specs=[pl.BlockSpec((1, tq, 8 * LANES), q_map), pl.BlockSpec((1, tk, GROUP), k_map),
                      pl.BlockSpec((1, tk, GROUP), k_map),
                      pl.BlockSpec((tk, N_HEADS * LANES), const), pl.BlockSpec((tk, 2 * LANES), const),
                      pl.BlockSpec((4, DIFF_HALF), const), pl.BlockSpec((1, LANES), const)],
            out_specs=pl.BlockSpec((1, tq, GROUP), q_map),
            scratch_shapes=[pltpu.VMEM((sets, tk, LANES), F32), pltpu.VMEM((sets, tk, LANES), F32),
                            pltpu.VMEM((sets, tk, LANES), F32)]),
        out_shape=jax.ShapeDtypeStruct((b, t, GROUP), BF16),
        compiler_params=_cparams(("parallel", "arbitrary")),
        name="diff_prompt",
    )(qi, kj, qv, kb, vb, qa, ka, lam_p, gsub)


def _band_bias_kernel(g_ref, bp_ref, bs_ref):
    lane = lax.broadcasted_iota(jnp.int32, (CHUNK, BAND_WIN), 1)
    width = g_ref.shape[1]
    for hd in range(N_HEADS):
        x = jnp.broadcast_to(g_ref[hd:hd + 1, :], (CHUNK, width))
        rp = pltpu.roll(x, width - CHUNK + 1, axis=1, stride=1, stride_axis=0)
        bp_ref[hd] = jnp.where(lane < CHUNK, NEG, rp[:, :BAND_WIN] * LOG2E)
        rs = pltpu.roll(x, width - 2 * CHUNK + 1, axis=1, stride=1, stride_axis=0)
        bs_ref[hd] = rs[:, :BAND_WIN] * LOG2E


def _band_bias(table):
    gv = jnp.concatenate([jnp.broadcast_to(table[:, 2 * REL_CLIP:], (N_HEADS, 4 * LANES)),
                          table[:, 2 * REL_CLIP - 1::-1]], axis=1)
    shp = jax.ShapeDtypeStruct((N_HEADS, CHUNK, BAND_WIN), F32)
    return pl.pallas_call(_band_bias_kernel, out_shape=(shp, shp), name="band_bias")(gv)


def _band_kernel(q_ref, k_ref, v_ref, bias_ref, o_ref, kpad_ref, vpad_ref, *, tq):
    t = pl.program_id(1)
    n_sub = tq // CHUNK
    lane = lax.broadcasted_iota(jnp.int32, (CHUNK, LANES), 1)
    col = lax.broadcasted_iota(jnp.int32, (CHUNK, BAND_WIN), 1)

    @pl.when(t == 0)
    def _():
        for src, dst in ((k_ref, kpad_ref), (v_ref, vpad_ref)):
            dst[0:BAND_WIN, :] = jnp.zeros((BAND_WIN, GROUP), BF16)
            dst[BAND_WIN:, :] = src[0]

    def window(ref, c, pr):
        start = pl.multiple_of((t * n_sub + c + 1) * CHUNK, CHUNK)
        return ref[pl.ds(start, BAND_WIN), LANES * pr:LANES * pr + LANES]

    def make_s(idx):
        c, hd = divmod(idx, N_HEADS)
        q = q_ref[0, CHUNK * c:CHUNK * c + CHUNK, LANES * hd:LANES * hd + LANES]
        valid = col + (t * n_sub + c - BAND_CHUNKS - 1) * CHUNK >= 0
        return jnp.where(valid, _nt_dot(q, window(kpad_ref, c, hd // 2)) + bias_ref[hd], NEG)

    even = {}

    def consume(idx, s):
        c, hd = divmod(idx, N_HEADS)
        pr, e = divmod(hd, 2)
        m = jnp.max(s, axis=-1, keepdims=True)
        p = jnp.exp2(s - m)
        l = jnp.sum(p, axis=-1, keepdims=True)
        o = _dot(p.astype(BF16), window(vpad_ref, c, pr)) / l
        if e == 0:
            even[pr] = o
        else:
            o_ref[0, CHUNK * c:CHUNK * c + CHUNK, LANES * pr:LANES * pr + LANES] = jnp.where(
                lane < HEAD_DIM, even[pr], o).astype(BF16)

    _pipelined(n_sub * N_HEADS, make_s, consume, depth=BAND_LOOKAHEAD)


def _band_prompt(qm, kb, vb, bias, *, tq):
    b, t, _ = qm.shape
    return pl.pallas_call(
        functools.partial(_band_kernel, tq=tq),
        grid=(b, t // tq),
        in_specs=[pl.BlockSpec((1, tq, 4 * LANES), lambda bi, i: (bi, i, 0)),
                  pl.BlockSpec((1, t, GROUP), lambda bi, i: (bi, 0, 0)),
                  pl.BlockSpec((1, t, GROUP), lambda bi, i: (bi, 0, 0)),
                  pl.BlockSpec((N_HEADS, CHUNK, BAND_WIN), lambda bi, i: (0, 0, 0))],
        out_specs=pl.BlockSpec((1, tq, GROUP), lambda bi, i: (bi, i, 0)),
        out_shape=jax.ShapeDtypeStruct((b, t, GROUP), BF16),
        scratch_shapes=[pltpu.VMEM((BAND_WIN + t, GROUP), BF16), pltpu.VMEM((BAND_WIN + t, GROUP), BF16)],
        compiler_params=_cparams(("arbitrary", "arbitrary")),
        name="band_prompt",
    )(qm, kb, vb, bias)


def _mlstm_kernel(mqk_ref, mv_ref, gates_ref, cum_ref, mos_ref, cw_ref, cb_ref, gmh_ref, shead_ref, bd_ref,
                  c0_ref, n0_ref, m0_ref, conv0_ref,
                  o_ref, c_out_ref, n_out_ref, m_out_ref,
                  cbuf_ref, c_ref, n_ref, m_ref, *, rows, valid, ns):
    t = pl.program_id(1)

    @pl.when(t == 0)
    def _():
        for s in range(ns):
            c_ref[s] = c0_ref[s]
            n_ref[s] = n0_ref[s]
            m_ref[s] = m0_ref[s]
            cbuf_ref[s, 0:8, :] = conv0_ref[s]

    for s in range(ns):
        _mlstm_chunk(mqk_ref.at[s], mv_ref.at[s], gates_ref.at[s], cum_ref.at[s], mos_ref.at[s], cw_ref, cb_ref,
                     gmh_ref, shead_ref, bd_ref, o_ref.at[s], cbuf_ref.at[s], c_ref.at[s], n_ref.at[s],
                     m_ref.at[s], rows=rows, valid=valid)

    @pl.when(t == pl.num_programs(1) - 1)
    def _():
        for s in range(ns):
            c_out_ref[s] = c_ref[s]
            n_out_ref[s] = n_ref[s]
            m_out_ref[s] = m_ref[s]


def _mlstm_chunk(mqk_ref, mv_ref, gates_ref, cum_ref, mos_ref, cw_ref, cb_ref, gmh_ref, shead_ref, bd_ref,
                 o_ref, cbuf_ref, c_ref, n_ref, m_ref, *, rows, valid):
    def padded(a, fill=0.0):
        if valid == rows:
            return a
        return jnp.concatenate([a, jnp.full((rows - valid, a.shape[1]), fill, a.dtype)], axis=0)

    u = padded(mqk_ref[...])
    cbuf_ref[8:8 + rows, :] = u
    y = cb_ref[...] + cw_ref[3:4, :] * u
    for jw in range(CONV_W - 1):
        y = y + cw_ref[jw:jw + 1, :] * cbuf_ref[5 + jw:5 + jw + rows, :]
    cbuf_ref[0:8, :] = cbuf_ref[valid:valid + 8, :]
    qk = y * jax.nn.sigmoid(y)
    q = qk[:, 0:GROUP]
    k = qk[:, GROUP:2 * GROUP] * (HEAD_DIM ** -0.5)
    qb = q.astype(BF16)
    kb = k.astype(BF16)
    v = padded(mv_ref[...])
    mos = padded(mos_ref[...])

    g = gates_ref[...]
    cm = cum_ref[...]
    if valid != rows:
        g = padded(g, NEG)
        cm = jnp.concatenate([cm, jnp.broadcast_to(cm[valid - 1:valid, :], (rows - valid, LANES))], axis=0)
    g_t = g.T
    cm_t = cm.T

    row = lax.broadcasted_iota(jnp.int32, (rows, rows), 0)
    col = lax.broadcasted_iota(jnp.int32, (rows, rows), 1)
    causal = col <= row
    lane_g = lax.broadcasted_iota(jnp.int32, (rows, GROUP), 1)
    lane_r = lax.broadcasted_iota(jnp.int32, (1, GROUP), 1)
    m_prev_all = m_ref[...]

    zeros = jnp.zeros((rows, GROUP), F32)
    num, a_full, ws_full, mt_full, wend_full = zeros, zeros, zeros, zeros, zeros
    decay_lane = jnp.zeros((1, GROUP), F32)
    mnew_lane = jnp.zeros((1, GROUP), F32)
    head_masks = [(lane_g >= HEAD_DIM * hd) & (lane_g < HEAD_DIM * hd + HEAD_DIM) for hd in range(N_HEADS)]
    sqks = [_nt_dot(jnp.where(hm, q, 0.0).astype(BF16), kb) for hm in head_masks]
    c_old = c_ref[...]
    n_old = n_ref[...]
    shead = shead_ref[...]
    q_c = _dot(qb, c_old.astype(BF16))
    q_n = _dot((q * n_old).astype(BF16), shead)
    for hd in range(N_HEADS):
        hm = head_masks[hd]
        hm_r = (lane_r >= HEAD_DIM * hd) & (lane_r < HEAD_DIM * hd + HEAD_DIM)
        f_c = cm[:, 2 * N_HEADS + hd:2 * N_HEADS + hd + 1]
        ig_c = g[:, N_HEADS + hd:N_HEADS + hd + 1]
        f_r = cm_t[2 * N_HEADS + hd:2 * N_HEADS + hd + 1, :]
        ig_r = g_t[N_HEADS + hd:N_HEADS + hd + 1, :]
        m_prev = m_prev_all[:, HEAD_DIM * hd:HEAD_DIM * hd + 1]
        logw = jnp.where(causal, f_c - (f_r - ig_r), NEG)
        logb = f_c + m_prev
        m_t = jnp.maximum(logb, jnp.max(logw, axis=-1, keepdims=True))
        w = jnp.exp(logw - m_t) * sqks[hd]
        a = jnp.exp(logb - m_t)
        num = jnp.where(hm, _dot(w.astype(BF16), v), num)
        a_full = jnp.where(hm, a, a_full)
        ws_full = jnp.where(hm, jnp.sum(w, axis=-1, keepdims=True), ws_full)
        mt_full = jnp.where(hm, m_t, mt_full)
        f_end = f_c[rows - 1:rows, :]
        log_end = f_end - f_c + ig_c
        m_new = jnp.maximum(f_end + m_prev, jnp.max(log_end, axis=0, keepdims=True))
        wend_full = jnp.where(hm, jnp.exp(log_end - m_new), wend_full)
        decay_lane = jnp.where(hm_r, jnp.exp(f_end + m_prev - m_new), decay_lane)
        mnew_lane = jnp.where(hm_r, m_new, mnew_lane)

    num = num + a_full * q_c
    den = ws_full + a_full * q_n
    h = num / jnp.maximum(jnp.abs(den), jnp.exp(-mt_full))
    o = mos * h
    ss = _dot((o * o).astype(BF16), shead)
    on = o * lax.rsqrt(ss * (1.0 / HEAD_DIM) + EPS) * gmh_ref[...]
    o_ref[...] = on[0:valid, :].astype(BF16)

    wk = k * wend_full
    kv = lax.dot_general(wk.astype(BF16), v, (((0,), (0,)), ((), ())), preferred_element_type=F32)
    c_ref[...] = c_old * decay_lane + jnp.where(bd_ref[...] > 0.5, kv, 0.0)
    n_ref[...] = n_old * decay_lane + jnp.sum(wk, axis=0, keepdims=True)
    m_ref[...] = mnew_lane


def _mlstm(mqk, mvb, gates, cum, mos, cw, cb, gmh, shead, bd, c0, n0, m0, conv0, *, nb, t, valid, rows, ns):
    nc = t // valid
    seq = lambda a: a.reshape(nb, t, a.shape[-1])
    tok = lambda bi, i: (bi, i, 0)
    const = lambda bi, i: (0, 0)
    per_b = lambda bi, i: (bi, 0, 0)
    o, c_new, n_new, m_new = pl.pallas_call(
        functools.partial(_mlstm_kernel, rows=rows, valid=valid, ns=ns),
        grid=(nb // ns, nc),
        in_specs=[pl.BlockSpec((ns, valid, 2 * GROUP), tok), pl.BlockSpec((ns, valid, GROUP), tok),
                  pl.BlockSpec((ns, valid, LANES), tok), pl.BlockSpec((ns, valid, LANES), tok),
                  pl.BlockSpec((ns, valid, GROUP), tok),
                  pl.BlockSpec((CONV_W, 2 * GROUP), const), pl.BlockSpec((1, 2 * GROUP), const),
                  pl.BlockSpec((1, GROUP), const), pl.BlockSpec((GROUP, GROUP), const),
                  pl.BlockSpec((GROUP, GROUP), const),
                  pl.BlockSpec((ns, GROUP, GROUP), per_b), pl.BlockSpec((ns, 1, GROUP), per_b),
                  pl.BlockSpec((ns, 1, GROUP), per_b), pl.BlockSpec((ns, 8, 2 * GROUP), per_b)],
        out_specs=[pl.BlockSpec((ns, valid, GROUP), tok), pl.BlockSpec((ns, GROUP, GROUP), per_b),
                   pl.BlockSpec((ns, 1, GROUP), per_b), pl.BlockSpec((ns, 1, GROUP), per_b)],
        out_shape=[jax.ShapeDtypeStruct((nb, t, GROUP), BF16), jax.ShapeDtypeStruct((nb, GROUP, GROUP), F32),
                   jax.ShapeDtypeStruct((nb, 1, GROUP), F32), jax.ShapeDtypeStruct((nb, 1, GROUP), F32)],
        scratch_shapes=[pltpu.VMEM((ns, 8 + rows, 2 * GROUP), F32), pltpu.VMEM((ns, GROUP, GROUP), F32),
                        pltpu.VMEM((ns, 1, GROUP), F32), pltpu.VMEM((ns, 1, GROUP), F32)],
        compiler_params=_cparams(("parallel", "arbitrary")),
        name="mlstm",
    )(seq(mqk), seq(mvb), seq(gates), seq(cum), seq(mos), cw, cb, gmh, shead, bd, c0, n0, m0, conv0)
    return o.reshape(nb * t, GROUP), c_new, n_new, m_new


def _pad_rows(a, rows):
    return jnp.concatenate([a, jnp.zeros((rows - a.shape[0], a.shape[1]), a.dtype)], axis=0)


def _two_part_softmax(s_c, s_n, vt_c, v_n):
    m = jnp.maximum(jnp.max(s_c, axis=-1, keepdims=True), jnp.max(s_n, axis=-1, keepdims=True))
    p_c = jnp.exp2(s_c - m)
    p_n = jnp.exp2(s_n - m)
    l = jnp.sum(p_c, axis=-1, keepdims=True) + jnp.sum(p_n, axis=-1, keepdims=True)
    return _nt_dot(p_c.astype(BF16), vt_c) + _dot(p_n.astype(BF16), v_n), l


def _fox_sample_kernel(q_ref, kn_ref, vn_ref, ck_ref, cv_ref, clf_ref, lstr_ref, o_ref, *, tn, past):
    q = q_ref[0]
    kn = _pad_rows(kn_ref[0], LANES)
    vn = _pad_rows(vn_ref[0], LANES)
    ck = ck_ref[0, 0]
    cv = cv_ref[0, 0].astype(BF16)
    x = _pad_rows(clf_ref[0, 0], 8)
    xhi, xmid, xlo = _split3(x)
    lstr = lstr_ref[...]
    suf = _dot(xhi, lstr) + _dot(xmid, lstr) + _dot(xlo, lstr)

    sub = lax.broadcasted_iota(jnp.int32, (8, past), 0)
    lane_o = lax.broadcasted_iota(jnp.int32, (tn, LANES), 1)
    row = lax.broadcasted_iota(jnp.int32, (tn, LANES), 0)
    causal = lane_o <= row
    fill = jnp.zeros((LANES - HEAD_DIM - 8, past), F32)

    def logits(hd):
        sb = jnp.broadcast_to(suf[hd:hd + 1, :], (8, past)) * LOG2E
        shi = sb.astype(BF16).astype(F32)
        r = sb - shi
        smid = r.astype(BF16).astype(F32)
        slo = (r - smid).astype(BF16).astype(F32)
        aug = jnp.where(sub < 3, 1.0,
                        jnp.where(sub == 3, shi, jnp.where(sub == 4, smid, jnp.where(sub == 5, slo, 0.0))))
        kc = jnp.concatenate([ck[HEAD_DIM * hd:HEAD_DIM * hd + HEAD_DIM, :], aug, fill], axis=0).astype(BF16)
        qh = q[:, LANES * hd:LANES * hd + LANES]
        return _dot(qh, kc), jnp.where(causal, _nt_dot(qh, kn[:, LANES * hd:LANES * hd + LANES]), NEG)

    even = {}

    def attend(hd, s):
        pr, e = divmod(hd, 2)
        acc, l = _two_part_softmax(s[0], s[1], cv[LANES * pr:LANES * pr + LANES, :],
                                   vn[:, LANES * pr:LANES * pr + LANES])
        if e == 0:
            even[pr] = acc / l
        else:
            o_ref[0, :, LANES * pr:LANES * pr + LANES] = jnp.where(lane_o < HEAD_DIM, even[pr], acc / l).astype(BF16)

    _pipelined(N_HEADS, logits, attend, depth=SAMPLE_LOOKAHEAD)


def _fox_sample(qa, ka, vb, ck, cv, clf, lstr, *, layer):
    b, tn, _ = qa.shape
    past = ck.shape[3]
    per_b = lambda i: (i, 0, 0)
    cache_b = lambda i: (layer, i, 0, 0)
    return pl.pallas_call(
        functools.partial(_fox_sample_kernel, tn=tn, past=past),
        grid=(b,),
        in_specs=[pl.BlockSpec((1, tn, 4 * LANES), per_b), pl.BlockSpec((1, tn, 4 * LANES), per_b),
                  pl.BlockSpec((1, tn, GROUP), per_b), pl.BlockSpec((1, 1, GROUP, past), cache_b),
                  pl.BlockSpec((1, 1, GROUP, past), cache_b), pl.BlockSpec((1, 1, N_HEADS, past), cache_b),
                  pl.BlockSpec((past, past), lambda i: (0, 0))],
        out_specs=pl.BlockSpec((1, tn, GROUP), per_b),
        out_shape=jax.ShapeDtypeStruct((b, tn, GROUP), BF16),
        compiler_params=_cparams(("parallel",)),
        name="fox_sample",
    )(qa, ka, vb, ck, cv, clf, lstr)


def _diff_sample_kernel(q_ref, kn_ref, vn_ref, ck_ref, cv_ref, lam_ref, gsub_ref, o_ref, l_ref, acc_ref,
                        *, tn, past, lam_init):
    q = q_ref[0]
    kn = _pad_rows(kn_ref[0], LANES)
    vn = _pad_rows(vn_ref[0], LANES)
    ck = ck_ref[0, 0].astype(BF16)
    cv = cv_ref[0, 0].astype(BF16)
    row_c = lax.broadcasted_iota(jnp.int32, (tn, past), 0)
    col_c = lax.broadcasted_iota(jnp.int32, (tn, past), 1)
    dist_c = (past + row_c - col_c).astype(F32)
    row_n = lax.broadcasted_iota(jnp.int32, (tn, LANES), 0)
    col_n = lax.broadcasted_iota(jnp.int32, (tn, LANES), 1)
    dist_n = jnp.abs(row_n - col_n).astype(F32)
    real = col_n < tn

    def logits(c):
        hd, jm = divmod(c, 2)
        pr, e = divmod(hd, 2)
        slope = _alibi_slope(hd)
        idx = 4 * pr + 2 * e + jm
        qh = q[:, LANES * idx:LANES * idx + LANES]
        s_c = _dot(qh, ck[LANES * pr:LANES * pr + LANES, :]) - slope * dist_c
        s_n = jnp.where(real, _nt_dot(qh, kn[:, LANES * pr:LANES * pr + LANES]) - slope * dist_n, NEG)
        return s_c, s_n

    def attend(c, s):
        pr = c // 4
        acc, l = _two_part_softmax(s[0], s[1], cv[LANES * pr:LANES * pr + LANES, :],
                                   vn[:, LANES * pr:LANES * pr + LANES])
        acc_ref[c] = acc
        l_ref[c] = l

    _pipelined(2 * N_HEADS, logits, attend, depth=SAMPLE_LOOKAHEAD)
    lam = _diff_lambda(lam_ref[...], lam_init)
    _diff_finish(acc_ref, l_ref, lam, gsub_ref[...], lam_init, tn, o_ref)


def _diff_sample(qv, kb, vb, ck, cv, lam_p, gsub, *, lam_init, layer):
    b, tn, _ = qv.shape
    past = ck.shape[3]
    per_b = lambda i: (i, 0, 0)
    cache_b = lambda i: (layer, i, 0, 0)
    return pl.pallas_call(
        functools.partial(_diff_sample_kernel, tn=tn, past=past, lam_init=lam_init),
        grid=(b,),
        in_specs=[pl.BlockSpec((1, tn, 8 * LANES), per_b), pl.BlockSpec((1, tn, GROUP), per_b),
                  pl.BlockSpec((1, tn, GROUP), per_b), pl.BlockSpec((1, 1, GROUP, past), cache_b),
                  pl.BlockSpec((1, 1, GROUP, past), cache_b), pl.BlockSpec((4, DIFF_HALF), lambda i: (0, 0)),
                  pl.BlockSpec((1, LANES), lambda i: (0, 0))],
        out_specs=pl.BlockSpec((1, tn, GROUP), per_b),
        out_shape=jax.ShapeDtypeStruct((b, tn, GROUP), BF16),
        scratch_shapes=[pltpu.VMEM((2 * N_HEADS, tn, 1), F32), pltpu.VMEM((2 * N_HEADS, tn, LANES), F32)],
        compiler_params=_cparams(("parallel",)),
        name="diff_sample",
    )(qv, kb, vb, ck, cv, lam_p, gsub)


def _band_sample_kernel(q_ref, kn_ref, vn_ref, ck_ref, cv_ref, bias_ref, o_ref, *, tn, past):
    q = q_ref[0]
    kn = _pad_rows(kn_ref[0], LANES)
    vn = _pad_rows(vn_ref[0], LANES)
    ck = ck_ref[0, 0].astype(BF16)
    cv = cv_ref[0, 0].astype(BF16)
    lane = lax.broadcasted_iota(jnp.int32, (tn, LANES), 1)
    real = lane < tn

    def logits(hd):
        pr = hd // 2
        qh = q[:, LANES * hd:LANES * hd + LANES]
        s_c = _dot(qh, ck[LANES * pr:LANES * pr + LANES, :]) + bias_ref[hd, 0:tn, 0:past]
        s_n = _nt_dot(qh, kn[:, LANES * pr:LANES * pr + LANES]) + bias_ref[hd, 0:tn, past:past + LANES]
        return s_c, jnp.where(real, s_n, NEG)

    even = {}

    def attend(hd, s):
        pr, e = divmod(hd, 2)
        acc, l = _two_part_softmax(s[0], s[1], cv[LANES * pr:LANES * pr + LANES, :],
                                   vn[:, LANES * pr:LANES * pr + LANES])
        if e == 0:
            even[pr] = acc / l
        else:
            o_ref[0, :, LANES * pr:LANES * pr + LANES] = jnp.where(lane < HEAD_DIM, even[pr], acc / l).astype(BF16)

    _pipelined(N_HEADS, logits, attend, depth=SAMPLE_LOOKAHEAD)


def _band_sample(qm, kb, vb, ck, cv, bias, *, layer):
    b, tn, _ = qm.shape
    past = ck.shape[3]
    per_b = lambda i: (i, 0, 0)
    cache_b = lambda i: (layer, i, 0, 0)
    return pl.pallas_call(
        functools.partial(_band_sample_kernel, tn=tn, past=past),
        grid=(b,),
        in_specs=[pl.BlockSpec((1, tn, 4 * LANES), per_b), pl.BlockSpec((1, tn, GROUP), per_b),
                  pl.BlockSpec((1, tn, GROUP), per_b), pl.BlockSpec((1, 1, GROUP, past), cache_b),
                  pl.BlockSpec((1, 1, GROUP, past), cache_b),
                  pl.BlockSpec((N_HEADS, CHUNK, BAND_WIN), lambda i: (0, 0, 0))],
        out_specs=pl.BlockSpec((1, tn, GROUP), per_b),
        out_shape=jax.ShapeDtypeStruct((b, tn, GROUP), BF16),
        compiler_params=_cparams(("parallel",)),
        name="band_sample",
    )(qm, kb, vb, ck, cv, bias)


def _ffn_kernel(x_ref, gt1_ref, fox_ref, diff_ref, ml_ref, band_ref, wo_ref, sh_ref, sc_ref, gt_ref, g2_ref,
                wg_ref, wu_ref, wd_ref, o_ref, acc_ref, *, bb, tt, tf):
    tm = bb * tt
    mix = _dot(fox_ref[...], wo_ref[0, 0:GROUP, :])
    mix += _dot(diff_ref[...], wo_ref[0, GROUP:2 * GROUP, :])
    mix += _dot(ml_ref[...], wo_ref[0, 2 * GROUP:3 * GROUP, :])
    mix += _dot(band_ref[...], wo_ref[0, 3 * GROUP:4 * GROUP, :])
    x = x_ref[...] + gt1_ref[...] * mix.reshape(bb, tt, D_MODEL)
    ms = jnp.mean(x * x, axis=-1, keepdims=True)
    h = x * lax.rsqrt(ms + EPS) * g2_ref[...]
    h = h * (1.0 + sc_ref[...]) + sh_ref[...]
    hb = h.reshape(tm, D_MODEL).astype(BF16)

    def gate_up(f):
        cols = slice(tf * f, tf * f + tf)
        return _dot(hb, wg_ref[0, :, cols]), _dot(hb, wu_ref[0, :, cols])

    def down(f, gu):
        g, u = gu
        a = (g * jax.nn.sigmoid(g) * u).astype(BF16)
        part = _dot(a, wd_ref[0, tf * f:tf * f + tf, :])
        if f == 0:
            acc_ref[...] = part
        else:
            acc_ref[...] += part

    _pipelined(D_FF // tf, gate_up, down)
    o_ref[...] = x + gt_ref[...] * acc_ref[...].reshape(bb, tt, D_MODEL)


def _out_ffn(x, gt1, o_fox, o_diff, o_ml, o_band, w_out, sh, sc, gt, g2, wg, wu, wd, *, bb, tt, tf, layer):
    bx, tx, d = x.shape
    nb, nt = bx // bb, tx // tt
    tm = bb * tt
    xs = pl.BlockSpec((bb, tt, d), lambda i, t: (i, t, 0))
    ms = pl.BlockSpec((bb, 1, d), lambda i, t: (i, 0, 0))
    mixer = pl.BlockSpec((tm, GROUP), lambda i, t: (i * nt + t, 0))
    resident = lambda shape: pl.BlockSpec((1,) + shape, lambda i, t: (layer, 0, 0), pipeline_mode=pl.Buffered(1))
    return pl.pallas_call(
        functools.partial(_ffn_kernel, bb=bb, tt=tt, tf=tf),
        grid=(nb, nt),
        in_specs=[xs, ms, mixer, mixer, mixer, mixer, resident((d, d)),
                  ms, ms, ms, pl.BlockSpec((1, d), lambda i, t: (0, 0)),
                  resident((d, D_FF)), resident((d, D_FF)), resident((D_FF, d))],
        out_specs=xs,
        out_shape=jax.ShapeDtypeStruct(x.shape, F32),
        scratch_shapes=[pltpu.VMEM((tm, d), F32)],
        compiler_params=_cparams(("parallel", "parallel")),
        name="out_ffn",
    )(x, gt1, o_fox, o_diff, o_ml, o_band, w_out, sh, sc, gt, g2, wg, wu, wd)


def _consts(seg):
    r = np.arange(GROUP)
    s64 = (r[:, None] // HEAD_DIM == r[None, :] // HEAD_DIM)
    s32 = (r[:, None] // DIFF_HALF == r[None, :] // DIFF_HALF)
    q = np.arange(LANES)
    tril = (q[None, :] <= q[:, None]) & (q[:, None] // seg == q[None, :] // seg)
    as_bf16 = lambda m: jnp.asarray(m.astype(np.float32), dtype=BF16)
    return as_bf16(s64), as_bf16(s32), as_bf16(tril), jnp.asarray(s64.astype(np.float32))


def _prep_layer(l, norm1_g, norm2_g, w_in_g, b_in, qk_g_fox, qk_g_diff, qk_g_band, conv_w, conv_b,
                diff_lambda, diff_subln_g, mlstm_norm_g, band_rel_bias, w_out, w_ffn_gate, w_ffn_up,
                w_ffn_down):
    starts = np.concatenate([[0], np.cumsum(IN_SPLIT_SIZES)]).tolist()
    order = list(_FULL_GROUPS) + list(_GATE_GROUPS)
    bl = b_in[l]
    b_cols = [bl[starts[g]:starts[g + 1]] for g in order]
    pad = N_IN_PAD - N_FULL - N_GATES
    b_cols.append(jnp.zeros((pad,), F32))
    gains = jnp.stack([
        jnp.tile(qk_g_fox[l, 0], N_HEADS), jnp.tile(qk_g_fox[l, 1], N_HEADS),
        jnp.tile(qk_g_diff[l, 0], 2 * N_HEADS), jnp.tile(qk_g_diff[l, 1], 2 * N_HEADS),
        jnp.tile(qk_g_band[l, 0], N_HEADS), jnp.tile(qk_g_band[l, 1], N_HEADS),
        jnp.zeros((GROUP,), F32), jnp.zeros((GROUP,), F32)])
    return dict(
        g1=norm1_g[l].reshape(1, D_MODEL), g2=norm2_g[l].reshape(1, D_MODEL),
        w_in=w_in_g,
        b_in=jnp.concatenate(b_cols).reshape(1, N_IN_PAD),
        gains=gains, conv_w=conv_w[l], conv_b=conv_b[l].reshape(1, 2 * GROUP),
        lam_p=diff_lambda[l], gsub=jnp.tile(diff_subln_g[l], 2).reshape(1, LANES),
        gmh=jnp.tile(mlstm_norm_g[l], N_HEADS).reshape(1, GROUP),
        table=band_rel_bias[l],
        w_out=w_out, wg=w_ffn_gate, wu=w_ffn_up, wd=w_ffn_down,
        lam_init=0.8 - 0.6 * math.exp(-0.3 * l))


def _layer(x, mod, lp, caches, *, bb, tt, layer, depth, prev_state=None):
    bx, tx, _ = x.shape
    n = bx * tx
    prompt = caches is None
    sh1, sc1, gt1, sh2, sc2, gt2 = [mod[:, i:i + 1, :] for i in range(6)]
    s64, s32, tril, bd = _consts(tx if prompt else min(tx, LANES))
    z, big = _in_proj(x, sh1, sc1, lp["g1"], lp["w_in"], lp["b_in"], lp["gains"], s64, s32, tril,
                      bb=bb, tt=tt, running=prompt, ml_blocks=ML_CHUNK // LANES,
                      layer=layer, depth=depth, prev_state=prev_state)
    r3 = lambda a: a.reshape(bx, tx, a.shape[-1])
    bias_p, bias_s = _band_bias(lp["table"])

    if prompt:
        tq = 512
        nh = Q_SUBTILES if tx % (Q_SUBTILES * tq) == 0 else 1
        o_fox = _fox_prompt(r3(z["fqa"]), r3(z["fka"]), r3(z["fvb"]), tk=tq, nh=nh)
        o_diff = _diff_prompt(r3(z["dqv"]), r3(z["dkb"]), r3(z["dvb"]), lp["lam_p"], lp["gsub"],
                              tk=tq, nh=nh, lam_init=lp["lam_init"])
        o_band = _band_prompt(r3(z["bqm"]), r3(z["bkb"]), r3(z["bvb"]), bias_p, tq=tq)
        c0 = jnp.zeros((bx, GROUP, GROUP), F32)
        n0 = jnp.zeros((bx, 1, GROUP), F32)
        m0 = jnp.zeros((bx, 1, GROUP), F32)
        conv0 = jnp.zeros((bx, 8, 2 * GROUP), F32)
        ml_valid, ml_rows = ML_CHUNK, ML_CHUNK
    else:
        (c_fk, c_fv, c_flf, c_dk, c_dv, c_bk, c_bv, s_c, s_n, s_m, s_conv) = caches
        past = c_fk.shape[3]
        assert past % CHUNK == 0 and tx <= CHUNK
        jj = np.arange(past)
        lstr = jnp.asarray((jj[:, None] > jj[None, :]).astype(np.float32), dtype=BF16)
        o_fox = _fox_sample(r3(z["fqa"]), r3(z["fka"]), r3(z["fvb"]), c_fk, c_fv, c_flf, lstr, layer=layer)
        o_diff = _diff_sample(r3(z["dqv"]), r3(z["dkb"]), r3(z["dvb"]), c_dk, c_dv,
                              lp["lam_p"], lp["gsub"], lam_init=lp["lam_init"], layer=layer)
        o_band = _band_sample(r3(z["bqm"]), r3(z["bkb"]), r3(z["bvb"]), c_bk, c_bv, bias_s, layer=layer)
        eye = jnp.eye(N_HEADS, dtype=F32)
        c0 = (s_c[:, :, :, None, :] * eye[None, :, None, :, None]).reshape(bx, GROUP, GROUP)
        n0 = s_n.reshape(bx, 1, GROUP)
        m0 = jnp.repeat(s_m, HEAD_DIM, axis=-1).reshape(bx, 1, GROUP)
        conv0 = jnp.pad(s_conv, ((0, 0), (8 - (CONV_W - 1), 0), (0, 0)))
        ml_valid, ml_rows = tx, ML_PAD
    o_ml, c_new, n_new, m_new = _mlstm(
        z["mqk"], z["mvb"], z["gates"], z["cum"], z["mos"], lp["conv_w"], lp["conv_b"], lp["gmh"],
        s64, bd, c0, n0, m0, conv0, nb=bx, t=tx, valid=ml_valid, rows=ml_rows, ns=min(bx, ML_SEQS))

    flat2 = lambda a: a.reshape(n, GROUP)
    x2 = _out_ffn(x, gt1, flat2(o_fox), flat2(o_diff), o_ml, flat2(o_band), lp["w_out"], sh2, sc2, gt2,
                  lp["g2"], lp["wg"], lp["wu"], lp["wd"], bb=bb, tt=tt, tf=FFN_CHUNK, layer=layer)

    mc = jnp.stack([c_new[:, HEAD_DIM * h:HEAD_DIM * h + HEAD_DIM, HEAD_DIM * h:HEAD_DIM * h + HEAD_DIM]
                    for h in range(N_HEADS)], axis=1)
    small = (r3(z["gates"])[:, :, 0:N_HEADS], mc, n_new.reshape(bx, N_HEADS, HEAD_DIM),
             m_new.reshape(bx, N_HEADS, HEAD_DIM)[:, :, 0], r3(z["mqk"])[:, tx - (CONV_W - 1):, :])
    return x2, big, small


def _channel_major(c):
    perm = (0, 1) + tuple(range(3, c.ndim)) + (2,)
    return jnp.transpose(c, perm).reshape(c.shape[0], c.shape[1], -1, c.shape[2])


def _token_major(a, inner):
    d, b, _, t = a.shape
    k = len(inner)
    return jnp.transpose(a.reshape(d, b, *inner, t), (0, 1, 2 + k) + tuple(range(2, 2 + k)))


def kernel(x_prompt, x_sample, c_prompt, c_sample, cache_fox_k, cache_fox_v, cache_fox_logf, cache_diff_k, cache_diff_v, cache_band_k, cache_band_v, state_mlstm_c, state_mlstm_n, state_mlstm_m, state_conv, norm1_g, norm2_g, w_mod, b_mod, w_in, b_in, qk_g_fox, qk_g_diff, qk_g_band, conv_w, conv_b, diff_lambda, diff_subln_g, mlstm_norm_g, band_rel_bias, w_out, w_ffn_gate, w_ffn_up, w_ffn_down):
    depth = w_in.shape[0]
    bp, bs = x_prompt.shape[0], x_sample.shape[0]
    ts = x_sample.shape[1]
    caches = tuple(_channel_major(c) for c in (cache_fox_k, cache_fox_v, cache_fox_logf, cache_diff_k,
                                               cache_diff_v, cache_band_k, cache_band_v))
    caches += (state_mlstm_c, state_mlstm_n, state_mlstm_m, state_conv)
    mod = _modulation(jnp.concatenate([c_prompt, c_sample], axis=0), w_mod, b_mod)
    mod = mod.reshape(depth, bp + bs, 6, D_MODEL)
    w_in_g = _regroup_w_in(w_in)
    big_w = [w.astype(BF16) for w in (w_out, w_ffn_gate, w_ffn_up, w_ffn_down)]
    layers = [_prep_layer(l, norm1_g, norm2_g, w_in_g, b_in, qk_g_fox, qk_g_diff, qk_g_band, conv_w,
                          conv_b, diff_lambda, diff_subln_g, mlstm_norm_g, band_rel_bias, *big_w)
              for l in range(depth)]
    sample_bb = 512 // ts
    xp, xs = x_prompt, x_sample
    p_big, p_small, s_big, s_small = None, [], [], []
    for l in range(depth):
        xp, p_big, sm = _layer(xp, mod[l, :bp], layers[l], None, bb=1, tt=512, layer=l, depth=depth,
                               prev_state=p_big)
        p_small.append(sm)
    for l in range(depth):
        xs, big, sm = _layer(xs, mod[l, bp:], layers[l], caches[:7] + tuple(c[l] for c in caches[7:]),
                             bb=sample_bb, tt=ts,
                             layer=l, depth=depth)
        s_big.append(big)
        s_small.append(sm)
    head = (N_HEADS, HEAD_DIM)
    half = (N_HEADS, 2, DIFF_HALF)
    inner = (head, head, half, head, head, head)
    p_fk, p_fv, p_dk, p_dv, p_bk, p_bv = [_token_major(a, inn) for a, inn in zip(p_big, inner)]
    p_flf, p_c, p_n, p_m, p_conv = [jnp.stack(zs) for zs in zip(*p_small)]
    s_fk, s_fv, s_dk, s_dv, s_bk, s_bv = [jnp.stack(zs).reshape((depth, bs, ts) + inn)
                                          for zs, inn in zip(zip(*s_big), inner)]
    s_flf, s_c, s_n, s_m, s_conv = [jnp.stack(zs) for zs in zip(*s_small)]
    return (xp, xs, p_fk, p_fv, p_flf, p_dk, p_dv, p_bk, p_bv, p_c, p_n, p_m, p_conv,
            s_fk, s_fv, s_flf, s_dk, s_dv, s_bk, s_bv, s_c, s_n, s_m, s_conv)
```

```python
import functools
import math

import numpy as np
import jax
import jax.numpy as jnp
from jax import lax
from jax.experimental import pallas as pl
from jax.experimental.pallas import tpu as pltpu

F32 = jnp.float32
BF16 = jnp.bfloat16

D_MODEL = 1024
HEAD_DIM = 64
N_HEADS = 4
GROUP = N_HEADS * HEAD_DIM
DIFF_HALF = HEAD_DIM // 2
CHUNK = 64
BAND_CHUNKS = 8
BAND_ROWS = BAND_CHUNKS * CHUNK
REL_CLIP = 128
CONV_W = 4
D_FF = 2816
EPS = 1e-6
NEG = -1e30
LOG2E = math.log2(math.e)

LANES = 128
PAIR = 2 * HEAD_DIM
N_GATES = 3 * N_HEADS
BAND_WIN = (BAND_CHUNKS + 2) * CHUNK
ML_CHUNK = 256
ML_PAD = 128
ML_SEQS = 4
BAND_LOOKAHEAD = 5
SAMPLE_LOOKAHEAD = 2
SAMPLE_SEQS = 4
Q_SUBTILES = 2
FFN_CHUNK = 256
VMEM_LIMIT = 56 * 1024 * 1024

IN_SPLIT_SIZES = (GROUP, GROUP, GROUP, N_HEADS, GROUP, GROUP, GROUP, 2 * GROUP, GROUP,
                  N_HEADS, N_HEADS, GROUP, GROUP, GROUP, GROUP)
_FULL_GROUPS = (0, 1, 2, 4, 5, 6, 7, 8, 11, 12, 13, 14)
_GATE_GROUPS = (3, 9, 10)
N_FULL = 13 * GROUP
N_IN_PAD = N_FULL + LANES


def _cparams(sem):
    return pltpu.CompilerParams(dimension_semantics=sem, vmem_limit_bytes=VMEM_LIMIT)


def _nt_dot(a, b):
    return lax.dot_general(a, b, (((1,), (1,)), ((), ())), preferred_element_type=F32)


def _dot(a, b):
    return jnp.dot(a, b, preferred_element_type=F32)


def _split3(x):
    hi = x.astype(BF16)
    r = x - hi.astype(F32)
    mid = r.astype(BF16)
    lo = (r - mid.astype(F32)).astype(BF16)
    return hi, mid, lo


def _log_sigmoid(x):
    return jnp.minimum(x, 0.0) - jnp.log1p(jnp.exp(-jnp.abs(x)))


def _mod_kernel(c_ref, w_ref, b_ref, o_ref):
    c = c_ref[...]
    a = (c * jax.nn.sigmoid(c)).astype(BF16)
    o_ref[0] = _dot(a, w_ref[0].astype(BF16)) + b_ref[0]


def _modulation(c_all, w_mod, b_mod):
    depth, d, n = w_mod.shape
    rows = c_all.shape[0]
    tn = 768
    return pl.pallas_call(
        _mod_kernel,
        grid=(depth, n // tn),
        in_specs=[pl.BlockSpec((rows, d), lambda l, j: (0, 0)),
                  pl.BlockSpec((1, d, tn), lambda l, j: (l, 0, j)),
                  pl.BlockSpec((1, 1, tn), lambda l, j: (l, 0, j))],
        out_specs=pl.BlockSpec((1, rows, tn), lambda l, j: (l, 0, j)),
        out_shape=jax.ShapeDtypeStruct((depth, rows, n), F32),
        compiler_params=_cparams(("parallel", "parallel")),
        name="modulation",
    )(c_all, w_mod, b_mod.reshape(depth, 1, n))


def _regroup_kernel(w_ref, o_ref):
    starts = np.concatenate([[0], np.cumsum(IN_SPLIT_SIZES)]).tolist()
    col = 0
    for g in _FULL_GROUPS + _GATE_GROUPS:
        width = IN_SPLIT_SIZES[g]
        o_ref[0, :, col:col + width] = w_ref[0, :, starts[g]:starts[g] + width].astype(BF16)
        col += width
    o_ref[0, :, col:N_IN_PAD] = jnp.zeros((o_ref.shape[1], N_IN_PAD - col), BF16)


def _regroup_w_in(w_in):
    depth, d, n_in = w_in.shape
    tr = 256
    return pl.pallas_call(
        _regroup_kernel,
        grid=(depth, d // tr),
        in_specs=[pl.BlockSpec((1, tr, n_in), lambda l, r: (l, r, 0))],
        out_specs=pl.BlockSpec((1, tr, N_IN_PAD), lambda l, r: (l, r, 0)),
        out_shape=jax.ShapeDtypeStruct((depth, d, N_IN_PAD), BF16),
        compiler_params=_cparams(("parallel", "parallel")),
        name="regroup_w_in",
    )(w_in)


_IN_OUTS = (
    ("fqa", 4 * LANES, BF16),
    ("fka", 4 * LANES, BF16),
    ("fvb", GROUP, BF16),
    ("dqv", 8 * LANES, BF16),
    ("dkb", GROUP, BF16), ("dvb", GROUP, BF16),
    ("mqk", 2 * GROUP, F32), ("mvb", GROUP, BF16), ("mos", GROUP, F32),
    ("bqm", 4 * LANES, BF16),
    ("bkb", GROUP, BF16), ("bvb", GROUP, BF16),
    ("gates", LANES, F32),
    ("cum", LANES, F32),
)
_IN_STATE = ("fk", "fv", "dk", "dv", "bk", "bv")
N_IN_ARGS = 10


def _in_proj_kernel(*refs, bb, tt, running, ml_blocks, n_prev):
    (x_ref, sh_ref, sc_ref, g1_ref, w_ref, b_ref, gains_ref, s64_ref, s32_ref, tril_ref) = refs[:N_IN_ARGS]
    outs = refs[N_IN_ARGS + n_prev:]
    (fqa_ref, fka_ref, fvb_ref, dqv_ref, dkb_ref, dvb_ref, mqk_ref, mvb_ref, mos_ref, bqm_ref, bkb_ref,
     bvb_ref, gates_ref, cum_ref) = outs[:len(_IN_OUTS)]
    state = dict(zip(_IN_STATE, outs[len(_IN_OUTS):]))
    carry_ref = outs[len(_IN_OUTS) + len(_IN_STATE)]
    tm = bb * tt

    def put_state(name, val):
        ref = state[name]
        if not running:
            ref[...] = val
        elif name in ("bk", "bv"):
            @pl.when(pl.program_id(1) == pl.num_programs(1) - 1)
            def _():
                ref[0, 0] = val.T
        else:
            ref[0, 0] = val.T

    x = x_ref[...]
    ms = jnp.mean(x * x, axis=-1, keepdims=True)
    h = x * lax.rsqrt(ms + EPS) * g1_ref[...]
    h = h * (1.0 + sc_ref[0, :, 0]) + sh_ref[0, :, 0]
    hb = h.reshape(tm, D_MODEL).astype(BF16)

    group_order = iter([13] + list(range(13)))
    pending = []

    def issue():
        g = next(group_order, None)
        if g is not None:
            c0, width = GROUP * g, (LANES if g == 13 else GROUP)
            pending.append((g, _dot(hb, w_ref[0, :, c0:c0 + width]) + b_ref[:, c0:c0 + width]))

    def proj(g, width=GROUP):
        got, z = pending.pop(0)
        assert got == g and z.shape[1] == width
        issue()
        return z

    issue()

    def rms_seg(z, s_ref, row, n):
        ss = _dot((z * z).astype(BF16), s_ref[...])
        return z * lax.rsqrt(ss * (1.0 / n) + EPS) * gains_ref[row:row + 1, :]

    lane = lax.broadcasted_iota(jnp.int32, (tm, LANES), 1)

    zg = proj(13, LANES)
    is_ls = (lane < N_HEADS) | ((lane >= 2 * N_HEADS) & (lane < N_GATES))
    gates = jnp.where(is_ls, _log_sigmoid(zg), zg)
    gates_ref[...] = gates
    ghi, gmid, glo = _split3(gates)
    tril = tril_ref[...]
    lane_b = lax.broadcasted_iota(jnp.int32, (LANES, LANES), 1)
    if running:
        @pl.when(pl.program_id(1) == 0)
        def _():
            carry_ref[...] = jnp.zeros_like(carry_ref)
        carry_a = carry_ref[0:1, :]
    carry_b = None
    cums = []
    for blk in range(tm // LANES):
        sl = slice(LANES * blk, LANES * blk + LANES)
        p = _dot(tril, ghi[sl]) + _dot(tril, gmid[sl]) + _dot(tril, glo[sl])
        if running:
            ca = p + carry_a
            carry_a = ca[LANES - 1:LANES, :]
            cb = p if blk % ml_blocks == 0 else p + carry_b
            carry_b = cb[LANES - 1:LANES, :]
            p = jnp.where(lane_b < N_HEADS, ca, cb)
        cums.append(p)
    if running:
        carry_ref[0:1, :] = carry_a
    cum = jnp.concatenate(cums, axis=0)
    cum_ref[...] = cum

    fqn = rms_seg(proj(0), s64_ref, 0, HEAD_DIM) * (HEAD_DIM ** -0.5 * LOG2E)
    fkn = rms_seg(proj(1), s64_ref, 1, HEAD_DIM)
    put_state("fk", fkn)
    fv = proj(2)
    put_state("fv", fv)
    fvb_ref[...] = fv.astype(BF16)
    ones_q = jnp.where((lane >= HEAD_DIM + 3) & (lane < HEAD_DIM + 6), 1.0, 0.0)
    ones_k = jnp.where((lane >= HEAD_DIM) & (lane < HEAD_DIM + 3), 1.0, 0.0)
    for hd in range(N_HEADS):
        pr, e = divmod(hd, 2)
        bq = fqn[:, LANES * pr:LANES * pr + LANES]
        bk = fkn[:, LANES * pr:LANES * pr + LANES]
        if e:
            bq = pltpu.roll(bq, HEAD_DIM, axis=1)
            bk = pltpu.roll(bk, HEAD_DIM, axis=1)
        cbc = jnp.broadcast_to(cum[:, hd:hd + 1], (tm, LANES)) * LOG2E
        chi = cbc.astype(BF16).astype(F32)
        r = cbc - chi
        cmid = r.astype(BF16).astype(F32)
        clo = (r - cmid).astype(BF16).astype(F32)
        aq = jnp.where(lane == HEAD_DIM, chi,
                       jnp.where(lane == HEAD_DIM + 1, cmid, jnp.where(lane == HEAD_DIM + 2, clo, ones_q)))
        ak = jnp.where(lane == HEAD_DIM + 3, -chi,
                       jnp.where(lane == HEAD_DIM + 4, -cmid, jnp.where(lane == HEAD_DIM + 5, -clo, ones_k)))
        fqa_ref[:, LANES * hd:LANES * hd + LANES] = jnp.where(lane < HEAD_DIM, bq, aq).astype(BF16)
        fka_ref[:, LANES * hd:LANES * hd + LANES] = jnp.where(lane < HEAD_DIM, bk, ak).astype(BF16)

    dqn = rms_seg(proj(3), s32_ref, 2, DIFF_HALF) * (DIFF_HALF ** -0.5 * LOG2E)
    dkn = rms_seg(proj(4), s32_ref, 3, DIFF_HALF)
    put_state("dk", dkn)
    dkb_ref[...] = dkn.astype(BF16)
    dv = proj(5)
    put_state("dv", dv)
    dvb_ref[...] = dv.astype(BF16)
    for pr in range(2):
        blk = dqn[:, LANES * pr:LANES * pr + LANES]
        for e in range(2):
            for jm in range(2):
                l0 = HEAD_DIM * e + DIFF_HALF * jm
                idx = 4 * pr + 2 * e + jm
                dqv_ref[:, LANES * idx:LANES * idx + LANES] = jnp.where(
                    (lane >= l0) & (lane < l0 + DIFF_HALF), blk, 0.0).astype(BF16)

    mqk_ref[:, 0:GROUP] = proj(6)
    mqk_ref[:, GROUP:2 * GROUP] = proj(7)
    mvb_ref[...] = proj(8).astype(BF16)
    mos_ref[...] = jax.nn.sigmoid(proj(9))

    bqn = rms_seg(proj(10), s64_ref, 4, HEAD_DIM) * (HEAD_DIM ** -0.5 * LOG2E)
    bkn = rms_seg(proj(11), s64_ref, 5, HEAD_DIM)
    put_state("bk", bkn)
    bkb_ref[...] = bkn.astype(BF16)
    bv = proj(12)
    put_state("bv", bv)
    bvb_ref[...] = bv.astype(BF16)
    for hd in range(N_HEADS):
        pr, e = divmod(hd, 2)
        blk = bqn[:, LANES * pr:LANES * pr + LANES]
        bqm_ref[:, LANES * hd:LANES * hd + LANES] = jnp.where(
            (lane >= HEAD_DIM * e) & (lane < HEAD_DIM * e + HEAD_DIM), blk, 0.0).astype(BF16)


def _mod_spec(bb, d, layer, row0, comp):
    return pl.BlockSpec((1, bb, 1, 1, d), lambda i, t: (layer, row0 // bb + i, comp, 0, 0))


def _in_proj(x, mod, row0, g1, w, b, gains, s64, s32, tril, *, bb, tt, running, ml_blocks,
             layer=0, depth=1, prev_state=None):
    bx, tx, d = x.shape
    n = bx * tx
    nb, nt = bx // bb, tx // tt
    tm = bb * tt
    const = lambda i, t: (0, 0)
    tok = lambda i, t: (i * nt + t, 0)
    keep = min(BAND_ROWS, tx)
    if running:
        assert bb == 1 and keep == tm
        st_shapes = [(depth, bx, GROUP, tx)] * 4 + [(depth, bx, GROUP, keep)] * 2
        st_specs = ([pl.BlockSpec((1, 1, GROUP, tm), lambda i, t: (layer, i, 0, t))] * 4
                    + [pl.BlockSpec((1, 1, GROUP, keep), lambda i, t: (layer, i, 0, 0))] * 2)
    else:
        st_shapes = [(n, GROUP)] * 6
        st_specs = [pl.BlockSpec((tm, GROUP), tok)] * 6
    prev = list(prev_state) if prev_state is not None else []
    aliases = {N_IN_ARGS + k: len(_IN_OUTS) + k for k in range(len(prev))}
    in_specs = [
        pl.BlockSpec((bb, tt, d), lambda i, t: (i, t, 0)),
        _mod_spec(bb, d, layer, row0, 0),
        _mod_spec(bb, d, layer, row0, 1),
        pl.BlockSpec((1, d), const),
        pl.BlockSpec((1, d, N_IN_PAD), lambda i, t: (layer, 0, 0), pipeline_mode=pl.Buffered(1)),
        pl.BlockSpec((1, N_IN_PAD), const),
        pl.BlockSpec((8, GROUP), const),
        pl.BlockSpec((GROUP, GROUP), const),
        pl.BlockSpec((GROUP, GROUP), const),
        pl.BlockSpec((LANES, LANES), const),
    ]
    in_specs += [pl.BlockSpec(memory_space=pl.ANY)] * len(prev)
    out_specs = [pl.BlockSpec((tm, wd), tok) for _, wd, _ in _IN_OUTS] + st_specs
    out_shape = ([jax.ShapeDtypeStruct((n, wd), dt) for _, wd, dt in _IN_OUTS]
                 + [jax.ShapeDtypeStruct(s, F32) for s in st_shapes])
    outs = pl.pallas_call(
        functools.partial(_in_proj_kernel, bb=bb, tt=tt, running=running, ml_blocks=ml_blocks,
                          n_prev=len(prev)),
        grid=(nb, nt),
        in_specs=in_specs,
        out_specs=out_specs,
        out_shape=out_shape,
        scratch_shapes=[pltpu.VMEM((8, LANES), F32)],
        input_output_aliases=aliases,
        compiler_params=_cparams(("arbitrary", "arbitrary")),
        name="in_proj",
    )(x, mod, mod, g1, w, b, gains, s64, s32, tril, *prev)
    ops = {name: o for (name, _, _), o in zip(_IN_OUTS, outs)}
    return ops, list(outs[len(_IN_OUTS):])


def _softmax_step(s, v, m_ref, l_ref, acc_ref, c, shift=None):
    nt = s.shape[1] // LANES
    tiles = [s[:, LANES * t:LANES * t + LANES] for t in range(nt)]
    m_prev = m_ref[c]
    m_cur = jnp.max(s, axis=-1, keepdims=True)
    if shift is not None:
        m_cur = m_cur - shift
    m_new = jnp.maximum(m_prev, m_cur)
    alpha = jnp.exp2(m_prev - m_new)
    m_sub = m_new if shift is None else m_new + shift
    ps = [jnp.exp2(t - m_sub) for t in tiles]
    psum = ps[0]
    for p in ps[1:]:
        psum = psum + p
    l_ref[c] = alpha * l_ref[c] + psum
    p = jnp.concatenate([p.astype(BF16) for p in ps], axis=1)
    acc_ref[c] = alpha * acc_ref[c] + _dot(p, v)
    m_ref[c] = m_new


def _pipelined(n, make_s, consume, depth=1):
    queue = [make_s(c) for c in range(min(depth, n))]
    for c in range(n):
        if c + depth < n:
            queue.append(make_s(c + depth))
        consume(c, queue.pop(0))


def _row_sum(l):
    return jnp.sum(l, axis=-1, keepdims=True)


def _causal_pairs(nq, nh):
    qi = np.array([i for i in range(nq) for _ in range(nh * i + nh)], np.int32)
    kj = np.array([j for i in range(nq) for j in range(nh * i + nh)], np.int32)
    return jnp.asarray(qi), jnp.asarray(kj)


def _causal_sweep(i, j, nh, step, finalize):
    d = j - nh * i

    @pl.when(d < 0)
    def _():
        step(("full",) * nh)

    for dd in range(nh):
        @pl.when(d == dd)
        def _(dd=dd):
            step(tuple("skip" if h < dd else ("diag" if h == dd else "full") for h in range(nh)))
            if dd == nh - 1:
                finalize()


def _init_softmax_state(m_ref, l_ref, acc_ref):
    m_ref[...] = jnp.full_like(m_ref, NEG)
    l_ref[...] = jnp.zeros_like(l_ref)
    acc_ref[...] = jnp.zeros_like(acc_ref)


def _fox_kernel(qi_ref, kj_ref, q_ref, k_ref, v_ref, o_ref, m_ref, l_ref, acc_ref, *, tk, nh):
    p = pl.program_id(1)
    i = qi_ref[p]
    j = kj_ref[p]

    @pl.when(j == 0)
    def _():
        _init_softmax_state(m_ref, l_ref, acc_ref)

    def step(modes):
        k = k_ref[0]
        v = v_ref[0]
        combos = [(h, hd) for h in range(nh) if modes[h] != "skip" for hd in range(N_HEADS)]
        if "diag" in modes:
            row = lax.broadcasted_iota(jnp.int32, (tk, tk), 0)
            col = lax.broadcasted_iota(jnp.int32, (tk, tk), 1)
            keep = col <= row

        def make_s(c):
            h, hd = combos[c]
            q = q_ref[0, tk * h:tk * h + tk, LANES * hd:LANES * hd + LANES]
            s = _nt_dot(q, k[:, LANES * hd:LANES * hd + LANES])
            return jnp.where(keep, s, NEG) if modes[h] == "diag" else s

        def consume(c, s):
            h, hd = combos[c]
            pr = hd // 2
            _softmax_step(s, v[:, LANES * pr:LANES * pr + LANES], m_ref, l_ref, acc_ref, N_HEADS * h + hd)

        _pipelined(len(combos), make_s, consume)

    def finalize():
        lane = lax.broadcasted_iota(jnp.int32, (tk, LANES), 1)
        for h in range(nh):
            for pr in range(2):
                c = N_HEADS * h + 2 * pr
                oe = acc_ref[c] / _row_sum(l_ref[c])
                oo = acc_ref[c + 1] / _row_sum(l_ref[c + 1])
                o_ref[0, tk * h:tk * h + tk, LANES * pr:LANES * pr + LANES] = jnp.where(
                    lane < HEAD_DIM, oe, oo).astype(BF16)

    _causal_sweep(i, j, nh, step, finalize)


def _fox_prompt(qa, ka, vb, *, tk, nh):
    b, t, _ = qa.shape
    tq = nh * tk
    qi, kj = _causal_pairs(t // tq, nh)
    q_map = lambda bi, p, qi, kj: (bi, qi[p], 0)
    k_map = lambda bi, p, qi, kj: (bi, kj[p], 0)
    sets = nh * N_HEADS
    return pl.pallas_call(
        functools.partial(_fox_kernel, tk=tk, nh=nh),
        grid_spec=pltpu.PrefetchScalarGridSpec(
            num_scalar_prefetch=2,
            grid=(b, qi.shape[0]),
            in_specs=[pl.BlockSpec((1, tq, 4 * LANES), q_map), pl.BlockSpec((1, tk, 4 * LANES), k_map),
                      pl.BlockSpec((1, tk, GROUP), k_map)],
            out_specs=pl.BlockSpec((1, tq, GROUP), q_map),
            scratch_shapes=[pltpu.VMEM((sets, tk, LANES), F32), pltpu.VMEM((sets, tk, LANES), F32),
                            pltpu.VMEM((sets, tk, LANES), F32)]),
        out_shape=jax.ShapeDtypeStruct((b, t, GROUP), BF16),
        compiler_params=_cparams(("parallel", "arbitrary")),
        name="fox_prompt",
    )(qi, kj, qa, ka, vb)


def _diff_lambda(lp, lam_init):
    a = jnp.sum(lp[0:1, :] * lp[1:2, :], axis=-1, keepdims=True)
    b = jnp.sum(lp[2:3, :] * lp[3:4, :], axis=-1, keepdims=True)
    return jnp.exp(a) - jnp.exp(b) + lam_init


def _diff_finish(acc_ref, l_ref, lam, gsub, lam_init, rows, o_ref, base=0, row0=0):
    lane = lax.broadcasted_iota(jnp.int32, (rows, LANES), 1)
    for pr in range(2):
        outs = []
        for e in range(2):
            c = base + 2 * (2 * pr + e)
            o = acc_ref[c] / _row_sum(l_ref[c]) - lam * (acc_ref[c + 1] / _row_sum(l_ref[c + 1]))
            valid = (lane >= HEAD_DIM * e) & (lane < HEAD_DIM * e + HEAD_DIM)
            ms = jnp.sum(jnp.where(valid, o * o, 0.0), axis=-1, keepdims=True) * (1.0 / HEAD_DIM)
            outs.append(o * lax.rsqrt(ms + EPS) * gsub * (1.0 - lam_init))
        o_ref[0, row0:row0 + rows, LANES * pr:LANES * pr + LANES] = jnp.where(
            lane < HEAD_DIM, outs[0], outs[1]).astype(BF16)


def _alibi_slope(hd):
    return 2.0 ** (-8.0 * (hd + 1) / N_HEADS) * LOG2E


def _alibi_features(tq):
    pos = np.arange(tq, dtype=np.float32)[:, None]
    lane = np.arange(LANES)[None, :]
    rnd = lambda v: v.astype(BF16).astype(np.float32)

    def split(v):
        v = v.astype(np.float32)
        hi = rnd(v)
        mid = rnd(v - hi)
        lo = rnd(v - hi - mid)
        return hi, mid, lo

    def block(e, first, second):
        out = np.zeros((tq, LANES), np.float32)
        for n, val in enumerate(first + second):
            out = np.where(lane == 8 * e + n, val, out)
        return out

    one = (np.ones((tq, 1), np.float32),) * 3
    qa = [block(hd % 2, split(-_alibi_slope(hd) * pos), one) for hd in range(N_HEADS)]
    ka = [block(0, one, split(_alibi_slope(2 * pr) * pos)) + block(1, one, split(_alibi_slope(2 * pr + 1) * pos))
          for pr in range(2)]
    return (jnp.asarray(np.concatenate(qa, axis=1), dtype=BF16),
            jnp.asarray(np.concatenate(ka, axis=1), dtype=BF16))


def _diff_kernel(qi_ref, kj_ref, q_ref, k_ref, v_ref, qa_ref, ka_ref, lam_ref, gsub_ref, o_ref,
                 m_ref, l_ref, acc_ref, *, tk, nh, lam_init):
    p = pl.program_id(1)
    i = qi_ref[p]
    j = kj_ref[p]
    per_half = 2 * N_HEADS

    @pl.when(j == 0)
    def _():
        _init_softmax_state(m_ref, l_ref, acc_ref)

    def step(modes):
        k = k_ref[0]
        v = v_ref[0]
        combos = [(h, c) for h in range(nh) if modes[h] != "skip" for c in range(per_half)]
        if "diag" in modes:
            row = lax.broadcasted_iota(jnp.int32, (tk, tk), 0)
            col = lax.broadcasted_iota(jnp.int32, (tk, tk), 1)
            seen = (col // CHUNK) <= (row // CHUNK)
            ahead = jnp.maximum(col - row, 0).astype(F32)
        gaps = [((nh * i + h - j) * tk).astype(F32) for h in range(nh)]

        def make_s(n):
            h, c = combos[n]
            hd, jm = divmod(c, 2)
            pr, e = divmod(hd, 2)
            idx = 4 * pr + 2 * e + jm
            qx = jnp.concatenate([q_ref[0, tk * h:tk * h + tk, LANES * idx:LANES * idx + LANES],
                                  qa_ref[:, LANES * hd:LANES * hd + LANES]], axis=1)
            kx = jnp.concatenate([k[:, LANES * pr:LANES * pr + LANES], ka_ref[:, LANES * pr:LANES * pr + LANES]],
                                 axis=1)
            s = _nt_dot(qx, kx)
            if modes[h] == "diag":
                s = jnp.where(seen, s - (2.0 * _alibi_slope(hd)) * ahead, NEG)
            return s

        def consume(n, s):
            h, c = combos[n]
            pr = c // 4
            shift = None if modes[h] == "diag" else _alibi_slope(c // 2) * gaps[h]
            _softmax_step(s, v[:, LANES * pr:LANES * pr + LANES], m_ref, l_ref, acc_ref, per_half * h + c, shift)

        _pipelined(len(combos), make_s, consume)

    def finalize():
        lam = _diff_lambda(lam_ref[...], lam_init)
        for h in range(nh):
            _diff_finish(acc_ref, l_ref, lam, gsub_ref[...], lam_init, tk, o_ref, base=per_half * h, row0=tk * h)

    _causal_sweep(i, j, nh, step, finalize)


def _diff_prompt(qv, kb, vb, lam_p, gsub, *, tk, nh, lam_init):
    b, t, _ = qv.shape
    tq = nh * tk
    qi, kj = _causal_pairs(t // tq, nh)
    qa, ka = _alibi_features(tk)
    q_map = lambda bi, p, qi, kj: (bi, qi[p], 0)
    k_map = lambda bi, p, qi, kj: (bi, kj[p], 0)
    const = lambda bi, p, qi, kj: (0, 0)
    sets = nh * 2 * N_HEADS
    return pl.pallas_call(
        functools.partial(_diff_kernel, tk=tk, nh=nh, lam_init=lam_init),
        grid_spec=pltpu.PrefetchScalarGridSpec(
            num_scalar_prefetch=2,
            grid=(b, qi.shape[0]),
            in_specs=[pl.BlockSpec((1, tq, 8 * LANES), q_map), pl.BlockSpec((1, tk, GROUP), k_map),
                      pl.BlockSpec((1, tk, GROUP), k_map),
                      pl.BlockSpec((tk, N_HEADS * LANES), const), pl.BlockSpec((tk, 2 * LANES), const),
                      pl.BlockSpec((4, DIFF_HALF), const), pl.BlockSpec((1, LANES), const)],
            out_specs=pl.BlockSpec((1, tq, GROUP), q_map),
            scratch_shapes=[pltpu.VMEM((sets, tk, LANES), F32), pltpu.VMEM((sets, tk, LANES), F32),
                            pltpu.VMEM((sets, tk, LANES), F32)]),
        out_shape=jax.ShapeDtypeStruct((b, t, GROUP), BF16),
        compiler_params=_cparams(("parallel", "arbitrary")),
        name="diff_prompt",
    )(qi, kj, qv, kb, vb, qa, ka, lam_p, gsub)


def _band_bias_kernel(g_ref, bp_ref, bs_ref):
    lane = lax.broadcasted_iota(jnp.int32, (CHUNK, BAND_WIN), 1)
    width = g_ref.shape[1]
    for hd in range(N_HEADS):
        x = jnp.broadcast_to(g_ref[hd:hd + 1, :], (CHUNK, width))
        rp = pltpu.roll(x, width - CHUNK + 1, axis=1, stride=1, stride_axis=0)
        bp_ref[hd] = jnp.where(lane < CHUNK, NEG, rp[:, :BAND_WIN] * LOG2E)
        rs = pltpu.roll(x, width - 2 * CHUNK + 1, axis=1, stride=1, stride_axis=0)
        bs_ref[hd] = rs[:, :BAND_WIN] * LOG2E


def _band_bias(table):
    gv = jnp.concatenate([jnp.broadcast_to(table[:, 2 * REL_CLIP:], (N_HEADS, 4 * LANES)),
                          table[:, 2 * REL_CLIP - 1::-1]], axis=1)
    shp = jax.ShapeDtypeStruct((N_HEADS, CHUNK, BAND_WIN), F32)
    return pl.pallas_call(_band_bias_kernel, out_shape=(shp, shp), name="band_bias")(gv)


def _band_kernel(q_ref, k_ref, v_ref, bias_ref, o_ref, kpad_ref, vpad_ref, *, tq):
    t = pl.program_id(1)
    n_sub = tq // CHUNK
    lane = lax.broadcasted_iota(jnp.int32, (CHUNK, LANES), 1)
    col = lax.broadcasted_iota(jnp.int32, (CHUNK, BAND_WIN), 1)

    @pl.when(t == 0)
    def _():
        for src, dst in ((k_ref, kpad_ref), (v_ref, vpad_ref)):
            dst[0:BAND_WIN, :] = jnp.zeros((BAND_WIN, GROUP), BF16)
            dst[BAND_WIN:, :] = src[0]

    def window(ref, c, pr):
        start = pl.multiple_of((t * n_sub + c + 1) * CHUNK, CHUNK)
        return ref[pl.ds(start, BAND_WIN), LANES * pr:LANES * pr + LANES]

    def make_s(idx):
        c, hd = divmod(idx, N_HEADS)
        q = q_ref[0, CHUNK * c:CHUNK * c + CHUNK, LANES * hd:LANES * hd + LANES]
        valid = col + (t * n_sub + c - BAND_CHUNKS - 1) * CHUNK >= 0
        return jnp.where(valid, _nt_dot(q, window(kpad_ref, c, hd // 2)) + bias_ref[hd], NEG)

    even = {}

    def consume(idx, s):
        c, hd = divmod(idx, N_HEADS)
        pr, e = divmod(hd, 2)
        m = jnp.max(s, axis=-1, keepdims=True)
        p = jnp.exp2(s - m)
        l = jnp.sum(p, axis=-1, keepdims=True)
        o = _dot(p.astype(BF16), window(vpad_ref, c, pr)) / l
        if e == 0:
            even[pr] = o
        else:
            o_ref[0, CHUNK * c:CHUNK * c + CHUNK, LANES * pr:LANES * pr + LANES] = jnp.where(
                lane < HEAD_DIM, even[pr], o).astype(BF16)

    _pipelined(n_sub * N_HEADS, make_s, consume, depth=BAND_LOOKAHEAD)


def _band_prompt(qm, kb, vb, bias, *, tq):
    b, t, _ = qm.shape
    return pl.pallas_call(
        functools.partial(_band_kernel, tq=tq),
        grid=(b, t // tq),
        in_specs=[pl.BlockSpec((1, tq, 4 * LANES), lambda bi, i: (bi, i, 0)),
                  pl.BlockSpec((1, t, GROUP), lambda bi, i: (bi, 0, 0)),
                  pl.BlockSpec((1, t, GROUP), lambda bi, i: (bi, 0, 0)),
                  pl.BlockSpec((N_HEADS, CHUNK, BAND_WIN), lambda bi, i: (0, 0, 0))],
        out_specs=pl.BlockSpec((1, tq, GROUP), lambda bi, i: (bi, i, 0)),
        out_shape=jax.ShapeDtypeStruct((b, t, GROUP), BF16),
        scratch_shapes=[pltpu.VMEM((BAND_WIN + t, GROUP), BF16), pltpu.VMEM((BAND_WIN + t, GROUP), BF16)],
        compiler_params=_cparams(("arbitrary", "arbitrary")),
        name="band_prompt",
    )(qm, kb, vb, bias)


def _mlstm_kernel(mqk_ref, mv_ref, gates_ref, cum_ref, mos_ref, cw_ref, cb_ref, gmh_ref, shead_ref, bd_ref,
                  c0_ref, n0_ref, m0_ref, conv0_ref,
                  o_ref, c_out_ref, n_out_ref, m_out_ref,
                  cbuf_ref, c_ref, n_ref, m_ref, *, rows, valid, ns):
    t = pl.program_id(1)

    @pl.when(t == 0)
    def _():
        for s in range(ns):
            c_ref[s] = c0_ref[s]
            n_ref[s] = n0_ref[s]
            m_ref[s] = m0_ref[s]
            cbuf_ref[s, 0:8, :] = conv0_ref[s]

    for s in range(ns):
        _mlstm_chunk(mqk_ref.at[s], mv_ref.at[s], gates_ref.at[s], cum_ref.at[s], mos_ref.at[s], cw_ref, cb_ref,
                     gmh_ref, shead_ref, bd_ref, o_ref.at[s], cbuf_ref.at[s], c_ref.at[s], n_ref.at[s],
                     m_ref.at[s], rows=rows, valid=valid)

    @pl.when(t == pl.num_programs(1) - 1)
    def _():
        for s in range(ns):
            c_out_ref[s] = c_ref[s]
            n_out_ref[s] = n_ref[s]
            m_out_ref[s] = m_ref[s]


def _mlstm_chunk(mqk_ref, mv_ref, gates_ref, cum_ref, mos_ref, cw_ref, cb_ref, gmh_ref, shead_ref, bd_ref,
                 o_ref, cbuf_ref, c_ref, n_ref, m_ref, *, rows, valid):
    def padded(a, fill=0.0):
        if valid == rows:
            return a
        return jnp.concatenate([a, jnp.full((rows - valid, a.shape[1]), fill, a.dtype)], axis=0)

    u = padded(mqk_ref[...])
    cbuf_ref[8:8 + rows, :] = u
    y = cb_ref[...] + cw_ref[3:4, :] * u
    for jw in range(CONV_W - 1):
        y = y + cw_ref[jw:jw + 1, :] * cbuf_ref[5 + jw:5 + jw + rows, :]
    cbuf_ref[0:8, :] = cbuf_ref[valid:valid + 8, :]
    qk = y * jax.nn.sigmoid(y)
    q = qk[:, 0:GROUP]
    k = qk[:, GROUP:2 * GROUP] * (HEAD_DIM ** -0.5)
    qb = q.astype(BF16)
    kb = k.astype(BF16)
    v = padded(mv_ref[...])
    mos = padded(mos_ref[...])

    g = gates_ref[...]
    cm = cum_ref[...]
    if valid != rows:
        g = padded(g, NEG)
        cm = jnp.concatenate([cm, jnp.broadcast_to(cm[valid - 1:valid, :], (rows - valid, LANES))], axis=0)
    g_t = g.T
    cm_t = cm.T

    row = lax.broadcasted_iota(jnp.int32, (rows, rows), 0)
    col = lax.broadcasted_iota(jnp.int32, (rows, rows), 1)
    causal = col <= row
    lane_g = lax.broadcasted_iota(jnp.int32, (rows, GROUP), 1)
    lane_r = lax.broadcasted_iota(jnp.int32, (1, GROUP), 1)
    m_prev_all = m_ref[...]

    zeros = jnp.zeros((rows, GROUP), F32)
    num, a_full, ws_full, mt_full, wend_full = zeros, zeros, zeros, zeros, zeros
    decay_lane = jnp.zeros((1, GROUP), F32)
    mnew_lane = jnp.zeros((1, GROUP), F32)
    head_masks = [(lane_g >= HEAD_DIM * hd) & (lane_g < HEAD_DIM * hd + HEAD_DIM) for hd in range(N_HEADS)]
    sqks = [_nt_dot(jnp.where(hm, q, 0.0).astype(BF16), kb) for hm in head_masks]
    c_old = c_ref[...]
    n_old = n_ref[...]
    shead = shead_ref[...]
    q_c = _dot(qb, c_old.astype(BF16))
    q_n = _dot((q * n_old).astype(BF16), shead)
    for hd in range(N_HEADS):
        hm = head_masks[hd]
        hm_r = (lane_r >= HEAD_DIM * hd) & (lane_r < HEAD_DIM * hd + HEAD_DIM)
        f_c = cm[:, 2 * N_HEADS + hd:2 * N_HEADS + hd + 1]
        ig_c = g[:, N_HEADS + hd:N_HEADS + hd + 1]
        f_r = cm_t[2 * N_HEADS + hd:2 * N_HEADS + hd + 1, :]
        ig_r = g_t[N_HEADS + hd:N_HEADS + hd + 1, :]
        m_prev = m_prev_all[:, HEAD_DIM * hd:HEAD_DIM * hd + 1]
        logw = jnp.where(causal, f_c - (f_r - ig_r), NEG)
        logb = f_c + m_prev
        m_t = jnp.maximum(logb, jnp.max(logw, axis=-1, keepdims=True))
        w = jnp.exp(logw - m_t) * sqks[hd]
        a = jnp.exp(logb - m_t)
        num = jnp.where(hm, _dot(w.astype(BF16), v), num)
        a_full = jnp.where(hm, a, a_full)
        ws_full = jnp.where(hm, jnp.sum(w, axis=-1, keepdims=True), ws_full)
        mt_full = jnp.where(hm, m_t, mt_full)
        f_end = f_c[rows - 1:rows, :]
        log_end = f_end - f_c + ig_c
        m_new = jnp.maximum(f_end + m_prev, jnp.max(log_end, axis=0, keepdims=True))
        wend_full = jnp.where(hm, jnp.exp(log_end - m_new), wend_full)
        decay_lane = jnp.where(hm_r, jnp.exp(f_end + m_prev - m_new), decay_lane)
        mnew_lane = jnp.where(hm_r, m_new, mnew_lane)

    num = num + a_full * q_c
    den = ws_full + a_full * q_n
    h = num / jnp.maximum(jnp.abs(den), jnp.exp(-mt_full))
    o = mos * h
    ss = _dot((o * o).astype(BF16), shead)
    on = o * lax.rsqrt(ss * (1.0 / HEAD_DIM) + EPS) * gmh_ref[...]
    o_ref[...] = on[0:valid, :].astype(BF16)

    wk = k * wend_full
    kv = lax.dot_general(wk.astype(BF16), v, (((0,), (0,)), ((), ())), preferred_element_type=F32)
    c_ref[...] = c_old * decay_lane + jnp.where(bd_ref[...] > 0.5, kv, 0.0)
    n_ref[...] = n_old * decay_lane + jnp.sum(wk, axis=0, keepdims=True)
    m_ref[...] = mnew_lane


def _mlstm(mqk, mvb, gates, cum, mos, cw, cb, gmh, shead, bd, c0, n0, m0, conv0, *, nb, t, valid, rows, ns):
    nc = t // valid
    seq = lambda a: a.reshape(nb, t, a.shape[-1])
    tok = lambda bi, i: (bi, i, 0)
    const = lambda bi, i: (0, 0)
    per_b = lambda bi, i: (bi, 0, 0)
    o, c_new, n_new, m_new = pl.pallas_call(
        functools.partial(_mlstm_kernel, rows=rows, valid=valid, ns=ns),
        grid=(nb // ns, nc),
        in_specs=[pl.BlockSpec((ns, valid, 2 * GROUP), tok), pl.BlockSpec((ns, valid, GROUP), tok),
                  pl.BlockSpec((ns, valid, LANES), tok), pl.BlockSpec((ns, valid, LANES), tok),
                  pl.BlockSpec((ns, valid, GROUP), tok),
                  pl.BlockSpec((CONV_W, 2 * GROUP), const), pl.BlockSpec((1, 2 * GROUP), const),
                  pl.BlockSpec((1, GROUP), const), pl.BlockSpec((GROUP, GROUP), const),
                  pl.BlockSpec((GROUP, GROUP), const),
                  pl.BlockSpec((ns, GROUP, GROUP), per_b), pl.BlockSpec((ns, 1, GROUP), per_b),
                  pl.BlockSpec((ns, 1, GROUP), per_b), pl.BlockSpec((ns, 8, 2 * GROUP), per_b)],
        out_specs=[pl.BlockSpec((ns, valid, GROUP), tok), pl.BlockSpec((ns, GROUP, GROUP), per_b),
                   pl.BlockSpec((ns, 1, GROUP), per_b), pl.BlockSpec((ns, 1, GROUP), per_b)],
        out_shape=[jax.ShapeDtypeStruct((nb, t, GROUP), BF16), jax.ShapeDtypeStruct((nb, GROUP, GROUP), F32),
                   jax.ShapeDtypeStruct((nb, 1, GROUP), F32), jax.ShapeDtypeStruct((nb, 1, GROUP), F32)],
        scratch_shapes=[pltpu.VMEM((ns, 8 + rows, 2 * GROUP), F32), pltpu.VMEM((ns, GROUP, GROUP), F32),
                        pltpu.VMEM((ns, 1, GROUP), F32), pltpu.VMEM((ns, 1, GROUP), F32)],
        compiler_params=_cparams(("parallel", "arbitrary")),
        name="mlstm",
    )(seq(mqk), seq(mvb), seq(gates), seq(cum), seq(mos), cw, cb, gmh, shead, bd, c0, n0, m0, conv0)
    return o.reshape(nb * t, GROUP), c_new, n_new, m_new


def _pad_rows(a, rows):
    return jnp.concatenate([a, jnp.zeros((rows - a.shape[0], a.shape[1]), a.dtype)], axis=0)


def _two_part_softmax(s_c, s_n, vt_c, v_n):
    m = jnp.maximum(jnp.max(s_c, axis=-1, keepdims=True), jnp.max(s_n, axis=-1, keepdims=True))
    p_c = jnp.exp2(s_c - m)
    p_n = jnp.exp2(s_n - m)
    l = jnp.sum(p_c, axis=-1, keepdims=True) + jnp.sum(p_n, axis=-1, keepdims=True)
    return _nt_dot(p_c.astype(BF16), vt_c) + _dot(p_n.astype(BF16), v_n), l


def _fox_sample_kernel(q_ref, kn_ref, vn_ref, ck_ref, cv_ref, clf_ref, lstr_ref, o_ref, *, tn, past):
    q = q_ref[0]
    kn = _pad_rows(kn_ref[0], LANES)
    vn = _pad_rows(vn_ref[0], LANES)
    ck = ck_ref[0, 0]
    cv = cv_ref[0, 0].astype(BF16)
    x = _pad_rows(clf_ref[0, 0], 8)
    xhi, xmid, xlo = _split3(x)
    lstr = lstr_ref[...]
    suf = _dot(xhi, lstr) + _dot(xmid, lstr) + _dot(xlo, lstr)

    sub = lax.broadcasted_iota(jnp.int32, (8, past), 0)
    lane_o = lax.broadcasted_iota(jnp.int32, (tn, LANES), 1)
    row = lax.broadcasted_iota(jnp.int32, (tn, LANES), 0)
    causal = lane_o <= row
    fill = jnp.zeros((LANES - HEAD_DIM - 8, past), F32)

    def logits(hd):
        sb = jnp.broadcast_to(suf[hd:hd + 1, :], (8, past)) * LOG2E
        shi = sb.astype(BF16).astype(F32)
        r = sb - shi
        smid = r.astype(BF16).astype(F32)
        slo = (r - smid).astype(BF16).astype(F32)
        aug = jnp.where(sub < 3, 1.0,
                        jnp.where(sub == 3, shi, jnp.where(sub == 4, smid, jnp.where(sub == 5, slo, 0.0))))
        kc = jnp.concatenate([ck[HEAD_DIM * hd:HEAD_DIM * hd + HEAD_DIM, :], aug, fill], axis=0).astype(BF16)
        qh = q[:, LANES * hd:LANES * hd + LANES]
        return _dot(qh, kc), jnp.where(causal, _nt_dot(qh, kn[:, LANES * hd:LANES * hd + LANES]), NEG)

    even = {}

    def attend(hd, s):
        pr, e = divmod(hd, 2)
        acc, l = _two_part_softmax(s[0], s[1], cv[LANES * pr:LANES * pr + LANES, :],
                                   vn[:, LANES * pr:LANES * pr + LANES])
        if e == 0:
            even[pr] = acc / l
        else:
            o_ref[0, :, LANES * pr:LANES * pr + LANES] = jnp.where(lane_o < HEAD_DIM, even[pr], acc / l).astype(BF16)

    _pipelined(N_HEADS, logits, attend, depth=SAMPLE_LOOKAHEAD)


def _fox_sample_multi(q_ref, kn_ref, vn_ref, ck_ref, cv_ref, clf_ref, lstr_ref, o_ref, *, tn, past, nsb):
    for s in range(nsb):
        one = pl.ds(s, 1)
        _fox_sample_kernel(q_ref.at[one], kn_ref.at[one], vn_ref.at[one], ck_ref.at[:, one], cv_ref.at[:, one],
                           clf_ref.at[:, one], lstr_ref, o_ref.at[one], tn=tn, past=past)


def _diff_sample_multi(q_ref, kn_ref, vn_ref, ck_ref, cv_ref, lam_ref, gsub_ref, o_ref, l_ref, acc_ref,
                       *, tn, past, lam_init, nsb):
    sets = 2 * N_HEADS
    for s in range(nsb):
        one = pl.ds(s, 1)
        mine = pl.ds(sets * s, sets)
        _diff_sample_kernel(q_ref.at[one], kn_ref.at[one], vn_ref.at[one], ck_ref.at[:, one], cv_ref.at[:, one],
                            lam_ref, gsub_ref, o_ref.at[one], l_ref.at[mine], acc_ref.at[mine],
                            tn=tn, past=past, lam_init=lam_init)


def _band_sample_multi(q_ref, kn_ref, vn_ref, ck_ref, cv_ref, bias_ref, o_ref, *, tn, past, nsb):
    for s in range(nsb):
        one = pl.ds(s, 1)
        _band_sample_kernel(q_ref.at[one], kn_ref.at[one], vn_ref.at[one], ck_ref.at[:, one], cv_ref.at[:, one],
                            bias_ref, o_ref.at[one], tn=tn, past=past)


def _fox_sample(qa, ka, vb, ck, cv, clf, lstr, *, layer):
    b, tn, _ = qa.shape
    past = ck.shape[3]
    nsb = math.gcd(b, SAMPLE_SEQS)
    per_b = lambda i: (i, 0, 0)
    cache_b = lambda i: (layer, i, 0, 0)
    return pl.pallas_call(
        functools.partial(_fox_sample_multi, tn=tn, past=past, nsb=nsb),
        grid=(b // nsb,),
        in_specs=[pl.BlockSpec((nsb, tn, 4 * LANES), per_b), pl.BlockSpec((nsb, tn, 4 * LANES), per_b),
                  pl.BlockSpec((nsb, tn, GROUP), per_b), pl.BlockSpec((1, nsb, GROUP, past), cache_b),
                  pl.BlockSpec((1, nsb, GROUP, past), cache_b), pl.BlockSpec((1, nsb, N_HEADS, past), cache_b),
                  pl.BlockSpec((past, past), lambda i: (0, 0))],
        out_specs=pl.BlockSpec((nsb, tn, GROUP), per_b),
        out_shape=jax.ShapeDtypeStruct((b, tn, GROUP), BF16),
        compiler_params=_cparams(("parallel",)),
        name="fox_sample",
    )(qa, ka, vb, ck, cv, clf, lstr)


def _diff_sample_kernel(q_ref, kn_ref, vn_ref, ck_ref, cv_ref, lam_ref, gsub_ref, o_ref, l_ref, acc_ref,
                        *, tn, past, lam_init):
    q = q_ref[0]
    kn = _pad_rows(kn_ref[0], LANES)
    vn = _pad_rows(vn_ref[0], LANES)
    ck = ck_ref[0, 0].astype(BF16)
    cv = cv_ref[0, 0].astype(BF16)
    row_c = lax.broadcasted_iota(jnp.int32, (tn, past), 0)
    col_c = lax.broadcasted_iota(jnp.int32, (tn, past), 1)
    dist_c = (past + row_c - col_c).astype(F32)
    row_n = lax.broadcasted_iota(jnp.int32, (tn, LANES), 0)
    col_n = lax.broadcasted_iota(jnp.int32, (tn, LANES), 1)
    dist_n = jnp.abs(row_n - col_n).astype(F32)
    real = col_n < tn

    def logits(c):
        hd, jm = divmod(c, 2)
        pr, e = divmod(hd, 2)
        slope = _alibi_slope(hd)
        idx = 4 * pr + 2 * e + jm
        qh = q[:, LANES * idx:LANES * idx + LANES]
        s_c = _dot(qh, ck[LANES * pr:LANES * pr + LANES, :]) - slope * dist_c
        s_n = jnp.where(real, _nt_dot(qh, kn[:, LANES * pr:LANES * pr + LANES]) - slope * dist_n, NEG)
        return s_c, s_n

    def attend(c, s):
        pr = c // 4
        acc, l = _two_part_softmax(s[0], s[1], cv[LANES * pr:LANES * pr + LANES, :],
                                   vn[:, LANES * pr:LANES * pr + LANES])
        acc_ref[c] = acc
        l_ref[c] = l

    _pipelined(2 * N_HEADS, logits, attend, depth=SAMPLE_LOOKAHEAD)
    lam = _diff_lambda(lam_ref[...], lam_init)
    _diff_finish(acc_ref, l_ref, lam, gsub_ref[...], lam_init, tn, o_ref)


def _diff_sample(qv, kb, vb, ck, cv, lam_p, gsub, *, lam_init, layer):
    b, tn, _ = qv.shape
    past = ck.shape[3]
    nsb = math.gcd(b, SAMPLE_SEQS)
    per_b = lambda i: (i, 0, 0)
    cache_b = lambda i: (layer, i, 0, 0)
    return pl.pallas_call(
        functools.partial(_diff_sample_multi, tn=tn, past=past, lam_init=lam_init, nsb=nsb),
        grid=(b // nsb,),
        in_specs=[pl.BlockSpec((nsb, tn, 8 * LANES), per_b), pl.BlockSpec((nsb, tn, GROUP), per_b),
                  pl.BlockSpec((nsb, tn, GROUP), per_b), pl.BlockSpec((1, nsb, GROUP, past), cache_b),
                  pl.BlockSpec((1, nsb, GROUP, past), cache_b), pl.BlockSpec((4, DIFF_HALF), lambda i: (0, 0)),
                  pl.BlockSpec((1, LANES), lambda i: (0, 0))],
        out_specs=pl.BlockSpec((nsb, tn, GROUP), per_b),
        out_shape=jax.ShapeDtypeStruct((b, tn, GROUP), BF16),
        scratch_shapes=[pltpu.VMEM((nsb * 2 * N_HEADS, tn, 1), F32),
                        pltpu.VMEM((nsb * 2 * N_HEADS, tn, LANES), F32)],
        compiler_params=_cparams(("parallel",)),
        name="diff_sample",
    )(qv, kb, vb, ck, cv, lam_p, gsub)


def _band_sample_kernel(q_ref, kn_ref, vn_ref, ck_ref, cv_ref, bias_ref, o_ref, *, tn, past):
    q = q_ref[0]
    kn = _pad_rows(kn_ref[0], LANES)
    vn = _pad_rows(vn_ref[0], LANES)
    ck = ck_ref[0, 0].astype(BF16)
    cv = cv_ref[0, 0].astype(BF16)
    lane = lax.broadcasted_iota(jnp.int32, (tn, LANES), 1)
    real = lane < tn

    def logits(hd):
        pr = hd // 2
        qh = q[:, LANES * hd:LANES * hd + LANES]
        s_c = _dot(qh, ck[LANES * pr:LANES * pr + LANES, :]) + bias_ref[hd, 0:tn, 0:past]
        s_n = _nt_dot(qh, kn[:, LANES * pr:LANES * pr + LANES]) + bias_ref[hd, 0:tn, past:past + LANES]
        return s_c, jnp.where(real, s_n, NEG)

    even = {}

    def attend(hd, s):
        pr, e = divmod(hd, 2)
        acc, l = _two_part_softmax(s[0], s[1], cv[LANES * pr:LANES * pr + LANES, :],
                                   vn[:, LANES * pr:LANES * pr + LANES])
        if e == 0:
            even[pr] = acc / l
        else:
            o_ref[0, :, LANES * pr:LANES * pr + LANES] = jnp.where(lane < HEAD_DIM, even[pr], acc / l).astype(BF16)

    _pipelined(N_HEADS, logits, attend, depth=SAMPLE_LOOKAHEAD)


def _band_sample(qm, kb, vb, ck, cv, bias, *, layer):
    b, tn, _ = qm.shape
    past = ck.shape[3]
    nsb = math.gcd(b, SAMPLE_SEQS)
    per_b = lambda i: (i, 0, 0)
    cache_b = lambda i: (layer, i, 0, 0)
    return pl.pallas_call(
        functools.partial(_band_sample_multi, tn=tn, past=past, nsb=nsb),
        grid=(b // nsb,),
        in_specs=[pl.BlockSpec((nsb, tn, 4 * LANES), per_b), pl.BlockSpec((nsb, tn, GROUP), per_b),
                  pl.BlockSpec((nsb, tn, GROUP), per_b), pl.BlockSpec((1, nsb, GROUP, past), cache_b),
                  pl.BlockSpec((1, nsb, GROUP, past), cache_b),
                  pl.BlockSpec((N_HEADS, CHUNK, BAND_WIN), lambda i: (0, 0, 0))],
        out_specs=pl.BlockSpec((nsb, tn, GROUP), per_b),
        out_shape=jax.ShapeDtypeStruct((b, tn, GROUP), BF16),
        compiler_params=_cparams(("parallel",)),
        name="band_sample",
    )(qm, kb, vb, ck, cv, bias)


def _ffn_kernel(x_ref, gt1_ref, fox_ref, diff_ref, ml_ref, band_ref, wo_ref, sh_ref, sc_ref, gt_ref, g2_ref,
                wg_ref, wu_ref, wd_ref, o_ref, acc_ref, *, bb, tt, tf):
    tm = bb * tt
    mix = _dot(fox_ref[...], wo_ref[0, 0:GROUP, :])
    mix += _dot(diff_ref[...], wo_ref[0, GROUP:2 * GROUP, :])
    mix += _dot(ml_ref[...], wo_ref[0, 2 * GROUP:3 * GROUP, :])
    mix += _dot(band_ref[...], wo_ref[0, 3 * GROUP:4 * GROUP, :])
    x = x_ref[...] + gt1_ref[0, :, 0] * mix.reshape(bb, tt, D_MODEL)
    ms = jnp.mean(x * x, axis=-1, keepdims=True)
    h = x * lax.rsqrt(ms + EPS) * g2_ref[...]
    h = h * (1.0 + sc_ref[0, :, 0]) + sh_ref[0, :, 0]
    hb = h.reshape(tm, D_MODEL).astype(BF16)

    def gate_up(f):
        cols = slice(tf * f, tf * f + tf)
        return _dot(hb, wg_ref[0, :, cols]), _dot(hb, wu_ref[0, :, cols])

    def down(f, gu):
        g, u = gu
        a = (g * jax.nn.sigmoid(g) * u).astype(BF16)
        part = _dot(a, wd_ref[0, tf * f:tf * f + tf, :])
        if f == 0:
            acc_ref[...] = part
        else:
            acc_ref[...] += part

    _pipelined(D_FF // tf, gate_up, down)
    o_ref[...] = x + gt_ref[0, :, 0] * acc_ref[...].reshape(bb, tt, D_MODEL)


def _out_ffn(x, mod, row0, o_fox, o_diff, o_ml, o_band, w_out, g2, wg, wu, wd, *, bb, tt, tf, layer):
    bx, tx, d = x.shape
    nb, nt = bx // bb, tx // tt
    tm = bb * tt
    xs = pl.BlockSpec((bb, tt, d), lambda i, t: (i, t, 0))
    ms = lambda comp: _mod_spec(bb, d, layer, row0, comp)
    mixer = pl.BlockSpec((tm, GROUP), lambda i, t: (i * nt + t, 0))
    resident = lambda shape: pl.BlockSpec((1,) + shape, lambda i, t: (layer, 0, 0), pipeline_mode=pl.Buffered(1))
    return pl.pallas_call(
        functools.partial(_ffn_kernel, bb=bb, tt=tt, tf=tf),
        grid=(nb, nt),
        in_specs=[xs, ms(2), mixer, mixer, mixer, mixer, resident((d, d)),
                  ms(3), ms(4), ms(5), pl.BlockSpec((1, d), lambda i, t: (0, 0)),
                  resident((d, D_FF)), resident((d, D_FF)), resident((D_FF, d))],
        out_specs=xs,
        out_shape=jax.ShapeDtypeStruct(x.shape, F32),
        scratch_shapes=[pltpu.VMEM((tm, d), F32)],
        compiler_params=_cparams(("parallel", "parallel")),
        name="out_ffn",
    )(x, mod, o_fox, o_diff, o_ml, o_band, w_out, mod, mod, mod, g2, wg, wu, wd)


def _consts(seg):
    r = np.arange(GROUP)
    s64 = (r[:, None] // HEAD_DIM == r[None, :] // HEAD_DIM)
    s32 = (r[:, None] // DIFF_HALF == r[None, :] // DIFF_HALF)
    q = np.arange(LANES)
    tril = (q[None, :] <= q[:, None]) & (q[:, None] // seg == q[None, :] // seg)
    as_bf16 = lambda m: jnp.asarray(m.astype(np.float32), dtype=BF16)
    return as_bf16(s64), as_bf16(s32), as_bf16(tril), jnp.asarray(s64.astype(np.float32))


def _prep_layer(l, norm1_g, norm2_g, w_in_g, b_in, qk_g_fox, qk_g_diff, qk_g_band, conv_w, conv_b,
                diff_lambda, diff_subln_g, mlstm_norm_g, band_rel_bias, w_out, w_ffn_gate, w_ffn_up,
                w_ffn_down):
    starts = np.concatenate([[0], np.cumsum(IN_SPLIT_SIZES)]).tolist()
    order = list(_FULL_GROUPS) + list(_GATE_GROUPS)
    bl = b_in[l]
    b_cols = [bl[starts[g]:starts[g + 1]] for g in order]
    pad = N_IN_PAD - N_FULL - N_GATES
    b_cols.append(jnp.zeros((pad,), F32))
    gains = jnp.stack([
        jnp.tile(qk_g_fox[l, 0], N_HEADS), jnp.tile(qk_g_fox[l, 1], N_HEADS),
        jnp.tile(qk_g_diff[l, 0], 2 * N_HEADS), jnp.tile(qk_g_diff[l, 1], 2 * N_HEADS),
        jnp.tile(qk_g_band[l, 0], N_HEADS), jnp.tile(qk_g_band[l, 1], N_HEADS),
        jnp.zeros((GROUP,), F32), jnp.zeros((GROUP,), F32)])
    return dict(
        g1=norm1_g[l].reshape(1, D_MODEL), g2=norm2_g[l].reshape(1, D_MODEL),
        w_in=w_in_g,
        b_in=jnp.concatenate(b_cols).reshape(1, N_IN_PAD),
        gains=gains, conv_w=conv_w[l], conv_b=conv_b[l].reshape(1, 2 * GROUP),
        lam_p=diff_lambda[l], gsub=jnp.tile(diff_subln_g[l], 2).reshape(1, LANES),
        gmh=jnp.tile(mlstm_norm_g[l], N_HEADS).reshape(1, GROUP),
        table=band_rel_bias[l],
        w_out=w_out, wg=w_ffn_gate, wu=w_ffn_up, wd=w_ffn_down,
        lam_init=0.8 - 0.6 * math.exp(-0.3 * l))


def _layer(x, mod, row0, lp, caches, *, bb, tt, layer, depth, prev_state=None):
    bx, tx, _ = x.shape
    n = bx * tx
    prompt = caches is None
    s64, s32, tril, bd = _consts(tx if prompt else min(tx, LANES))
    z, big = _in_proj(x, mod, row0, lp["g1"], lp["w_in"], lp["b_in"], lp["gains"], s64, s32, tril,
                      bb=bb, tt=tt, running=prompt, ml_blocks=ML_CHUNK // LANES,
                      layer=layer, depth=depth, prev_state=prev_state)
    r3 = lambda a: a.reshape(bx, tx, a.shape[-1])
    bias_p, bias_s = _band_bias(lp["table"])

    if prompt:
        tq = 512
        nh = Q_SUBTILES if tx % (Q_SUBTILES * tq) == 0 else 1
        o_fox = _fox_prompt(r3(z["fqa"]), r3(z["fka"]), r3(z["fvb"]), tk=tq, nh=nh)
        o_diff = _diff_prompt(r3(z["dqv"]), r3(z["dkb"]), r3(z["dvb"]), lp["lam_p"], lp["gsub"],
                              tk=tq, nh=nh, lam_init=lp["lam_init"])
        o_band = _band_prompt(r3(z["bqm"]), r3(z["bkb"]), r3(z["bvb"]), bias_p, tq=tq)
        c0 = jnp.zeros((bx, GROUP, GROUP), F32)
        n0 = jnp.zeros((bx, 1, GROUP), F32)
        m0 = jnp.zeros((bx, 1, GROUP), F32)
        conv0 = jnp.zeros((bx, 8, 2 * GROUP), F32)
        ml_valid, ml_rows = ML_CHUNK, ML_CHUNK
    else:
        (c_fk, c_fv, c_flf, c_dk, c_dv, c_bk, c_bv, s_c, s_n, s_m, s_conv) = caches
        past = c_fk.shape[3]
        assert past % CHUNK == 0 and tx <= CHUNK
        jj = np.arange(past)
        lstr = jnp.asarray((jj[:, None] > jj[None, :]).astype(np.float32), dtype=BF16)
        o_fox = _fox_sample(r3(z["fqa"]), r3(z["fka"]), r3(z["fvb"]), c_fk, c_fv, c_flf, lstr, layer=layer)
        o_diff = _diff_sample(r3(z["dqv"]), r3(z["dkb"]), r3(z["dvb"]), c_dk, c_dv,
                              lp["lam_p"], lp["gsub"], lam_init=lp["lam_init"], layer=layer)
        o_band = _band_sample(r3(z["bqm"]), r3(z["bkb"]), r3(z["bvb"]), c_bk, c_bv, bias_s, layer=layer)
        eye = jnp.eye(N_HEADS, dtype=F32)
        c0 = (s_c[:, :, :, None, :] * eye[None, :, None, :, None]).reshape(bx, GROUP, GROUP)
        n0 = s_n.reshape(bx, 1, GROUP)
        m0 = jnp.repeat(s_m, HEAD_DIM, axis=-1).reshape(bx, 1, GROUP)
        conv0 = jnp.pad(s_conv, ((0, 0), (8 - (CONV_W - 1), 0), (0, 0)))
        ml_valid, ml_rows = tx, ML_PAD
    o_ml, c_new, n_new, m_new = _mlstm(
        z["mqk"], z["mvb"], z["gates"], z["cum"], z["mos"], lp["conv_w"], lp["conv_b"], lp["gmh"],
        s64, bd, c0, n0, m0, conv0, nb=bx, t=tx, valid=ml_valid, rows=ml_rows, ns=min(bx, ML_SEQS))

    flat2 = lambda a: a.reshape(n, GROUP)
    x2 = _out_ffn(x, mod, row0, flat2(o_fox), flat2(o_diff), o_ml, flat2(o_band), lp["w_out"],
                  lp["g2"], lp["wg"], lp["wu"], lp["wd"], bb=bb, tt=tt, tf=FFN_CHUNK, layer=layer)

    mc = jnp.stack([c_new[:, HEAD_DIM * h:HEAD_DIM * h + HEAD_DIM, HEAD_DIM * h:HEAD_DIM * h + HEAD_DIM]
                    for h in range(N_HEADS)], axis=1)
    small = (r3(z["gates"])[:, :, 0:N_HEADS], mc, n_new.reshape(bx, N_HEADS, HEAD_DIM),
             m_new.reshape(bx, N_HEADS, HEAD_DIM)[:, :, 0], r3(z["mqk"])[:, tx - (CONV_W - 1):, :])
    return x2, big, small


def _channel_major(c):
    perm = (0, 1) + tuple(range(3, c.ndim)) + (2,)
    return jnp.transpose(c, perm).reshape(c.shape[0], c.shape[1], -1, c.shape[2])


def _token_major(a, inner):
    d, b, _, t = a.shape
    k = len(inner)
    return jnp.transpose(a.reshape(d, b, *inner, t), (0, 1, 2 + k) + tuple(range(2, 2 + k)))


def kernel(x_prompt, x_sample, c_prompt, c_sample, cache_fox_k, cache_fox_v, cache_fox_logf, cache_diff_k, cache_diff_v, cache_band_k, cache_band_v, state_mlstm_c, state_mlstm_n, state_mlstm_m, state_conv, norm1_g, norm2_g, w_mod, b_mod, w_in, b_in, qk_g_fox, qk_g_diff, qk_g_band, conv_w, conv_b, diff_lambda, diff_subln_g, mlstm_norm_g, band_rel_bias, w_out, w_ffn_gate, w_ffn_up, w_ffn_down):
    depth = w_in.shape[0]
    bp, bs = x_prompt.shape[0], x_sample.shape[0]
    ts = x_sample.shape[1]
    caches = tuple(_channel_major(c) for c in (cache_fox_k, cache_fox_v, cache_fox_logf, cache_diff_k,
                                               cache_diff_v, cache_band_k, cache_band_v))
    caches += (state_mlstm_c, state_mlstm_n, state_mlstm_m, state_conv)
    mod = _modulation(jnp.concatenate([c_sample, c_prompt], axis=0), w_mod, b_mod)
    mod = mod.reshape(depth, bp + bs, 6, 1, D_MODEL)
    w_in_g = _regroup_w_in(w_in)
    big_w = [w.astype(BF16) for w in (w_out, w_ffn_gate, w_ffn_up, w_ffn_down)]
    layers = [_prep_layer(l, norm1_g, norm2_g, w_in_g, b_in, qk_g_fox, qk_g_diff, qk_g_band, conv_w,
                          conv_b, diff_lambda, diff_subln_g, mlstm_norm_g, band_rel_bias, *big_w)
              for l in range(depth)]
    sample_bb = 512 // ts
    xp, xs = x_prompt, x_sample
    p_big, p_small, s_big, s_small = None, [], [], []
    for l in range(depth):
        xp, p_big, sm = _layer(xp, mod, bs, layers[l], None, bb=1, tt=512, layer=l, depth=depth,
                               prev_state=p_big)
        p_small.append(sm)
    for l in range(depth):
        xs, big, sm = _layer(xs, mod, 0, layers[l], caches[:7] + tuple(c[l] for c in caches[7:]),
                             bb=sample_bb, tt=ts,
                             layer=l, depth=depth)
        s_big.append(big)
        s_small.append(sm)
    head = (N_HEADS, HEAD_DIM)
    half = (N_HEADS, 2, DIFF_HALF)
    inner = (head, head, half, head, head, head)
    p_fk, p_fv, p_dk, p_dv, p_bk, p_bv = [_token_major(a, inn) for a, inn in zip(p_big, inner)]
    p_flf, p_c, p_n, p_m, p_conv = [jnp.stack(zs) for zs in zip(*p_small)]
    s_fk, s_fv, s_dk, s_dv, s_bk, s_bv = [jnp.stack(zs).reshape((depth, bs, ts) + inn)
                                          for zs, inn in zip(zip(*s_big), inner)]
    s_flf, s_c, s_n, s_m, s_conv = [jnp.stack(zs) for zs in zip(*s_small)]
    return (xp, xs, p_fk, p_fv, p_flf, p_dk, p_dv, p_bk, p_bv, p_c, p_n, p_m, p_conv,
            s_fk, s_fv, s_flf, s_dk, s_dv, s_bk, s_bv, s_c, s_n, s_m, s_conv)
```

```python
import functools
import math

import numpy as np
import jax
import jax.numpy as jnp
from jax import lax
from jax.experimental import pallas as pl
from jax.experimental.pallas import tpu as pltpu

F32 = jnp.float32
BF16 = jnp.bfloat16

D_MODEL = 1024
HEAD_DIM = 64
N_HEADS = 4
GROUP = N_HEADS * HEAD_DIM
DIFF_HALF = HEAD_DIM // 2
CHUNK = 64
BAND_CHUNKS = 8
BAND_ROWS = BAND_CHUNKS * CHUNK
REL_CLIP = 128
CONV_W = 4
D_FF = 2816
EPS = 1e-6
NEG = -1e30
LOG2E = math.log2(math.e)

LANES = 128
PAIR = 2 * HEAD_DIM
N_GATES = 3 * N_HEADS
BAND_WIN = (BAND_CHUNKS + 2) * CHUNK
ML_CHUNK = 512
ML_PAD = 128
ML_SEQS = 4
BAND_LOOKAHEAD = 5
SAMPLE_LOOKAHEAD = 2
SAMPLE_SEQS = 4
Q_SUBTILES = 2
FFN_CHUNK = 256
VMEM_LIMIT = 56 * 1024 * 1024

IN_SPLIT_SIZES = (GROUP, GROUP, GROUP, N_HEADS, GROUP, GROUP, GROUP, 2 * GROUP, GROUP,
                  N_HEADS, N_HEADS, GROUP, GROUP, GROUP, GROUP)
_FULL_GROUPS = (0, 1, 2, 4, 5, 6, 7, 8, 11, 12, 13, 14)
_GATE_GROUPS = (3, 9, 10)
N_FULL = 13 * GROUP
N_IN_PAD = N_FULL + LANES


def _cparams(sem):
    return pltpu.CompilerParams(dimension_semantics=sem, vmem_limit_bytes=VMEM_LIMIT)


def _nt_dot(a, b):
    return lax.dot_general(a, b, (((1,), (1,)), ((), ())), preferred_element_type=F32)


def _dot(a, b):
    return jnp.dot(a, b, preferred_element_type=F32)


def _split3(x):
    hi = x.astype(BF16)
    r = x - hi.astype(F32)
    mid = r.astype(BF16)
    lo = (r - mid.astype(F32)).astype(BF16)
    return hi, mid, lo


def _log_sigmoid(x):
    return jnp.minimum(x, 0.0) - jnp.log1p(jnp.exp(-jnp.abs(x)))


def _mod_kernel(c_ref, w_ref, b_ref, o_ref):
    c = c_ref[...]
    a = (c * jax.nn.sigmoid(c)).astype(BF16)
    o_ref[0] = _dot(a, w_ref[0].astype(BF16)) + b_ref[0]


def _modulation(c_all, w_mod, b_mod):
    depth, d, n = w_mod.shape
    rows = c_all.shape[0]
    tn = 768
    return pl.pallas_call(
        _mod_kernel,
        grid=(depth, n // tn),
        in_specs=[pl.BlockSpec((rows, d), lambda l, j: (0, 0)),
                  pl.BlockSpec((1, d, tn), lambda l, j: (l, 0, j)),
                  pl.BlockSpec((1, 1, tn), lambda l, j: (l, 0, j))],
        out_specs=pl.BlockSpec((1, rows, tn), lambda l, j: (l, 0, j)),
        out_shape=jax.ShapeDtypeStruct((depth, rows, n), F32),
        compiler_params=_cparams(("parallel", "parallel")),
        name="modulation",
    )(c_all, w_mod, b_mod.reshape(depth, 1, n))


def _regroup_kernel(w_ref, o_ref):
    starts = np.concatenate([[0], np.cumsum(IN_SPLIT_SIZES)]).tolist()
    col = 0
    for g in _FULL_GROUPS + _GATE_GROUPS:
        width = IN_SPLIT_SIZES[g]
        o_ref[0, :, col:col + width] = w_ref[0, :, starts[g]:starts[g] + width].astype(BF16)
        col += width
    o_ref[0, :, col:N_IN_PAD] = jnp.zeros((o_ref.shape[1], N_IN_PAD - col), BF16)


def _regroup_w_in(w_in):
    depth, d, n_in = w_in.shape
    tr = 256
    return pl.pallas_call(
        _regroup_kernel,
        grid=(depth, d // tr),
        in_specs=[pl.BlockSpec((1, tr, n_in), lambda l, r: (l, r, 0))],
        out_specs=pl.BlockSpec((1, tr, N_IN_PAD), lambda l, r: (l, r, 0)),
        out_shape=jax.ShapeDtypeStruct((depth, d, N_IN_PAD), BF16),
        compiler_params=_cparams(("parallel", "parallel")),
        name="regroup_w_in",
    )(w_in)


_IN_OUTS = (
    ("fqa", 4 * LANES, BF16),
    ("fka", 4 * LANES, BF16),
    ("fvb", GROUP, BF16),
    ("dqv", 8 * LANES, BF16),
    ("dkb", GROUP, BF16), ("dvb", GROUP, BF16),
    ("mqk", 2 * GROUP, F32), ("mvb", GROUP, BF16), ("mos", GROUP, F32),
    ("bqm", 4 * LANES, BF16),
    ("bkb", GROUP, BF16), ("bvb", GROUP, BF16),
    ("gates", LANES, F32),
    ("cum", LANES, F32),
)
_IN_STATE = ("fk", "fv", "dk", "dv", "bk", "bv")
N_IN_ARGS = 10


def _in_proj_kernel(*refs, bb, tt, running, ml_blocks, n_prev):
    (x_ref, sh_ref, sc_ref, g1_ref, w_ref, b_ref, gains_ref, s64_ref, s32_ref, tril_ref) = refs[:N_IN_ARGS]
    outs = refs[N_IN_ARGS + n_prev:]
    (fqa_ref, fka_ref, fvb_ref, dqv_ref, dkb_ref, dvb_ref, mqk_ref, mvb_ref, mos_ref, bqm_ref, bkb_ref,
     bvb_ref, gates_ref, cum_ref) = outs[:len(_IN_OUTS)]
    state = dict(zip(_IN_STATE, outs[len(_IN_OUTS):]))
    carry_ref = outs[len(_IN_OUTS) + len(_IN_STATE)]
    tm = bb * tt

    def put_state(name, val):
        ref = state[name]
        if not running:
            ref[...] = val
        elif name in ("bk", "bv"):
            @pl.when(pl.program_id(1) == pl.num_programs(1) - 1)
            def _():
                ref[0, 0] = val.T
        else:
            ref[0, 0] = val.T

    x = x_ref[...]
    ms = jnp.mean(x * x, axis=-1, keepdims=True)
    h = x * lax.rsqrt(ms + EPS) * g1_ref[...]
    h = h * (1.0 + sc_ref[0, :, 0]) + sh_ref[0, :, 0]
    hb = h.reshape(tm, D_MODEL).astype(BF16)

    group_order = iter([13] + list(range(13)))
    pending = []

    def issue():
        g = next(group_order, None)
        if g is not None:
            c0, width = GROUP * g, (LANES if g == 13 else GROUP)
            pending.append((g, _dot(hb, w_ref[0, :, c0:c0 + width]) + b_ref[:, c0:c0 + width]))

    def proj(g, width=GROUP):
        got, z = pending.pop(0)
        assert got == g and z.shape[1] == width
        issue()
        return z

    issue()

    def rms_seg(z, s_ref, row, n):
        ss = _dot((z * z).astype(BF16), s_ref[...])
        return z * lax.rsqrt(ss * (1.0 / n) + EPS) * gains_ref[row:row + 1, :]

    lane = lax.broadcasted_iota(jnp.int32, (tm, LANES), 1)

    zg = proj(13, LANES)
    is_ls = (lane < N_HEADS) | ((lane >= 2 * N_HEADS) & (lane < N_GATES))
    gates = jnp.where(is_ls, _log_sigmoid(zg), zg)
    gates_ref[...] = gates
    ghi, gmid, glo = _split3(gates)
    tril = tril_ref[...]
    lane_b = lax.broadcasted_iota(jnp.int32, (LANES, LANES), 1)
    if running:
        @pl.when(pl.program_id(1) == 0)
        def _():
            carry_ref[...] = jnp.zeros_like(carry_ref)
        carry_a = carry_ref[0:1, :]
    carry_b = None
    cums = []
    for blk in range(tm // LANES):
        sl = slice(LANES * blk, LANES * blk + LANES)
        p = _dot(tril, ghi[sl]) + _dot(tril, gmid[sl]) + _dot(tril, glo[sl])
        if running:
            ca = p + carry_a
            carry_a = ca[LANES - 1:LANES, :]
            cb = p if blk % ml_blocks == 0 else p + carry_b
            carry_b = cb[LANES - 1:LANES, :]
            p = jnp.where(lane_b < N_HEADS, ca, cb)
        cums.append(p)
    if running:
        carry_ref[0:1, :] = carry_a
    cum = jnp.concatenate(cums, axis=0)
    cum_ref[...] = cum

    fqn = rms_seg(proj(0), s64_ref, 0, HEAD_DIM) * (HEAD_DIM ** -0.5 * LOG2E)
    fkn = rms_seg(proj(1), s64_ref, 1, HEAD_DIM)
    put_state("fk", fkn)
    fv = proj(2)
    put_state("fv", fv)
    fvb_ref[...] = fv.astype(BF16)
    ones_q = jnp.where((lane >= HEAD_DIM + 3) & (lane < HEAD_DIM + 6), 1.0, 0.0)
    ones_k = jnp.where((lane >= HEAD_DIM) & (lane < HEAD_DIM + 3), 1.0, 0.0)
    for hd in range(N_HEADS):
        pr, e = divmod(hd, 2)
        bq = fqn[:, LANES * pr:LANES * pr + LANES]
        bk = fkn[:, LANES * pr:LANES * pr + LANES]
        if e:
            bq = pltpu.roll(bq, HEAD_DIM, axis=1)
            bk = pltpu.roll(bk, HEAD_DIM, axis=1)
        cbc = jnp.broadcast_to(cum[:, hd:hd + 1], (tm, LANES)) * LOG2E
        chi = cbc.astype(BF16).astype(F32)
        r = cbc - chi
        cmid = r.astype(BF16).astype(F32)
        clo = (r - cmid).astype(BF16).astype(F32)
        aq = jnp.where(lane == HEAD_DIM, chi,
                       jnp.where(lane == HEAD_DIM + 1, cmid, jnp.where(lane == HEAD_DIM + 2, clo, ones_q)))
        ak = jnp.where(lane == HEAD_DIM + 3, -chi,
                       jnp.where(lane == HEAD_DIM + 4, -cmid, jnp.where(lane == HEAD_DIM + 5, -clo, ones_k)))
        fqa_ref[:, LANES * hd:LANES * hd + LANES] = jnp.where(lane < HEAD_DIM, bq, aq).astype(BF16)
        fka_ref[:, LANES * hd:LANES * hd + LANES] = jnp.where(lane < HEAD_DIM, bk, ak).astype(BF16)

    dqn = rms_seg(proj(3), s32_ref, 2, DIFF_HALF) * (DIFF_HALF ** -0.5 * LOG2E)
    dkn = rms_seg(proj(4), s32_ref, 3, DIFF_HALF)
    put_state("dk", dkn)
    dkb_ref[...] = dkn.astype(BF16)
    dv = proj(5)
    put_state("dv", dv)
    dvb_ref[...] = dv.astype(BF16)
    for pr in range(2):
        blk = dqn[:, LANES * pr:LANES * pr + LANES]
        for e in range(2):
            for jm in range(2):
                l0 = HEAD_DIM * e + DIFF_HALF * jm
                idx = 4 * pr + 2 * e + jm
                dqv_ref[:, LANES * idx:LANES * idx + LANES] = jnp.where(
                    (lane >= l0) & (lane < l0 + DIFF_HALF), blk, 0.0).astype(BF16)

    mqk_ref[:, 0:GROUP] = proj(6)
    mqk_ref[:, GROUP:2 * GROUP] = proj(7)
    mvb_ref[...] = proj(8).astype(BF16)
    mos_ref[...] = jax.nn.sigmoid(proj(9))

    bqn = rms_seg(proj(10), s64_ref, 4, HEAD_DIM) * (HEAD_DIM ** -0.5 * LOG2E)
    bkn = rms_seg(proj(11), s64_ref, 5, HEAD_DIM)
    put_state("bk", bkn)
    bkb_ref[...] = bkn.astype(BF16)
    bv = proj(12)
    put_state("bv", bv)
    bvb_ref[...] = bv.astype(BF16)
    for hd in range(N_HEADS):
        pr, e = divmod(hd, 2)
        blk = bqn[:, LANES * pr:LANES * pr + LANES]
        bqm_ref[:, LANES * hd:LANES * hd + LANES] = jnp.where(
            (lane >= HEAD_DIM * e) & (lane < HEAD_DIM * e + HEAD_DIM), blk, 0.0).astype(BF16)


def _mod_spec(bb, d, layer, row0, comp):
    return pl.BlockSpec((1, bb, 1, 1, d), lambda i, t: (layer, row0 // bb + i, comp, 0, 0))


def _in_proj(x, mod, row0, g1, w, b, gains, s64, s32, tril, *, bb, tt, running, ml_blocks,
             layer=0, depth=1, prev_state=None):
    bx, tx, d = x.shape
    n = bx * tx
    nb, nt = bx // bb, tx // tt
    tm = bb * tt
    const = lambda i, t: (0, 0)
    tok = lambda i, t: (i * nt + t, 0)
    keep = min(BAND_ROWS, tx)
    if running:
        assert bb == 1 and keep == tm
        st_shapes = [(depth, bx, GROUP, tx)] * 4 + [(depth, bx, GROUP, keep)] * 2
        st_specs = ([pl.BlockSpec((1, 1, GROUP, tm), lambda i, t: (layer, i, 0, t))] * 4
                    + [pl.BlockSpec((1, 1, GROUP, keep), lambda i, t: (layer, i, 0, 0))] * 2)
    else:
        st_shapes = [(n, GROUP)] * 6
        st_specs = [pl.BlockSpec((tm, GROUP), tok)] * 6
    prev = list(prev_state) if prev_state is not None else []
    aliases = {N_IN_ARGS + k: len(_IN_OUTS) + k for k in range(len(prev))}
    in_specs = [
        pl.BlockSpec((bb, tt, d), lambda i, t: (i, t, 0)),
        _mod_spec(bb, d, layer, row0, 0),
        _mod_spec(bb, d, layer, row0, 1),
        pl.BlockSpec((1, d), const),
        pl.BlockSpec((1, d, N_IN_PAD), lambda i, t: (layer, 0, 0), pipeline_mode=pl.Buffered(1)),
        pl.BlockSpec((1, N_IN_PAD), const),
        pl.BlockSpec((8, GROUP), const),
        pl.BlockSpec((GROUP, GROUP), const),
        pl.BlockSpec((GROUP, GROUP), const),
        pl.BlockSpec((LANES, LANES), const),
    ]
    in_specs += [pl.BlockSpec(memory_space=pl.ANY)] * len(prev)
    out_specs = [pl.BlockSpec((tm, wd), tok) for _, wd, _ in _IN_OUTS] + st_specs
    out_shape = ([jax.ShapeDtypeStruct((n, wd), dt) for _, wd, dt in _IN_OUTS]
                 + [jax.ShapeDtypeStruct(s, F32) for s in st_shapes])
    outs = pl.pallas_call(
        functools.partial(_in_proj_kernel, bb=bb, tt=tt, running=running, ml_blocks=ml_blocks,
                          n_prev=len(prev)),
        grid=(nb, nt),
        in_specs=in_specs,
        out_specs=out_specs,
        out_shape=out_shape,
        scratch_shapes=[pltpu.VMEM((8, LANES), F32)],
        input_output_aliases=aliases,
        compiler_params=_cparams(("arbitrary", "arbitrary")),
        name="in_proj",
    )(x, mod, mod, g1, w, b, gains, s64, s32, tril, *prev)
    ops = {name: o for (name, _, _), o in zip(_IN_OUTS, outs)}
    return ops, list(outs[len(_IN_OUTS):])


def _softmax_step(s, v, m_ref, l_ref, acc_ref, c, shift=None):
    nt = s.shape[1] // LANES
    tiles = [s[:, LANES * t:LANES * t + LANES] for t in range(nt)]
    m_prev = m_ref[c]
    m_cur = jnp.max(s, axis=-1, keepdims=True)
    if shift is not None:
        m_cur = m_cur - shift
    m_new = jnp.maximum(m_prev, m_cur)
    alpha = jnp.exp2(m_prev - m_new)
    m_sub = m_new if shift is None else m_new + shift
    ps = [jnp.exp2(t - m_sub) for t in tiles]
    psum = ps[0]
    for p in ps[1:]:
        psum = psum + p
    l_ref[c] = alpha * l_ref[c] + psum
    p = jnp.concatenate([p.astype(BF16) for p in ps], axis=1)
    acc_ref[c] = alpha * acc_ref[c] + _dot(p, v)
    m_ref[c] = m_new


def _pipelined(n, make_s, consume, depth=1):
    queue = [make_s(c) for c in range(min(depth, n))]
    for c in range(n):
        if c + depth < n:
            queue.append(make_s(c + depth))
        consume(c, queue.pop(0))


def _row_sum(l):
    return jnp.sum(l, axis=-1, keepdims=True)


def _causal_pairs(nq, nh):
    qi = np.array([i for i in range(nq) for _ in range(nh * i + nh)], np.int32)
    kj = np.array([j for i in range(nq) for j in range(nh * i + nh)], np.int32)
    return jnp.asarray(qi), jnp.asarray(kj)


def _causal_sweep(i, j, nh, step, finalize):
    d = j - nh * i

    @pl.when(d < 0)
    def _():
        step(("full",) * nh)

    for dd in range(nh):
        @pl.when(d == dd)
        def _(dd=dd):
            step(tuple("skip" if h < dd else ("diag" if h == dd else "full") for h in range(nh)))
            if dd == nh - 1:
                finalize()


def _init_softmax_state(m_ref, l_ref, acc_ref):
    m_ref[...] = jnp.full_like(m_ref, NEG)
    l_ref[...] = jnp.zeros_like(l_ref)
    acc_ref[...] = jnp.zeros_like(acc_ref)


def _fox_kernel(qi_ref, kj_ref, q_ref, k_ref, v_ref, o_ref, m_ref, l_ref, acc_ref, *, tk, nh):
    p = pl.program_id(1)
    i = qi_ref[p]
    j = kj_ref[p]

    @pl.when(j == 0)
    def _():
        _init_softmax_state(m_ref, l_ref, acc_ref)

    def step(modes):
        k = k_ref[0]
        v = v_ref[0]
        combos = [(h, hd) for h in range(nh) if modes[h] != "skip" for hd in range(N_HEADS)]
        if "diag" in modes:
            row = lax.broadcasted_iota(jnp.int32, (tk, tk), 0)
            col = lax.broadcasted_iota(jnp.int32, (tk, tk), 1)
            keep = col <= row

        def make_s(c):
            h, hd = combos[c]
            q = q_ref[0, tk * h:tk * h + tk, LANES * hd:LANES * hd + LANES]
            s = _nt_dot(q, k[:, LANES * hd:LANES * hd + LANES])
            return jnp.where(keep, s, NEG) if modes[h] == "diag" else s

        def consume(c, s):
            h, hd = combos[c]
            pr = hd // 2
            _softmax_step(s, v[:, LANES * pr:LANES * pr + LANES], m_ref, l_ref, acc_ref, N_HEADS * h + hd)

        _pipelined(len(combos), make_s, consume)

    def finalize():
        lane = lax.broadcasted_iota(jnp.int32, (tk, LANES), 1)
        for h in range(nh):
            for pr in range(2):
                c = N_HEADS * h + 2 * pr
                oe = acc_ref[c] / _row_sum(l_ref[c])
                oo = acc_ref[c + 1] / _row_sum(l_ref[c + 1])
                o_ref[0, tk * h:tk * h + tk, LANES * pr:LANES * pr + LANES] = jnp.where(
                    lane < HEAD_DIM, oe, oo).astype(BF16)

    _causal_sweep(i, j, nh, step, finalize)


def _fox_prompt(qa, ka, vb, *, tk, nh):
    b, t, _ = qa.shape
    tq = nh * tk
    qi, kj = _causal_pairs(t // tq, nh)
    q_map = lambda bi, p, qi, kj: (bi, qi[p], 0)
    k_map = lambda bi, p, qi, kj: (bi, kj[p], 0)
    sets = nh * N_HEADS
    return pl.pallas_call(
        functools.partial(_fox_kernel, tk=tk, nh=nh),
        grid_spec=pltpu.PrefetchScalarGridSpec(
            num_scalar_prefetch=2,
            grid=(b, qi.shape[0]),
            in_specs=[pl.BlockSpec((1, tq, 4 * LANES), q_map), pl.BlockSpec((1, tk, 4 * LANES), k_map),
                      pl.BlockSpec((1, tk, GROUP), k_map)],
            out_specs=pl.BlockSpec((1, tq, GROUP), q_map),
            scratch_shapes=[pltpu.VMEM((sets, tk, LANES), F32), pltpu.VMEM((sets, tk, LANES), F32),
                            pltpu.VMEM((sets, tk, LANES), F32)]),
        out_shape=jax.ShapeDtypeStruct((b, t, GROUP), BF16),
        compiler_params=_cparams(("parallel", "arbitrary")),
        name="fox_prompt",
    )(qi, kj, qa, ka, vb)


def _diff_lambda(lp, lam_init):
    a = jnp.sum(lp[0:1, :] * lp[1:2, :], axis=-1, keepdims=True)
    b = jnp.sum(lp[2:3, :] * lp[3:4, :], axis=-1, keepdims=True)
    return jnp.exp(a) - jnp.exp(b) + lam_init


def _diff_finish(acc_ref, l_ref, lam, gsub, lam_init, rows, o_ref, base=0, row0=0):
    lane = lax.broadcasted_iota(jnp.int32, (rows, LANES), 1)
    for pr in range(2):
        outs = []
        for e in range(2):
            c = base + 2 * (2 * pr + e)
            o = acc_ref[c] / _row_sum(l_ref[c]) - lam * (acc_ref[c + 1] / _row_sum(l_ref[c + 1]))
            valid = (lane >= HEAD_DIM * e) & (lane < HEAD_DIM * e + HEAD_DIM)
            ms = jnp.sum(jnp.where(valid, o * o, 0.0), axis=-1, keepdims=True) * (1.0 / HEAD_DIM)
            outs.append(o * lax.rsqrt(ms + EPS) * gsub * (1.0 - lam_init))
        o_ref[0, row0:row0 + rows, LANES * pr:LANES * pr + LANES] = jnp.where(
            lane < HEAD_DIM, outs[0], outs[1]).astype(BF16)


def _alibi_slope(hd):
    return 2.0 ** (-8.0 * (hd + 1) / N_HEADS) * LOG2E


def _alibi_features(tq):
    pos = np.arange(tq, dtype=np.float32)[:, None]
    lane = np.arange(LANES)[None, :]
    rnd = lambda v: v.astype(BF16).astype(np.float32)

    def split(v):
        v = v.astype(np.float32)
        hi = rnd(v)
        mid = rnd(v - hi)
        lo = rnd(v - hi - mid)
        return hi, mid, lo

    def block(e, first, second):
        out = np.zeros((tq, LANES), np.float32)
        for n, val in enumerate(first + second):
            out = np.where(lane == 8 * e + n, val, out)
        return out

    one = (np.ones((tq, 1), np.float32),) * 3
    qa = [block(hd % 2, split(-_alibi_slope(hd) * pos), one) for hd in range(N_HEADS)]
    ka = [block(0, one, split(_alibi_slope(2 * pr) * pos)) + block(1, one, split(_alibi_slope(2 * pr + 1) * pos))
          for pr in range(2)]
    return (jnp.asarray(np.concatenate(qa, axis=1), dtype=BF16),
            jnp.asarray(np.concatenate(ka, axis=1), dtype=BF16))


def _diff_kernel(qi_ref, kj_ref, q_ref, k_ref, v_ref, qa_ref, ka_ref, lam_ref, gsub_ref, o_ref,
                 m_ref, l_ref, acc_ref, *, tk, nh, lam_init):
    p = pl.program_id(1)
    i = qi_ref[p]
    j = kj_ref[p]
    per_half = 2 * N_HEADS

    @pl.when(j == 0)
    def _():
        _init_softmax_state(m_ref, l_ref, acc_ref)

    def step(modes):
        k = k_ref[0]
        v = v_ref[0]
        combos = [(h, c) for h in range(nh) if modes[h] != "skip" for c in range(per_half)]
        if "diag" in modes:
            row = lax.broadcasted_iota(jnp.int32, (tk, tk), 0)
            col = lax.broadcasted_iota(jnp.int32, (tk, tk), 1)
            seen = (col // CHUNK) <= (row // CHUNK)
            ahead = jnp.maximum(col - row, 0).astype(F32)
        gaps = [((nh * i + h - j) * tk).astype(F32) for h in range(nh)]

        def make_s(n):
            h, c = combos[n]
            hd, jm = divmod(c, 2)
            pr, e = divmod(hd, 2)
            idx = 4 * pr + 2 * e + jm
            qx = jnp.concatenate([q_ref[0, tk * h:tk * h + tk, LANES * idx:LANES * idx + LANES],
                                  qa_ref[:, LANES * hd:LANES * hd + LANES]], axis=1)
            kx = jnp.concatenate([k[:, LANES * pr:LANES * pr + LANES], ka_ref[:, LANES * pr:LANES * pr + LANES]],
                                 axis=1)
            s = _nt_dot(qx, kx)
            if modes[h] == "diag":
                s = jnp.where(seen, s - (2.0 * _alibi_slope(hd)) * ahead, NEG)
            return s

        def consume(n, s):
            h, c = combos[n]
            pr = c // 4
            shift = None if modes[h] == "diag" else _alibi_slope(c // 2) * gaps[h]
            _softmax_step(s, v[:, LANES * pr:LANES * pr + LANES], m_ref, l_ref, acc_ref, per_half * h + c, shift)

        _pipelined(len(combos), make_s, consume)

    def finalize():
        lam = _diff_lambda(lam_ref[...], lam_init)
        for h in range(nh):
            _diff_finish(acc_ref, l_ref, lam, gsub_ref[...], lam_init, tk, o_ref, base=per_half * h, row0=tk * h)

    _causal_sweep(i, j, nh, step, finalize)


def _diff_prompt(qv, kb, vb, lam_p, gsub, *, tk, nh, lam_init):
    b, t, _ = qv.shape
    tq = nh * tk
    qi, kj = _causal_pairs(t // tq, nh)
    qa, ka = _alibi_features(tk)
    q_map = lambda bi, p, qi, kj: (bi, qi[p], 0)
    k_map = lambda bi, p, qi, kj: (bi, kj[p], 0)
    const = lambda bi, p, qi, kj: (0, 0)
    sets = nh * 2 * N_HEADS
    return pl.pallas_call(
        functools.partial(_diff_kernel, tk=tk, nh=nh, lam_init=lam_init),
        grid_spec=pltpu.PrefetchScalarGridSpec(
            num_scalar_prefetch=2,
            grid=(b, qi.shape[0]),
            in_specs=[pl.BlockSpec((1, tq, 8 * LANES), q_map), pl.BlockSpec((1, tk, GROUP), k_map),
                      pl.BlockSpec((1, tk, GROUP), k_map),
                      pl.BlockSpec((tk, N_HEADS * LANES), const), pl.BlockSpec((tk, 2 * LANES), const),
                      pl.BlockSpec((4, DIFF_HALF), const), pl.BlockSpec((1, LANES), const)],
            out_specs=pl.BlockSpec((1, tq, GROUP), q_map),
            scratch_shapes=[pltpu.VMEM((sets, tk, LANES), F32), pltpu.VMEM((sets, tk, LANES), F32),
                            pltpu.VMEM((sets, tk, LANES), F32)]),
        out_shape=jax.ShapeDtypeStruct((b, t, GROUP), BF16),
        compiler_params=_cparams(("parallel", "arbitrary")),
        name="diff_prompt",
    )(qi, kj, qv, kb, vb, qa, ka, lam_p, gsub)


def _band_bias_kernel(g_ref, bp_ref, bs_ref):
    lane = lax.broadcasted_iota(jnp.int32, (CHUNK, BAND_WIN), 1)
    width = g_ref.shape[1]
    for hd in range(N_HEADS):
        x = jnp.broadcast_to(g_ref[hd:hd + 1, :], (CHUNK, width))
        rp = pltpu.roll(x, width - CHUNK + 1, axis=1, stride=1, stride_axis=0)
        bp_ref[hd] = jnp.where(lane < CHUNK, NEG, rp[:, :BAND_WIN] * LOG2E)
        rs = pltpu.roll(x, width - 2 * CHUNK + 1, axis=1, stride=1, stride_axis=0)
        bs_ref[hd] = rs[:, :BAND_WIN] * LOG2E


def _band_bias(table):
    gv = jnp.concatenate([jnp.broadcast_to(table[:, 2 * REL_CLIP:], (N_HEADS, 4 * LANES)),
                          table[:, 2 * REL_CLIP - 1::-1]], axis=1)
    shp = jax.ShapeDtypeStruct((N_HEADS, CHUNK, BAND_WIN), F32)
    return pl.pallas_call(_band_bias_kernel, out_shape=(shp, shp), name="band_bias")(gv)


def _band_kernel(q_ref, k_ref, v_ref, bias_ref, o_ref, kpad_ref, vpad_ref, *, tq):
    t = pl.program_id(1)
    n_sub = tq // CHUNK
    lane = lax.broadcasted_iota(jnp.int32, (CHUNK, LANES), 1)
    col = lax.broadcasted_iota(jnp.int32, (CHUNK, BAND_WIN), 1)

    @pl.when(t == 0)
    def _():
        for src, dst in ((k_ref, kpad_ref), (v_ref, vpad_ref)):
            dst[0:BAND_WIN, :] = jnp.zeros((BAND_WIN, GROUP), BF16)
            dst[BAND_WIN:, :] = src[0]

    def window(ref, c, pr):
        start = pl.multiple_of((t * n_sub + c + 1) * CHUNK, CHUNK)
        return ref[pl.ds(start, BAND_WIN), LANES * pr:LANES * pr + LANES]

    def make_s(idx):
        c, hd = divmod(idx, N_HEADS)
        q = q_ref[0, CHUNK * c:CHUNK * c + CHUNK, LANES * hd:LANES * hd + LANES]
        valid = col + (t * n_sub + c - BAND_CHUNKS - 1) * CHUNK >= 0
        return jnp.where(valid, _nt_dot(q, window(kpad_ref, c, hd // 2)) + bias_ref[hd], NEG)

    even = {}

    def consume(idx, s):
        c, hd = divmod(idx, N_HEADS)
        pr, e = divmod(hd, 2)
        m = jnp.max(s, axis=-1, keepdims=True)
        p = jnp.exp2(s - m)
        l = jnp.sum(p, axis=-1, keepdims=True)
        o = _dot(p.astype(BF16), window(vpad_ref, c, pr)) / l
        if e == 0:
            even[pr] = o
        else:
            o_ref[0, CHUNK * c:CHUNK * c + CHUNK, LANES * pr:LANES * pr + LANES] = jnp.where(
                lane < HEAD_DIM, even[pr], o).astype(BF16)

    _pipelined(n_sub * N_HEADS, make_s, consume, depth=BAND_LOOKAHEAD)


def _band_prompt(qm, kb, vb, bias, *, tq):
    b, t, _ = qm.shape
    return pl.pallas_call(
        functools.partial(_band_kernel, tq=tq),
        grid=(b, t // tq),
        in_specs=[pl.BlockSpec((1, tq, 4 * LANES), lambda bi, i: (bi, i, 0)),
                  pl.BlockSpec((1, t, GROUP), lambda bi, i: (bi, 0, 0)),
                  pl.BlockSpec((1, t, GROUP), lambda bi, i: (bi, 0, 0)),
                  pl.BlockSpec((N_HEADS, CHUNK, BAND_WIN), lambda bi, i: (0, 0, 0))],
        out_specs=pl.BlockSpec((1, tq, GROUP), lambda bi, i: (bi, i, 0)),
        out_shape=jax.ShapeDtypeStruct((b, t, GROUP), BF16),
        scratch_shapes=[pltpu.VMEM((BAND_WIN + t, GROUP), BF16), pltpu.VMEM((BAND_WIN + t, GROUP), BF16)],
        compiler_params=_cparams(("arbitrary", "arbitrary")),
        name="band_prompt",
    )(qm, kb, vb, bias)


def _mlstm_kernel(mqk_ref, mv_ref, gates_ref, cum_ref, mos_ref, cw_ref, cb_ref, gmh_ref, shead_ref, bd_ref,
                  c0_ref, n0_ref, m0_ref, conv0_ref,
                  o_ref, c_out_ref, n_out_ref, m_out_ref,
                  cbuf_ref, c_ref, n_ref, m_ref, *, rows, valid, ns):
    t = pl.program_id(1)

    @pl.when(t == 0)
    def _():
        for s in range(ns):
            c_ref[s] = c0_ref[s]
            n_ref[s] = n0_ref[s]
            m_ref[s] = m0_ref[s]
            cbuf_ref[s, 0:8, :] = conv0_ref[s]

    for s in range(ns):
        _mlstm_chunk(mqk_ref.at[s], mv_ref.at[s], gates_ref.at[s], cum_ref.at[s], mos_ref.at[s], cw_ref, cb_ref,
                     gmh_ref, shead_ref, bd_ref, o_ref.at[s], cbuf_ref.at[s], c_ref.at[s], n_ref.at[s],
                     m_ref.at[s], rows=rows, valid=valid)

    @pl.when(t == pl.num_programs(1) - 1)
    def _():
        for s in range(ns):
            c_out_ref[s] = c_ref[s]
            n_out_ref[s] = n_ref[s]
            m_out_ref[s] = m_ref[s]


def _mlstm_chunk(mqk_ref, mv_ref, gates_ref, cum_ref, mos_ref, cw_ref, cb_ref, gmh_ref, shead_ref, bd_ref,
                 o_ref, cbuf_ref, c_ref, n_ref, m_ref, *, rows, valid):
    def padded(a, fill=0.0):
        if valid == rows:
            return a
        return jnp.concatenate([a, jnp.full((rows - valid, a.shape[1]), fill, a.dtype)], axis=0)

    u = padded(mqk_ref[...])
    cbuf_ref[8:8 + rows, :] = u
    y = cb_ref[...] + cw_ref[3:4, :] * u
    for jw in range(CONV_W - 1):
        y = y + cw_ref[jw:jw + 1, :] * cbuf_ref[5 + jw:5 + jw + rows, :]
    cbuf_ref[0:8, :] = cbuf_ref[valid:valid + 8, :]
    qk = y * jax.nn.sigmoid(y)
    q = qk[:, 0:GROUP]
    k = qk[:, GROUP:2 * GROUP] * (HEAD_DIM ** -0.5)
    qb = q.astype(BF16)
    kb = k.astype(BF16)
    v = padded(mv_ref[...])
    mos = padded(mos_ref[...])

    g = gates_ref[...]
    cm = cum_ref[...]
    if valid != rows:
        g = padded(g, NEG)
        cm = jnp.concatenate([cm, jnp.broadcast_to(cm[valid - 1:valid, :], (rows - valid, LANES))], axis=0)
    g_t = g.T
    cm_t = cm.T

    row = lax.broadcasted_iota(jnp.int32, (rows, rows), 0)
    col = lax.broadcasted_iota(jnp.int32, (rows, rows), 1)
    causal = col <= row
    lane_g = lax.broadcasted_iota(jnp.int32, (rows, GROUP), 1)
    lane_r = lax.broadcasted_iota(jnp.int32, (1, GROUP), 1)
    m_prev_all = m_ref[...]

    zeros = jnp.zeros((rows, GROUP), F32)
    num, a_full, ws_full, mt_full, wend_full = zeros, zeros, zeros, zeros, zeros
    decay_lane = jnp.zeros((1, GROUP), F32)
    mnew_lane = jnp.zeros((1, GROUP), F32)
    head_masks = [(lane_g >= HEAD_DIM * hd) & (lane_g < HEAD_DIM * hd + HEAD_DIM) for hd in range(N_HEADS)]
    sqks = [_nt_dot(jnp.where(hm, q, 0.0).astype(BF16), kb) for hm in head_masks]
    c_old = c_ref[...]
    n_old = n_ref[...]
    shead = shead_ref[...]
    q_c = _dot(qb, c_old.astype(BF16))
    q_n = _dot((q * n_old).astype(BF16), shead)
    for hd in range(N_HEADS):
        hm = head_masks[hd]
        hm_r = (lane_r >= HEAD_DIM * hd) & (lane_r < HEAD_DIM * hd + HEAD_DIM)
        f_c = cm[:, 2 * N_HEADS + hd:2 * N_HEADS + hd + 1]
        ig_c = g[:, N_HEADS + hd:N_HEADS + hd + 1]
        f_r = cm_t[2 * N_HEADS + hd:2 * N_HEADS + hd + 1, :]
        ig_r = g_t[N_HEADS + hd:N_HEADS + hd + 1, :]
        m_prev = m_prev_all[:, HEAD_DIM * hd:HEAD_DIM * hd + 1]
        logw = jnp.where(causal, f_c - (f_r - ig_r), NEG)
        logb = f_c + m_prev
        m_t = jnp.maximum(logb, jnp.max(logw, axis=-1, keepdims=True))
        w = jnp.exp(logw - m_t) * sqks[hd]
        a = jnp.exp(logb - m_t)
        num = jnp.where(hm, _dot(w.astype(BF16), v), num)
        a_full = jnp.where(hm, a, a_full)
        ws_full = jnp.where(hm, jnp.sum(w, axis=-1, keepdims=True), ws_full)
        mt_full = jnp.where(hm, m_t, mt_full)
        f_end = f_c[rows - 1:rows, :]
        log_end = f_end - f_c + ig_c
        m_new = jnp.maximum(f_end + m_prev, jnp.max(log_end, axis=0, keepdims=True))
        wend_full = jnp.where(hm, jnp.exp(log_end - m_new), wend_full)
        decay_lane = jnp.where(hm_r, jnp.exp(f_end + m_prev - m_new), decay_lane)
        mnew_lane = jnp.where(hm_r, m_new, mnew_lane)

    num = num + a_full * q_c
    den = ws_full + a_full * q_n
    h = num / jnp.maximum(jnp.abs(den), jnp.exp(-mt_full))
    o = mos * h
    ss = _dot((o * o).astype(BF16), shead)
    on = o * lax.rsqrt(ss * (1.0 / HEAD_DIM) + EPS) * gmh_ref[...]
    o_ref[...] = on[0:valid, :].astype(BF16)

    wk = k * wend_full
    kv = lax.dot_general(wk.astype(BF16), v, (((0,), (0,)), ((), ())), preferred_element_type=F32)
    c_ref[...] = c_old * decay_lane + jnp.where(bd_ref[...] > 0.5, kv, 0.0)
    n_ref[...] = n_old * decay_lane + jnp.sum(wk, axis=0, keepdims=True)
    m_ref[...] = mnew_lane


def _mlstm(mqk, mvb, gates, cum, mos, cw, cb, gmh, shead, bd, c0, n0, m0, conv0, *, nb, t, valid, rows, ns):
    nc = t // valid
    seq = lambda a: a.reshape(nb, t, a.shape[-1])
    tok = lambda bi, i: (bi, i, 0)
    const = lambda bi, i: (0, 0)
    per_b = lambda bi, i: (bi, 0, 0)
    o, c_new, n_new, m_new = pl.pallas_call(
        functools.partial(_mlstm_kernel, rows=rows, valid=valid, ns=ns),
        grid=(nb // ns, nc),
        in_specs=[pl.BlockSpec((ns, valid, 2 * GROUP), tok), pl.BlockSpec((ns, valid, GROUP), tok),
                  pl.BlockSpec((ns, valid, LANES), tok), pl.BlockSpec((ns, valid, LANES), tok),
                  pl.BlockSpec((ns, valid, GROUP), tok),
                  pl.BlockSpec((CONV_W, 2 * GROUP), const), pl.BlockSpec((1, 2 * GROUP), const),
                  pl.BlockSpec((1, GROUP), const), pl.BlockSpec((GROUP, GROUP), const),
                  pl.BlockSpec((GROUP, GROUP), const),
                  pl.BlockSpec((ns, GROUP, GROUP), per_b), pl.BlockSpec((ns, 1, GROUP), per_b),
                  pl.BlockSpec((ns, 1, GROUP), per_b), pl.BlockSpec((ns, 8, 2 * GROUP), per_b)],
        out_specs=[pl.BlockSpec((ns, valid, GROUP), tok), pl.BlockSpec((ns, GROUP, GROUP), per_b),
                   pl.BlockSpec((ns, 1, GROUP), per_b), pl.BlockSpec((ns, 1, GROUP), per_b)],
        out_shape=[jax.ShapeDtypeStruct((nb, t, GROUP), BF16), jax.ShapeDtypeStruct((nb, GROUP, GROUP), F32),
                   jax.ShapeDtypeStruct((nb, 1, GROUP), F32), jax.ShapeDtypeStruct((nb, 1, GROUP), F32)],
        scratch_shapes=[pltpu.VMEM((ns, 8 + rows, 2 * GROUP), F32), pltpu.VMEM((ns, GROUP, GROUP), F32),
                        pltpu.VMEM((ns, 1, GROUP), F32), pltpu.VMEM((ns, 1, GROUP), F32)],
        compiler_params=_cparams(("parallel", "arbitrary")),
        name="mlstm",
    )(seq(mqk), seq(mvb), seq(gates), seq(cum), seq(mos), cw, cb, gmh, shead, bd, c0, n0, m0, conv0)
    return o.reshape(nb * t, GROUP), c_new, n_new, m_new


def _pad_rows(a, rows):
    return jnp.concatenate([a, jnp.zeros((rows - a.shape[0], a.shape[1]), a.dtype)], axis=0)


def _two_part_softmax(s_c, s_n, vt_c, v_n):
    m = jnp.maximum(jnp.max(s_c, axis=-1, keepdims=True), jnp.max(s_n, axis=-1, keepdims=True))
    p_c = jnp.exp2(s_c - m)
    p_n = jnp.exp2(s_n - m)
    l = jnp.sum(p_c, axis=-1, keepdims=True) + jnp.sum(p_n, axis=-1, keepdims=True)
    return _nt_dot(p_c.astype(BF16), vt_c) + _dot(p_n.astype(BF16), v_n), l


def _fox_sample_kernel(q_ref, kn_ref, vn_ref, ck_ref, cv_ref, clf_ref, lstr_ref, o_ref, *, tn, past):
    q = q_ref[0]
    kn = _pad_rows(kn_ref[0], LANES)
    vn = _pad_rows(vn_ref[0], LANES)
    ck = ck_ref[0, 0]
    cv = cv_ref[0, 0].astype(BF16)
    x = _pad_rows(clf_ref[0, 0], 8)
    xhi, xmid, xlo = _split3(x)
    lstr = lstr_ref[...]
    suf = _dot(xhi, lstr) + _dot(xmid, lstr) + _dot(xlo, lstr)

    sub = lax.broadcasted_iota(jnp.int32, (8, past), 0)
    lane_o = lax.broadcasted_iota(jnp.int32, (tn, LANES), 1)
    row = lax.broadcasted_iota(jnp.int32, (tn, LANES), 0)
    causal = lane_o <= row
    fill = jnp.zeros((LANES - HEAD_DIM - 8, past), F32)

    def logits(hd):
        sb = jnp.broadcast_to(suf[hd:hd + 1, :], (8, past)) * LOG2E
        shi = sb.astype(BF16).astype(F32)
        r = sb - shi
        smid = r.astype(BF16).astype(F32)
        slo = (r - smid).astype(BF16).astype(F32)
        aug = jnp.where(sub < 3, 1.0,
                        jnp.where(sub == 3, shi, jnp.where(sub == 4, smid, jnp.where(sub == 5, slo, 0.0))))
        kc = jnp.concatenate([ck[HEAD_DIM * hd:HEAD_DIM * hd + HEAD_DIM, :], aug, fill], axis=0).astype(BF16)
        qh = q[:, LANES * hd:LANES * hd + LANES]
        return _dot(qh, kc), jnp.where(causal, _nt_dot(qh, kn[:, LANES * hd:LANES * hd + LANES]), NEG)

    even = {}

    def attend(hd, s):
        pr, e = divmod(hd, 2)
        acc, l = _two_part_softmax(s[0], s[1], cv[LANES * pr:LANES * pr + LANES, :],
                                   vn[:, LANES * pr:LANES * pr + LANES])
        if e == 0:
            even[pr] = acc / l
        else:
            o_ref[0, :, LANES * pr:LANES * pr + LANES] = jnp.where(lane_o < HEAD_DIM, even[pr], acc / l).astype(BF16)

    _pipelined(N_HEADS, logits, attend, depth=SAMPLE_LOOKAHEAD)


def _fox_sample_multi(q_ref, kn_ref, vn_ref, ck_ref, cv_ref, clf_ref, lstr_ref, o_ref, *, tn, past, nsb):
    for s in range(nsb):
        one = pl.ds(s, 1)
        _fox_sample_kernel(q_ref.at[one], kn_ref.at[one], vn_ref.at[one], ck_ref.at[:, one], cv_ref.at[:, one],
                           clf_ref.at[:, one], lstr_ref, o_ref.at[one], tn=tn, past=past)


def _diff_sample_multi(q_ref, kn_ref, vn_ref, ck_ref, cv_ref, lam_ref, gsub_ref, o_ref, l_ref, acc_ref,
                       *, tn, past, lam_init, nsb):
    sets = 2 * N_HEADS
    for s in range(nsb):
        one = pl.ds(s, 1)
        mine = pl.ds(sets * s, sets)
        _diff_sample_kernel(q_ref.at[one], kn_ref.at[one], vn_ref.at[one], ck_ref.at[:, one], cv_ref.at[:, one],
                            lam_ref, gsub_ref, o_ref.at[one], l_ref.at[mine], acc_ref.at[mine],
                            tn=tn, past=past, lam_init=lam_init)


def _band_sample_multi(q_ref, kn_ref, vn_ref, ck_ref, cv_ref, bias_ref, o_ref, *, tn, past, nsb):
    for s in range(nsb):
        one = pl.ds(s, 1)
        _band_sample_kernel(q_ref.at[one], kn_ref.at[one], vn_ref.at[one], ck_ref.at[:, one], cv_ref.at[:, one],
                            bias_ref, o_ref.at[one], tn=tn, past=past)


def _fox_sample(qa, ka, vb, ck, cv, clf, lstr, *, layer):
    b, tn, _ = qa.shape
    past = ck.shape[3]
    nsb = math.gcd(b, SAMPLE_SEQS)
    per_b = lambda i: (i, 0, 0)
    cache_b = lambda i: (layer, i, 0, 0)
    return pl.pallas_call(
        functools.partial(_fox_sample_multi, tn=tn, past=past, nsb=nsb),
        grid=(b // nsb,),
        in_specs=[pl.BlockSpec((nsb, tn, 4 * LANES), per_b), pl.BlockSpec((nsb, tn, 4 * LANES), per_b),
                  pl.BlockSpec((nsb, tn, GROUP), per_b), pl.BlockSpec((1, nsb, GROUP, past), cache_b),
                  pl.BlockSpec((1, nsb, GROUP, past), cache_b), pl.BlockSpec((1, nsb, N_HEADS, past), cache_b),
                  pl.BlockSpec((past, past), lambda i: (0, 0))],
        out_specs=pl.BlockSpec((nsb, tn, GROUP), per_b),
        out_shape=jax.ShapeDtypeStruct((b, tn, GROUP), BF16),
        compiler_params=_cparams(("parallel",)),
        name="fox_sample",
    )(qa, ka, vb, ck, cv, clf, lstr)


def _diff_sample_kernel(q_ref, kn_ref, vn_ref, ck_ref, cv_ref, lam_ref, gsub_ref, o_ref, l_ref, acc_ref,
                        *, tn, past, lam_init):
    q = q_ref[0]
    kn = _pad_rows(kn_ref[0], LANES)
    vn = _pad_rows(vn_ref[0], LANES)
    ck = ck_ref[0, 0].astype(BF16)
    cv = cv_ref[0, 0].astype(BF16)
    row_c = lax.broadcasted_iota(jnp.int32, (tn, past), 0)
    col_c = lax.broadcasted_iota(jnp.int32, (tn, past), 1)
    dist_c = (past + row_c - col_c).astype(F32)
    row_n = lax.broadcasted_iota(jnp.int32, (tn, LANES), 0)
    col_n = lax.broadcasted_iota(jnp.int32, (tn, LANES), 1)
    dist_n = jnp.abs(row_n - col_n).astype(F32)
    real = col_n < tn

    def logits(c):
        hd, jm = divmod(c, 2)
        pr, e = divmod(hd, 2)
        slope = _alibi_slope(hd)
        idx = 4 * pr + 2 * e + jm
        qh = q[:, LANES * idx:LANES * idx + LANES]
        s_c = _dot(qh, ck[LANES * pr:LANES * pr + LANES, :]) - slope * dist_c
        s_n = jnp.where(real, _nt_dot(qh, kn[:, LANES * pr:LANES * pr + LANES]) - slope * dist_n, NEG)
        return s_c, s_n

    def attend(c, s):
        pr = c // 4
        acc, l = _two_part_softmax(s[0], s[1], cv[LANES * pr:LANES * pr + LANES, :],
                                   vn[:, LANES * pr:LANES * pr + LANES])
        acc_ref[c] = acc
        l_ref[c] = l

    _pipelined(2 * N_HEADS, logits, attend, depth=SAMPLE_LOOKAHEAD)
    lam = _diff_lambda(lam_ref[...], lam_init)
    _diff_finish(acc_ref, l_ref, lam, gsub_ref[...], lam_init, tn, o_ref)


def _diff_sample(qv, kb, vb, ck, cv, lam_p, gsub, *, lam_init, layer):
    b, tn, _ = qv.shape
    past = ck.shape[3]
    nsb = math.gcd(b, SAMPLE_SEQS)
    per_b = lambda i: (i, 0, 0)
    cache_b = lambda i: (layer, i, 0, 0)
    return pl.pallas_call(
        functools.partial(_diff_sample_multi, tn=tn, past=past, lam_init=lam_init, nsb=nsb),
        grid=(b // nsb,),
        in_specs=[pl.BlockSpec((nsb, tn, 8 * LANES), per_b), pl.BlockSpec((nsb, tn, GROUP), per_b),
                  pl.BlockSpec((nsb, tn, GROUP), per_b), pl.BlockSpec((1, nsb, GROUP, past), cache_b),
                  pl.BlockSpec((1, nsb, GROUP, past), cache_b), pl.BlockSpec((4, DIFF_HALF), lambda i: (0, 0)),
                  pl.BlockSpec((1, LANES), lambda i: (0, 0))],
        out_specs=pl.BlockSpec((nsb, tn, GROUP), per_b),
        out_shape=jax.ShapeDtypeStruct((b, tn, GROUP), BF16),
        scratch_shapes=[pltpu.VMEM((nsb * 2 * N_HEADS, tn, 1), F32),
                        pltpu.VMEM((nsb * 2 * N_HEADS, tn, LANES), F32)],
        compiler_params=_cparams(("parallel",)),
        name="diff_sample",
    )(qv, kb, vb, ck, cv, lam_p, gsub)


def _band_sample_kernel(q_ref, kn_ref, vn_ref, ck_ref, cv_ref, bias_ref, o_ref, *, tn, past):
    q = q_ref[0]
    kn = _pad_rows(kn_ref[0], LANES)
    vn = _pad_rows(vn_ref[0], LANES)
    ck = ck_ref[0, 0].astype(BF16)
    cv = cv_ref[0, 0].astype(BF16)
    lane = lax.broadcasted_iota(jnp.int32, (tn, LANES), 1)
    real = lane < tn

    def logits(hd):
        pr = hd // 2
        qh = q[:, LANES * hd:LANES * hd + LANES]
        s_c = _dot(qh, ck[LANES * pr:LANES * pr + LANES, :]) + bias_ref[hd, 0:tn, 0:past]
        s_n = _nt_dot(qh, kn[:, LANES * pr:LANES * pr + LANES]) + bias_ref[hd, 0:tn, past:past + LANES]
        return s_c, jnp.where(real, s_n, NEG)

    even = {}

    def attend(hd, s):
        pr, e = divmod(hd, 2)
        acc, l = _two_part_softmax(s[0], s[1], cv[LANES * pr:LANES * pr + LANES, :],
                                   vn[:, LANES * pr:LANES * pr + LANES])
        if e == 0:
            even[pr] = acc / l
        else:
            o_ref[0, :, LANES * pr:LANES * pr + LANES] = jnp.where(lane < HEAD_DIM, even[pr], acc / l).astype(BF16)

    _pipelined(N_HEADS, logits, attend, depth=SAMPLE_LOOKAHEAD)


def _band_sample(qm, kb, vb, ck, cv, bias, *, layer):
    b, tn, _ = qm.shape
    past = ck.shape[3]
    nsb = math.gcd(b, SAMPLE_SEQS)
    per_b = lambda i: (i, 0, 0)
    cache_b = lambda i: (layer, i, 0, 0)
    return pl.pallas_call(
        functools.partial(_band_sample_multi, tn=tn, past=past, nsb=nsb),
        grid=(b // nsb,),
        in_specs=[pl.BlockSpec((nsb, tn, 4 * LANES), per_b), pl.BlockSpec((nsb, tn, GROUP), per_b),
                  pl.BlockSpec((nsb, tn, GROUP), per_b), pl.BlockSpec((1, nsb, GROUP, past), cache_b),
                  pl.BlockSpec((1, nsb, GROUP, past), cache_b),
                  pl.BlockSpec((N_HEADS, CHUNK, BAND_WIN), lambda i: (0, 0, 0))],
        out_specs=pl.BlockSpec((nsb, tn, GROUP), per_b),
        out_shape=jax.ShapeDtypeStruct((b, tn, GROUP), BF16),
        compiler_params=_cparams(("parallel",)),
        name="band_sample",
    )(qm, kb, vb, ck, cv, bias)


def _ffn_kernel(x_ref, gt1_ref, fox_ref, diff_ref, ml_ref, band_ref, wo_ref, sh_ref, sc_ref, gt_ref, g2_ref,
                wg_ref, wu_ref, wd_ref, o_ref, acc_ref, *, bb, tt, tf):
    tm = bb * tt
    mix = _dot(fox_ref[...], wo_ref[0, 0:GROUP, :])
    mix += _dot(diff_ref[...], wo_ref[0, GROUP:2 * GROUP, :])
    mix += _dot(ml_ref[...], wo_ref[0, 2 * GROUP:3 * GROUP, :])
    mix += _dot(band_ref[...], wo_ref[0, 3 * GROUP:4 * GROUP, :])
    x = x_ref[...] + gt1_ref[0, :, 0] * mix.reshape(bb, tt, D_MODEL)
    ms = jnp.mean(x * x, axis=-1, keepdims=True)
    h = x * lax.rsqrt(ms + EPS) * g2_ref[...]
    h = h * (1.0 + sc_ref[0, :, 0]) + sh_ref[0, :, 0]
    hb = h.reshape(tm, D_MODEL).astype(BF16)

    def gate_up(f):
        cols = slice(tf * f, tf * f + tf)
        return _dot(hb, wg_ref[0, :, cols]), _dot(hb, wu_ref[0, :, cols])

    def down(f, gu):
        g, u = gu
        a = (g * jax.nn.sigmoid(g) * u).astype(BF16)
        part = _dot(a, wd_ref[0, tf * f:tf * f + tf, :])
        if f == 0:
            acc_ref[...] = part
        else:
            acc_ref[...] += part

    _pipelined(D_FF // tf, gate_up, down)
    o_ref[...] = x + gt_ref[0, :, 0] * acc_ref[...].reshape(bb, tt, D_MODEL)


def _out_ffn(x, mod, row0, o_fox, o_diff, o_ml, o_band, w_out, g2, wg, wu, wd, *, bb, tt, tf, layer):
    bx, tx, d = x.shape
    nb, nt = bx // bb, tx // tt
    tm = bb * tt
    xs = pl.BlockSpec((bb, tt, d), lambda i, t: (i, t, 0))
    ms = lambda comp: _mod_spec(bb, d, layer, row0, comp)
    mixer = pl.BlockSpec((tm, GROUP), lambda i, t: (i * nt + t, 0))
    resident = lambda shape: pl.BlockSpec((1,) + shape, lambda i, t: (layer, 0, 0), pipeline_mode=pl.Buffered(1))
    return pl.pallas_call(
        functools.partial(_ffn_kernel, bb=bb, tt=tt, tf=tf),
        grid=(nb, nt),
        in_specs=[xs, ms(2), mixer, mixer, mixer, mixer, resident((d, d)),
                  ms(3), ms(4), ms(5), pl.BlockSpec((1, d), lambda i, t: (0, 0)),
                  resident((d, D_FF)), resident((d, D_FF)), resident((D_FF, d))],
        out_specs=xs,
        out_shape=jax.ShapeDtypeStruct(x.shape, F32),
        scratch_shapes=[pltpu.VMEM((tm, d), F32)],
        compiler_params=_cparams(("parallel", "parallel")),
        name="out_ffn",
    )(x, mod, o_fox, o_diff, o_ml, o_band, w_out, mod, mod, mod, g2, wg, wu, wd)


def _consts(seg):
    r = np.arange(GROUP)
    s64 = (r[:, None] // HEAD_DIM == r[None, :] // HEAD_DIM)
    s32 = (r[:, None] // DIFF_HALF == r[None, :] // DIFF_HALF)
    q = np.arange(LANES)
    tril = (q[None, :] <= q[:, None]) & (q[:, None] // seg == q[None, :] // seg)
    as_bf16 = lambda m: jnp.asarray(m.astype(np.float32), dtype=BF16)
    return as_bf16(s64), as_bf16(s32), as_bf16(tril), jnp.asarray(s64.astype(np.float32))


def _prep_layer(l, norm1_g, norm2_g, w_in_g, b_in, qk_g_fox, qk_g_diff, qk_g_band, conv_w, conv_b,
                diff_lambda, diff_subln_g, mlstm_norm_g, band_rel_bias, w_out, w_ffn_gate, w_ffn_up,
                w_ffn_down):
    starts = np.concatenate([[0], np.cumsum(IN_SPLIT_SIZES)]).tolist()
    order = list(_FULL_GROUPS) + list(_GATE_GROUPS)
    bl = b_in[l]
    b_cols = [bl[starts[g]:starts[g + 1]] for g in order]
    pad = N_IN_PAD - N_FULL - N_GATES
    b_cols.append(jnp.zeros((pad,), F32))
    gains = jnp.stack([
        jnp.tile(qk_g_fox[l, 0], N_HEADS), jnp.tile(qk_g_fox[l, 1], N_HEADS),
        jnp.tile(qk_g_diff[l, 0], 2 * N_HEADS), jnp.tile(qk_g_diff[l, 1], 2 * N_HEADS),
        jnp.tile(qk_g_band[l, 0], N_HEADS), jnp.tile(qk_g_band[l, 1], N_HEADS),
        jnp.zeros((GROUP,), F32), jnp.zeros((GROUP,), F32)])
    return dict(
        g1=norm1_g[l].reshape(1, D_MODEL), g2=norm2_g[l].reshape(1, D_MODEL),
        w_in=w_in_g,
        b_in=jnp.concatenate(b_cols).reshape(1, N_IN_PAD),
        gains=gains, conv_w=conv_w[l], conv_b=conv_b[l].reshape(1, 2 * GROUP),
        lam_p=diff_lambda[l], gsub=jnp.tile(diff_subln_g[l], 2).reshape(1, LANES),
        gmh=jnp.tile(mlstm_norm_g[l], N_HEADS).reshape(1, GROUP),
        table=band_rel_bias[l],
        w_out=w_out, wg=w_ffn_gate, wu=w_ffn_up, wd=w_ffn_down,
        lam_init=0.8 - 0.6 * math.exp(-0.3 * l))


def _layer(x, mod, row0, lp, caches, *, bb, tt, layer, depth, prev_state=None):
    bx, tx, _ = x.shape
    n = bx * tx
    prompt = caches is None
    s64, s32, tril, bd = _consts(tx if prompt else min(tx, LANES))
    z, big = _in_proj(x, mod, row0, lp["g1"], lp["w_in"], lp["b_in"], lp["gains"], s64, s32, tril,
                      bb=bb, tt=tt, running=prompt, ml_blocks=ML_CHUNK // LANES,
                      layer=layer, depth=depth, prev_state=prev_state)
    r3 = lambda a: a.reshape(bx, tx, a.shape[-1])
    bias_p, bias_s = _band_bias(lp["table"])

    if prompt:
        tq = 512
        nh = Q_SUBTILES if tx % (Q_SUBTILES * tq) == 0 else 1
        o_fox = _fox_prompt(r3(z["fqa"]), r3(z["fka"]), r3(z["fvb"]), tk=tq, nh=nh)
        o_diff = _diff_prompt(r3(z["dqv"]), r3(z["dkb"]), r3(z["dvb"]), lp["lam_p"], lp["gsub"],
                              tk=tq, nh=nh, lam_init=lp["lam_init"])
        o_band = _band_prompt(r3(z["bqm"]), r3(z["bkb"]), r3(z["bvb"]), bias_p, tq=tq)
        c0 = jnp.zeros((bx, GROUP, GROUP), F32)
        n0 = jnp.zeros((bx, 1, GROUP), F32)
        m0 = jnp.zeros((bx, 1, GROUP), F32)
        conv0 = jnp.zeros((bx, 8, 2 * GROUP), F32)
        ml_valid, ml_rows = ML_CHUNK, ML_CHUNK
    else:
        (c_fk, c_fv, c_flf, c_dk, c_dv, c_bk, c_bv, s_c, s_n, s_m, s_conv) = caches
        past = c_fk.shape[3]
        assert past % CHUNK == 0 and tx <= CHUNK
        jj = np.arange(past)
        lstr = jnp.asarray((jj[:, None] > jj[None, :]).astype(np.float32), dtype=BF16)
        o_fox = _fox_sample(r3(z["fqa"]), r3(z["fka"]), r3(z["fvb"]), c_fk, c_fv, c_flf, lstr, layer=layer)
        o_diff = _diff_sample(r3(z["dqv"]), r3(z["dkb"]), r3(z["dvb"]), c_dk, c_dv,
                              lp["lam_p"], lp["gsub"], lam_init=lp["lam_init"], layer=layer)
        o_band = _band_sample(r3(z["bqm"]), r3(z["bkb"]), r3(z["bvb"]), c_bk, c_bv, bias_s, layer=layer)
        eye = jnp.eye(N_HEADS, dtype=F32)
        c0 = (s_c[:, :, :, None, :] * eye[None, :, None, :, None]).reshape(bx, GROUP, GROUP)
        n0 = s_n.reshape(bx, 1, GROUP)
        m0 = jnp.repeat(s_m, HEAD_DIM, axis=-1).reshape(bx, 1, GROUP)
        conv0 = jnp.pad(s_conv, ((0, 0), (8 - (CONV_W - 1), 0), (0, 0)))
        ml_valid, ml_rows = tx, ML_PAD
    o_ml, c_new, n_new, m_new = _mlstm(
        z["mqk"], z["mvb"], z["gates"], z["cum"], z["mos"], lp["conv_w"], lp["conv_b"], lp["gmh"],
        s64, bd, c0, n0, m0, conv0, nb=bx, t=tx, valid=ml_valid, rows=ml_rows, ns=min(bx, ML_SEQS))

    flat2 = lambda a: a.reshape(n, GROUP)
    x2 = _out_ffn(x, mod, row0, flat2(o_fox), flat2(o_diff), o_ml, flat2(o_band), lp["w_out"],
                  lp["g2"], lp["wg"], lp["wu"], lp["wd"], bb=bb, tt=tt, tf=FFN_CHUNK, layer=layer)

    mc = jnp.stack([c_new[:, HEAD_DIM * h:HEAD_DIM * h + HEAD_DIM, HEAD_DIM * h:HEAD_DIM * h + HEAD_DIM]
                    for h in range(N_HEADS)], axis=1)
    small = (r3(z["gates"])[:, :, 0:N_HEADS], mc, n_new.reshape(bx, N_HEADS, HEAD_DIM),
             m_new.reshape(bx, N_HEADS, HEAD_DIM)[:, :, 0], r3(z["mqk"])[:, tx - (CONV_W - 1):, :])
    return x2, big, small


def _channel_major(c):
    perm = (0, 1) + tuple(range(3, c.ndim)) + (2,)
    return jnp.transpose(c, perm).reshape(c.shape[0], c.shape[1], -1, c.shape[2])


def _token_major(a, inner):
    d, b, _, t = a.shape
    k = len(inner)
    return jnp.transpose(a.reshape(d, b, *inner, t), (0, 1, 2 + k) + tuple(range(2, 2 + k)))


def kernel(x_prompt, x_sample, c_prompt, c_sample, cache_fox_k, cache_fox_v, cache_fox_logf, cache_diff_k, cache_diff_v, cache_band_k, cache_band_v, state_mlstm_c, state_mlstm_n, state_mlstm_m, state_conv, norm1_g, norm2_g, w_mod, b_mod, w_in, b_in, qk_g_fox, qk_g_diff, qk_g_band, conv_w, conv_b, diff_lambda, diff_subln_g, mlstm_norm_g, band_rel_bias, w_out, w_ffn_gate, w_ffn_up, w_ffn_down):
    depth = w_in.shape[0]
    bp, bs = x_prompt.shape[0], x_sample.shape[0]
    ts = x_sample.shape[1]
    caches = tuple(_channel_major(c) for c in (cache_fox_k, cache_fox_v, cache_fox_logf, cache_diff_k,
                                               cache_diff_v, cache_band_k, cache_band_v))
    caches += (state_mlstm_c, state_mlstm_n, state_mlstm_m, state_conv)
    mod = _modulation(jnp.concatenate([c_sample, c_prompt], axis=0), w_mod, b_mod)
    mod = mod.reshape(depth, bp + bs, 6, 1, D_MODEL)
    w_in_g = _regroup_w_in(w_in)
    big_w = [w.astype(BF16) for w in (w_out, w_ffn_gate, w_ffn_up, w_ffn_down)]
    layers = [_prep_layer(l, norm1_g, norm2_g, w_in_g, b_in, qk_g_fox, qk_g_diff, qk_g_band, conv_w,
                          conv_b, diff_lambda, diff_subln_g, mlstm_norm_g, band_rel_bias, *big_w)
              for l in range(depth)]
    sample_bb = 512 // ts
    xp, xs = x_prompt, x_sample
    p_big, p_small, s_big, s_small = None, [], [], []
    for l in range(depth):
        xp, p_big, sm = _layer(xp, mod, bs, layers[l], None, bb=1, tt=512, layer=l, depth=depth,
                               prev_state=p_big)
        p_small.append(sm)
    for l in range(depth):
        xs, big, sm = _layer(xs, mod, 0, layers[l], caches[:7] + tuple(c[l] for c in caches[7:]),
                             bb=sample_bb, tt=ts,
                             layer=l, depth=depth)
        s_big.append(big)
        s_small.append(sm)
    head = (N_HEADS, HEAD_DIM)
    half = (N_HEADS, 2, DIFF_HALF)
    inner = (head, head, half, head, head, head)
    p_fk, p_fv, p_dk, p_dv, p_bk, p_bv = [_token_major(a, inn) for a, inn in zip(p_big, inner)]
    p_flf, p_c, p_n, p_m, p_conv = [jnp.stack(zs) for zs in zip(*p_small)]
    s_fk, s_fv, s_dk, s_dv, s_bk, s_bv = [jnp.stack(zs).reshape((depth, bs, ts) + inn)
                                          for zs, inn in zip(zip(*s_big), inner)]
    s_flf, s_c, s_n, s_m, s_conv = [jnp.stack(zs) for zs in zip(*s_small)]
    return (xp, xs, p_fk, p_fv, p_flf, p_dk, p_dv, p_bk, p_bv, p_c, p_n, p_m, p_conv,
            s_fk, s_fv, s_flf, s_dk, s_dv, s_bk, s_bv, s_c, s_n, s_m, s_conv)
```
